```python
import math
import jax, jax.numpy as jnp
from jax import lax
import numpy as np

D_MODEL = 4096
BATCH = 4
SEQ = 2048
DEPTH = 1

EPS = 1e-6
ML_HEADS = 8
ML_QK = D_MODEL // 32
ML_V = D_MODEL // 16
ML_QK_W = ML_HEADS * ML_QK
ML_V_W = ML_HEADS * ML_V
ML_CHUNK = 128
CONV_W = 4
SB_HEADS = 16
SB_HD = D_MODEL // 32
SB_W = SB_HEADS * SB_HD
SB_BLOCK = 128
N_MEM = 256
XA_HEADS = 4
XA_HD = D_MODEL // XA_HEADS
N_GROUPS = 8
EXPERTS_PER_GROUP = 8
N_EXPERTS = N_GROUPS * EXPERTS_PER_GROUP
TOP_K = 2
D_EXPERT = D_MODEL // 8
MOE_BLOCK = 128
IN_SIZES = (ML_QK_W, ML_QK_W, ML_V_W, ML_V_W, ML_HEADS, ML_HEADS, SB_W, SB_W, SB_W, D_MODEL, D_MODEL)
IN_TOTAL = 2 * ML_QK_W + 2 * ML_V_W + 2 * ML_HEADS + 3 * SB_W + 2 * D_MODEL

kernel_name = "hybrid_mlstm_stickbreak_hmoe_block"


def rmsnorm(x, g):
    xf = x.astype(jnp.float32)
    y = xf * lax.rsqrt(jnp.mean(xf * xf, axis=-1, keepdims=True) + EPS)
    return (y * g.astype(jnp.float32)).astype(x.dtype)


def causal_depthwise_conv(x, w):
    c = x.shape[-1]
    return lax.conv_general_dilated(
        x, w[:, None, :].astype(x.dtype), window_strides=(1,),
        padding=[(CONV_W - 1, 0)], dimension_numbers=('NWC', 'WIO', 'NWC'),
        feature_group_count=c)


def to_heads(a, n_heads):
    b, s, w = a.shape
    return a.reshape(b, s, n_heads, w // n_heads).transpose(0, 2, 1, 3)


def mlstm_chunkwise(q, k, v, i_pre, f_pre):
    b_, h_, s_, dk = q.shape
    dv = v.shape[-1]
    nc = s_ // ML_CHUNK
    qf = q.astype(jnp.float32)
    kf = k.astype(jnp.float32) / math.sqrt(dk)
    vf = v.astype(jnp.float32)
    logi = i_pre.astype(jnp.float32)
    logf = jax.nn.log_sigmoid(f_pre.astype(jnp.float32))

    def to_chunks(a):
        a = a.reshape(a.shape[:2] + (nc, ML_CHUNK) + a.shape[3:])
        return jnp.moveaxis(a, 2, 0)

    xs = tuple(to_chunks(a) for a in (qf, kf, vf, logi, logf))
    causal = jnp.tril(jnp.ones((ML_CHUNK, ML_CHUNK), dtype=bool))

    def step(carry, chunk):
        c_st, n_st, m_st = carry
        qb, kb, vb, ib, fb = chunk
        bcum = jnp.cumsum(fb, axis=-1)
        g = bcum[..., -1]
        d_intra = bcum[..., :, None] - bcum[..., None, :] + ib[..., None, :]
        d_intra = jnp.where(causal, d_intra, -jnp.inf)
        inter = bcum + m_st[..., None]
        m_t = jnp.maximum(inter, jnp.max(d_intra, axis=-1))
        w_intra = jnp.exp(d_intra - m_t[..., None])
        w_inter = jnp.exp(inter - m_t)
        sc = jnp.einsum('bhtd,bhsd->bhts', qb, kb) * w_intra
        num = (w_inter[..., None] * jnp.einsum('bhtd,bhde->bhte', qb, c_st)
               + jnp.einsum('bhts,bhse->bhte', sc, vb))
        den = w_inter * jnp.einsum('bhtd,bhd->bht', qb, n_st) + jnp.sum(sc, axis=-1)
        h = num / jnp.maximum(jnp.abs(den), jnp.exp(-m_t))[..., None]
        d_state = g[..., None] - bcum + ib
        m_new = jnp.maximum(g + m_st, jnp.max(d_state, axis=-1))
        w_s = jnp.exp(d_state - m_new[..., None])
        decay = jnp.exp(g + m_st - m_new)
        c_new = decay[..., None, None] * c_st + jnp.einsum('bhs,bhsd,bhse->bhde', w_s, kb, vb)
        n_new = decay[..., None] * n_st + jnp.einsum('bhs,bhsd->bhd', w_s, kb)
        return (c_new, n_new, m_new), h

    init = (jnp.zeros((b_, h_, dk, dv), jnp.float32),
            jnp.zeros((b_, h_, dk), jnp.float32),
            jnp.full((b_, h_), -1e30, jnp.float32))
    _, hs = lax.scan(step, init, xs)
    return jnp.moveaxis(hs, 0, 2).reshape(b_, h_, s_, dv)


def stick_breaking_attention(q, k, v):
    s_ = q.shape[2]
    scale = 1.0 / math.sqrt(q.shape[-1])
    outs = []
    for blk in range(s_ // SB_BLOCK):
        t0 = blk * SB_BLOCK
        kv_len = t0 + SB_BLOCK
        qb = q[:, :, t0:kv_len]
        kb = k[:, :, :kv_len]
        vb = v[:, :, :kv_len]
        z = jnp.einsum('bhtd,bhsd->bhts', qb, kb).astype(jnp.float32) * scale
        t_idx = t0 + jnp.arange(SB_BLOCK)[:, None]
        s_idx = jnp.arange(kv_len)[None, :]
        strict = s_idx < t_idx
        log_1m = jnp.where(strict, jax.nn.log_sigmoid(-z), 0.0)
        rest = lax.cumsum(log_1m, axis=3, reverse=True) - log_1m
        a = jnp.where(strict, jnp.exp(jax.nn.log_sigmoid(z) + rest), 0.0)
        outs.append(jnp.einsum('bhts,bhsd->bhtd', a.astype(vb.dtype), vb))
    return jnp.concatenate(outs, axis=2)


def hybrid_mixer(xn, w_in, conv_qk, b_gates, g_mlstm, w_proj_a, w_proj_b, w_out):
    b_, s_, _ = xn.shape
    proj = xn @ w_in
    split_pts = np.cumsum(IN_SIZES)[:-1].tolist()
    q_m, k_m, v_m, o_m, i_m, f_m, q_s, k_s, v_s, gate_a, gate_b = jnp.split(proj, split_pts, axis=-1)
    qk = jax.nn.silu(causal_depthwise_conv(jnp.concatenate([q_m, k_m], axis=-1), conv_qk))
    q_m, k_m = jnp.split(qk, [ML_QK_W], axis=-1)
    i_pre = (i_m + b_gates[:ML_HEADS].astype(i_m.dtype)).transpose(0, 2, 1)
    f_pre = (f_m + b_gates[ML_HEADS:].astype(f_m.dtype)).transpose(0, 2, 1)
    hm = mlstm_chunkwise(to_heads(q_m, ML_HEADS), to_heads(k_m, ML_HEADS),
                         to_heads(v_m, ML_HEADS), i_pre, f_pre)
    hm = hm * lax.rsqrt(jnp.mean(hm * hm, axis=-1, keepdims=True) + EPS)
    hm = hm.transpose(0, 2, 1, 3).reshape(b_, s_, ML_V_W) * g_mlstm.astype(jnp.float32)
    hm = (jax.nn.sigmoid(o_m.astype(jnp.float32)) * hm).astype(xn.dtype)
    y_a = hm @ w_proj_a
    hs = stick_breaking_attention(to_heads(q_s, SB_HEADS), to_heads(k_s, SB_HEADS), to_heads(v_s, SB_HEADS))
    y_b = hs.transpose(0, 2, 1, 3).reshape(b_, s_, SB_W) @ w_proj_b
    y = jax.nn.sigmoid(gate_a) * y_a + jax.nn.sigmoid(gate_b) * y_b
    return y @ w_out


def memory_cross_attention(hn, mem, g_mem, w_q, w_kv, w_o):
    b_, s_, _ = hn.shape
    memn = rmsnorm(mem, g_mem)
    q = (hn @ w_q).reshape(b_, s_, XA_HEADS, XA_HD)
    k, v = jnp.split(memn @ w_kv, 2, axis=-1)
    k = k.reshape(b_, N_MEM, XA_HEADS, XA_HD)
    v = v.reshape(b_, N_MEM, XA_HEADS, XA_HD)
    sc = jnp.einsum('bshd,bmhd->bhsm', q, k).astype(jnp.float32) / math.sqrt(XA_HD)
    p = jax.nn.softmax(sc, axis=-1)
    o = jnp.einsum('bhsm,bmhd->bshd', p.astype(v.dtype), v).reshape(b_, s_, D_MODEL)
    return o @ w_o


def hierarchical_moe(hn, w_router_group, b_router_group, w_router_expert, b_router_expert,
                     w_gate, w_up, w_down):
    b_, s_, d_ = hn.shape
    t_ = b_ * s_
    xt = hn.reshape(t_, d_)
    g_logits = (xt @ w_router_group).astype(jnp.float32) + b_router_group.astype(jnp.float32)
    g_prob = jax.nn.softmax(g_logits, axis=-1)
    g_w, g_idx = lax.top_k(g_prob, 1)
    e_logits = ((xt @ w_router_expert).astype(jnp.float32)
                + b_router_expert.astype(jnp.float32)).reshape(t_, N_GROUPS, EXPERTS_PER_GROUP)
    e_sel = e_logits[jnp.arange(t_), g_idx[:, 0]]
    e_w, e_idx = lax.top_k(jax.nn.softmax(e_sel, axis=-1), TOP_K)
    e_w = e_w / jnp.sum(e_w, axis=-1, keepdims=True)
    weights = g_w * e_w
    expert_id = g_idx * EXPERTS_PER_GROUP + e_idx
    n_assign = t_ * TOP_K
    flat_e = expert_id.reshape(n_assign).astype(jnp.int32)
    flat_tok = jnp.repeat(jnp.arange(t_, dtype=jnp.int32), TOP_K)
    flat_w = weights.reshape(n_assign)
    order = jnp.argsort(flat_e)
    se, stok, sw = flat_e[order], flat_tok[order], flat_w[order]
    counts = jax.ops.segment_sum(jnp.ones_like(se), se, num_segments=N_EXPERTS)
    starts = jnp.cumsum(counts) - counts
    padded = (counts + MOE_BLOCK - 1) // MOE_BLOCK * MOE_BLOCK
    pad_ends = jnp.cumsum(padded)
    pad_starts = pad_ends - padded
    dest = pad_starts[se] + (jnp.arange(n_assign, dtype=jnp.int32) - starts[se])
    n_blocks = (n_assign + MOE_BLOCK - 1) // MOE_BLOCK + N_EXPERTS
    n_rows = n_blocks * MOE_BLOCK
    buf_tok = jnp.zeros((n_rows,), jnp.int32).at[dest].set(stok)
    buf_w = jnp.zeros((n_rows,), xt.dtype).at[dest].set(sw.astype(xt.dtype))
    block_start = jnp.arange(n_blocks, dtype=jnp.int32) * MOE_BLOCK
    block_expert = jnp.minimum(jnp.searchsorted(pad_ends, block_start, side='right'),
                               N_EXPERTS - 1).astype(jnp.int32)

    def expert_block(args):
        tok, w, e = args
        xb = xt[tok]
        hb = jax.nn.silu(xb @ w_gate[e]) * (xb @ w_up[e])
        return (hb @ w_down[e]) * w[:, None]

    yb = lax.map(expert_block, (buf_tok.reshape(n_blocks, MOE_BLOCK),
                                buf_w.reshape(n_blocks, MOE_BLOCK), block_expert))
    y = jnp.zeros((t_, d_), xt.dtype).at[buf_tok].add(yb.reshape(n_rows, d_))
    return y.reshape(b_, s_, d_)


def setup_inputs(seed: int = 0) -> dict:
    key = jax.random.key(seed)
    ks = jax.random.split(key, 24)
    f32 = jnp.float32

    def nrm(k, shape, scale):
        return jax.random.normal(k, shape, f32) * scale

    def gain(k, shape):
        return 1.0 + 0.02 * jax.random.normal(k, shape, f32)

    dsc = D_MODEL ** -0.5
    b_if = jnp.concatenate([
        0.1 * jax.random.normal(ks[4], (DEPTH, ML_HEADS), f32),
        jnp.broadcast_to(jnp.linspace(3.0, 6.0, ML_HEADS, dtype=f32), (DEPTH, ML_HEADS))
        + 0.1 * jax.random.normal(ks[5], (DEPTH, ML_HEADS), f32)], axis=-1)
    return {
        "x": nrm(ks[0], (BATCH, SEQ, D_MODEL), 1.0),
        "mem": nrm(ks[1], (BATCH, N_MEM, D_MODEL), 1.0),
        "norm_mix": gain(ks[2], (DEPTH, D_MODEL)),
        "w_in": nrm(ks[3], (DEPTH, D_MODEL, IN_TOTAL), dsc),
        "conv_qk": nrm(ks[6], (DEPTH, CONV_W, 2 * ML_QK_W), CONV_W ** -0.5),
        "b_gates": b_if,
        "g_mlstm": gain(ks[7], (DEPTH, ML_V_W)),
        "w_proj_a": nrm(ks[8], (DEPTH, ML_V_W, D_MODEL), ML_V_W ** -0.5),
        "w_proj_b": nrm(ks[9], (DEPTH, SB_W, D_MODEL), SB_W ** -0.5),
        "w_out": nrm(ks[10], (DEPTH, D_MODEL, D_MODEL), dsc),
        "norm_xattn": gain(ks[11], (DEPTH, D_MODEL)),
        "norm_mem": gain(ks[12], (DEPTH, D_MODEL)),
        "w_q_mem": nrm(ks[13], (DEPTH, D_MODEL, D_MODEL), dsc),
        "w_kv_mem": nrm(ks[14], (DEPTH, D_MODEL, 2 * D_MODEL), dsc),
        "w_o_mem": nrm(ks[15], (DEPTH, D_MODEL, D_MODEL), dsc),
        "norm_moe": gain(ks[16], (DEPTH, D_MODEL)),
        "w_router_group": nrm(ks[17], (DEPTH, D_MODEL, N_GROUPS), dsc),
        "b_router_group": nrm(ks[18], (DEPTH, N_GROUPS), 0.01),
        "w_router_expert": nrm(ks[19], (DEPTH, D_MODEL, N_EXPERTS), dsc),
        "b_router_expert": nrm(ks[20], (DEPTH, N_EXPERTS), 0.01),
        "w_gate": nrm(ks[21], (DEPTH, N_EXPERTS, D_MODEL, D_EXPERT), dsc),
        "w_up": nrm(ks[22], (DEPTH, N_EXPERTS, D_MODEL, D_EXPERT), dsc),
        "w_down": nrm(ks[23], (DEPTH, N_EXPERTS, D_EXPERT, D_MODEL), D_EXPERT ** -0.5),
        "norm_final": gain(jax.random.fold_in(key, 99), (D_MODEL,)),
    }


def reference(x, mem, norm_mix, w_in, conv_qk, b_gates, g_mlstm, w_proj_a, w_proj_b, w_out,
              norm_xattn, norm_mem, w_q_mem, w_kv_mem, w_o_mem, norm_moe,
              w_router_group, b_router_group, w_router_expert, b_router_expert,
              w_gate, w_up, w_down, norm_final):
    h = x
    for l in range(DEPTH):
        h = h + hybrid_mixer(rmsnorm(h, norm_mix[l]), w_in[l], conv_qk[l], b_gates[l],
                             g_mlstm[l], w_proj_a[l], w_proj_b[l], w_out[l])
        h = h + memory_cross_attention(rmsnorm(h, norm_xattn[l]), mem, norm_mem[l],
                                       w_q_mem[l], w_kv_mem[l], w_o_mem[l])
        h = h + hierarchical_moe(rmsnorm(h, norm_moe[l]), w_router_group[l], b_router_group[l],
                                 w_router_expert[l], b_router_expert[l],
                                 w_gate[l], w_up[l], w_down[l])
    return rmsnorm(h, norm_final)
```

```python
import functools
import math

import jax
import jax.numpy as jnp
from jax import lax
from jax.experimental import pallas as pl
from jax.experimental.pallas import tpu as pltpu

F32 = jnp.float32
BF16 = jnp.bfloat16
I32 = jnp.int32

EPS = 1e-6
ML_HEADS = 8
ML_QK = 128
ML_V = 256
ML_CHUNK = 128
CONV_W = 4
SB_HEADS = 16
SB_HD = 128
SB_BLOCK = 128
XA_HEADS = 4
N_GROUPS = 8
EXPERTS_PER_GROUP = 8
N_EXPERTS = N_GROUPS * EXPERTS_PER_GROUP
D_EXPERT = 512

LANES = 128
VMEM_LIMIT = 56 * 1024 * 1024

MM_TM = 1024
MM_TN = 512
MOE_BLK = 256
MOE_KS = 4
ROUTE_TM = 256
META_TB = 256
DISPATCH_TB = 256
COMBINE_TB = 128


def _cparams(sem, vmem=VMEM_LIMIT):
    return pltpu.CompilerParams(dimension_semantics=sem, vmem_limit_bytes=vmem)


def _sigmoid(x):
    return 1.0 / (1.0 + jnp.exp(-x))


def _neg_softplus(x):
    return -(jnp.maximum(x, 0.0) + jnp.log1p(jnp.exp(-jnp.abs(x))))


def _rmsnorm_kernel(x_ref, g_ref, o_ref):
    x = x_ref[...].astype(F32)
    ms = jnp.mean(x * x, axis=-1, keepdims=True)
    o_ref[...] = ((x * lax.rsqrt(ms + EPS)) * g_ref[...]).astype(o_ref.dtype)


def _rmsnorm(x, g, out_dtype, tm=256):
    m, d = x.shape
    return pl.pallas_call(
        _rmsnorm_kernel,
        grid=(m // tm,),
        in_specs=[pl.BlockSpec((tm, d), lambda i: (i, 0)),
                  pl.BlockSpec((1, d), lambda i: (0, 0))],
        out_specs=pl.BlockSpec((tm, d), lambda i: (i, 0)),
        out_shape=jax.ShapeDtypeStruct((m, d), out_dtype),
        compiler_params=_cparams(("arbitrary",)),
        name="rmsnorm",
    )(x, g.reshape(1, d).astype(F32))


def _mm_kernel(*refs, has_res):
    if has_res:
        a_ref, w_ref, r_ref, o_ref, w16_ref = refs
    else:
        a_ref, w_ref, o_ref, w16_ref = refs

    @pl.when(pl.program_id(1) == 0)
    def _():
        w16_ref[...] = w_ref[...].astype(BF16)

    acc = jnp.dot(a_ref[...], w16_ref[...], preferred_element_type=F32)
    if has_res:
        acc = r_ref[...] + acc
    o_ref[...] = acc.astype(o_ref.dtype)


def _matmul(a, w, n_cols, out_dtype, res=None, tn=MM_TN, name="matmul"):
    m, k = a.shape
    tm = min(MM_TM, m)
    grid = (n_cols // tn, m // tm)
    in_specs = [pl.BlockSpec((tm, k), lambda j, i: (i, 0)),
                pl.BlockSpec((k, tn), lambda j, i: (0, j))]
    args = [a, w]
    if res is not None:
        in_specs.append(pl.BlockSpec((tm, tn), lambda j, i: (i, j)))
        args.append(res)
    return pl.pallas_call(
        functools.partial(_mm_kernel, has_res=res is not None),
        grid=grid,
        in_specs=in_specs,
        out_specs=pl.BlockSpec((tm, tn), lambda j, i: (i, j)),
        out_shape=jax.ShapeDtypeStruct((m, n_cols), out_dtype),
        scratch_shapes=[pltpu.VMEM((k, tn), BF16)],
        compiler_params=_cparams(("arbitrary", "arbitrary")),
        name=name,
    )(*args)


def _mlstm_kernel(bg_ref, q_ref, k_ref, v_ref, o_ref, gt_ref, cwq_ref, cwk_ref, gm_ref,
                  out_ref, qpad, kpad, bc_s, li_s, *, seq):
    head = pl.program_id(1)
    n_chunks = seq // ML_CHUNK
    L = ML_CHUNK

    qpad[0:8, :] = jnp.zeros((8, ML_QK), F32)
    kpad[0:8, :] = jnp.zeros((8, ML_QK), F32)
    qpad[8:, :] = q_ref[...].astype(F32)
    kpad[8:, :] = k_ref[...].astype(F32)

    li = gt_ref[0] + bg_ref[head]
    lf = _neg_softplus(-(gt_ref[1] + bg_ref[ML_HEADS + head]))
    lane = lax.broadcasted_iota(I32, (n_chunks, L), 1)
    bc = lf
    for s in (1, 2, 4, 8, 16, 32, 64):
        bc = bc + jnp.where(lane >= s, pltpu.roll(bc, s, axis=1), 0.0)
    bc_s[...] = bc
    li_s[...] = li

    rows = lax.broadcasted_iota(I32, (L, L), 0)
    cols = lax.broadcasted_iota(I32, (L, L), 1)
    eye = rows == cols
    causal = cols <= rows
    cwq = cwq_ref[...]
    cwk = cwk_ref[...]
    gm = gm_ref[...]
    k_scale = 1.0 / math.sqrt(ML_QK)

    def to_col(row):
        return jnp.sum(jnp.where(eye, row, 0.0), axis=1, keepdims=True)

    def conv_silu(win, cw):
        y = (cw[0:1, :] * win[5:5 + L] + cw[1:2, :] * win[6:6 + L]
             + cw[2:3, :] * win[7:7 + L] + cw[3:4, :] * win[8:8 + L])
        return y * _sigmoid(y)

    def chunk(c, carry):
        c_st, n_st, m_st = carry
        r0 = pl.multiple_of(c * L, L)
        qb = conv_silu(qpad[pl.ds(r0, L + 8), :], cwq)
        kb = conv_silu(kpad[pl.ds(r0, L + 8), :], cwk) * k_scale
        vb = v_ref[pl.ds(r0, L), :]
        bc_row = bc_s[pl.ds(c, 1), :]
        li_row = li_s[pl.ds(c, 1), :]
        bc_col = to_col(bc_row)

        d = jnp.where(causal, bc_col - bc_row + li_row, -jnp.inf)
        inter = bc_col + m_st
        m_t = jnp.maximum(inter, jnp.max(d, axis=1, keepdims=True))
        w_intra = jnp.exp(d - m_t)
        w_inter = jnp.exp(inter - m_t)

        qb16 = qb.astype(BF16)
        kb16 = kb.astype(BF16)
        sc = lax.dot_general(qb16, kb16, (((1,), (1,)), ((), ())),
                             preferred_element_type=F32) * w_intra
        num = (w_inter * jnp.dot(qb16, c_st.astype(BF16), preferred_element_type=F32)
               + jnp.dot(sc.astype(BF16), vb, preferred_element_type=F32))
        den = (w_inter * jnp.sum(qb * n_st, axis=1, keepdims=True)
               + jnp.sum(sc, axis=1, keepdims=True))
        hh = num / jnp.maximum(jnp.abs(den), jnp.exp(-m_t))

        hn = hh * lax.rsqrt(jnp.mean(hh * hh, axis=1, keepdims=True) + EPS)
        og = _sigmoid(o_ref[pl.ds(r0, L), :].astype(F32))
        out_ref[pl.ds(r0, L), :] = (og * (hn * gm)).astype(out_ref.dtype)

        g = bc_row[:, L - 1:L]
        ds_row = g - bc_row + li_row
        m_new = jnp.maximum(g + m_st, jnp.max(ds_row, axis=1, keepdims=True))
        w_s = jnp.exp(ds_row - m_new)
        decay = jnp.exp(g + m_st - m_new)
        kw = kb * to_col(w_s)
        c_new = decay * c_st + jnp.dot(kw.T.astype(BF16), vb, preferred_element_type=F32)
        n_new = decay * n_st + jnp.sum(kw, axis=0, keepdims=True)
        return c_new, n_new, m_new

    init = (jnp.zeros((ML_QK, ML_V), F32), jnp.zeros((1, ML_QK), F32),
            jnp.full((1, 1), -1e30, F32))
    lax.fori_loop(0, n_chunks, chunk, init)


def _mlstm(seg_a, gates_t, conv_qk, b_gates, g_mlstm):
    b, s, _ = seg_a.shape
    h = ML_HEADS
    n_chunks = s // ML_CHUNK
    qk_blocks = h
    return pl.pallas_call(
        functools.partial(_mlstm_kernel, seq=s),
        grid=(b, h),
        in_specs=[
            pl.BlockSpec(memory_space=pltpu.SMEM),
            pl.BlockSpec((None, s, ML_QK), lambda bi, hi: (bi, 0, hi)),
            pl.BlockSpec((None, s, ML_QK), lambda bi, hi: (bi, 0, qk_blocks + hi)),
            pl.BlockSpec((None, s, ML_V), lambda bi, hi: (bi, 0, qk_blocks + hi)),
            pl.BlockSpec((None, s, ML_V), lambda bi, hi: (bi, 0, 2 * qk_blocks + hi)),
            pl.BlockSpec((None, 2, None, n_chunks, ML_CHUNK), lambda bi, hi: (bi, 0, hi, 0, 0)),
            pl.BlockSpec((CONV_W, ML_QK), lambda bi, hi: (0, hi)),
            pl.BlockSpec((CONV_W, ML_QK), lambda bi, hi: (0, qk_blocks + hi)),
            pl.BlockSpec((1, ML_V), lambda bi, hi: (0, hi)),
        ],
        out_specs=pl.BlockSpec((None, s, ML_V), lambda bi, hi: (bi, 0, hi)),
        out_shape=jax.ShapeDtypeStruct((b, s, h * ML_V), BF16),
        scratch_shapes=[pltpu.VMEM((s + 8, ML_QK), F32), pltpu.VMEM((s + 8, ML_QK), F32),
                        pltpu.VMEM((n_chunks, ML_CHUNK), F32),
                        pltpu.VMEM((n_chunks, ML_CHUNK), F32)],
        compiler_params=_cparams(("arbitrary", "arbitrary")),
        name="mlstm",
    )(b_gates.astype(F32), seg_a, seg_a, seg_a, seg_a, gates_t, conv_qk.astype(F32),
      conv_qk.astype(F32), g_mlstm.reshape(1, -1).astype(F32))


def _sb_kernel(q_ref, k_ref, v_ref, o_ref, *, seq):
    L = SB_BLOCK
    n_blocks = seq // L
    scale = 1.0 / math.sqrt(SB_HD)
    rows = lax.broadcasted_iota(I32, (L, L), 0)
    cols = lax.broadcasted_iota(I32, (L, L), 1)
    strict = cols < rows
    ur = lax.broadcasted_iota(I32, (2 * L, 2 * L), 0) & (L - 1)
    uc = lax.broadcasted_iota(I32, (2 * L, 2 * L), 1)
    u = jnp.where((uc >= L) | (ur > uc), 1.0, 0.0).astype(BF16)

    def block(qi, j, r_acc, acc, diag):
        c0 = pl.multiple_of(j * L, L)
        kj = k_ref[pl.ds(c0, L), :]
        vj = v_ref[pl.ds(c0, L), :]
        z = lax.dot_general(qi, kj, (((1,), (1,)), ((), ())), preferred_element_type=F32) * scale
        l1m = _neg_softplus(z)
        lm = jnp.where(strict, l1m, 0.0) if diag else l1m
        hi = lm.astype(BF16)
        lo = (lm - hi.astype(F32)).astype(BF16)
        cs = jnp.dot(jnp.concatenate([hi, lo], axis=1), u, preferred_element_type=F32)
        rest = cs[:, :L] + r_acc
        a = jnp.exp((z + l1m) + rest)
        if diag:
            a = jnp.where(strict, a, 0.0)
        acc = acc + jnp.dot(a.astype(BF16), vj, preferred_element_type=F32)
        return r_acc + cs[:, L:], acc

    def qblock(i, _):
        q0 = pl.multiple_of(i * L, L)
        qi = q_ref[pl.ds(q0, L), :]
        zeros = jnp.zeros((L, L), F32)
        r_acc, acc = block(qi, i, zeros, zeros, True)

        def inner(jj, carry):
            return block(qi, i - jj, carry[0], carry[1], False)

        r_acc, acc = lax.fori_loop(1, i + 1, inner, (r_acc, acc))
        o_ref[pl.ds(q0, L), :] = acc.astype(o_ref.dtype)
        return 0

    lax.fori_loop(0, n_blocks, qblock, 0)


def _stick_breaking(seg_b):
    b, s, _ = seg_b.shape
    h = SB_HEADS
    return pl.pallas_call(
        functools.partial(_sb_kernel, seq=s),
        grid=(b, h),
        in_specs=[pl.BlockSpec((None, s, SB_HD), lambda bi, hi: (bi, 0, hi)),
                  pl.BlockSpec((None, s, SB_HD), lambda bi, hi: (bi, 0, h + hi)),
                  pl.BlockSpec((None, s, SB_HD), lambda bi, hi: (bi, 0, 2 * h + hi))],
        out_specs=pl.BlockSpec((None, s, SB_HD), lambda bi, hi: (bi, 0, hi)),
        out_shape=jax.ShapeDtypeStruct((b, s, h * SB_HD), BF16),
        compiler_params=_cparams(("arbitrary", "arbitrary")),
        name="stick_breaking",
    )(seg_b, seg_b, seg_b)


def _merge_kernel(hm_ref, hs_ref, wa_ref, wb_ref, ga_ref, gb_ref, o_ref, wa16, wb16):
    @pl.when(pl.program_id(1) == 0)
    def _():
        wa16[...] = wa_ref[...].astype(BF16)
        wb16[...] = wb_ref[...].astype(BF16)

    ya = jnp.dot(hm_ref[...], wa16[...], preferred_element_type=F32)
    yb = jnp.dot(hs_ref[...], wb16[...], preferred_element_type=F32)
    y = _sigmoid(ga_ref[...].astype(F32)) * ya + _sigmoid(gb_ref[...].astype(F32)) * yb
    o_ref[...] = y.astype(o_ref.dtype)


def _merge(hm, hs, w_a, w_b, seg_b, gate_a_col, gate_b_col, d_model):
    m, ka = hm.shape
    kb = hs.shape[1]
    tm, tn = MM_TM, MM_TN
    ga_blk = gate_a_col // tn
    gb_blk = gate_b_col // tn
    return pl.pallas_call(
        _merge_kernel,
        grid=(d_model // tn, m // tm),
        in_specs=[pl.BlockSpec((tm, ka), lambda j, i: (i, 0)),
                  pl.BlockSpec((tm, kb), lambda j, i: (i, 0)),
                  pl.BlockSpec((ka, tn), lambda j, i: (0, j)),
                  pl.BlockSpec((kb, tn), lambda j, i: (0, j)),
                  pl.BlockSpec((tm, tn), lambda j, i: (i, ga_blk + j)),
                  pl.BlockSpec((tm, tn), lambda j, i: (i, gb_blk + j))],
        out_specs=pl.BlockSpec((tm, tn), lambda j, i: (i, j)),
        out_shape=jax.ShapeDtypeStruct((m, d_model), BF16),
        scratch_shapes=[pltpu.VMEM((ka, tn), BF16), pltpu.VMEM((kb, tn), BF16)],
        compiler_params=_cparams(("arbitrary", "arbitrary")),
        name="merge",
    )(hm, hs, w_a, w_b, seg_b, seg_b)


def _xattn_kernel(q_ref, k_ref, v_ref, o_ref, *, scale):
    s = lax.dot_general(q_ref[...], k_ref[...], (((1,), (1,)), ((), ())),
                        preferred_element_type=F32) * scale
    p = jnp.exp(s - jnp.max(s, axis=1, keepdims=True))
    p = p / jnp.sum(p, axis=1, keepdims=True)
    o_ref[...] = jnp.dot(p.astype(BF16), v_ref[...], preferred_element_type=F32).astype(o_ref.dtype)


def _xattn(q, kv, tq=512):
    b, s, d = q.shape
    n_mem = kv.shape[1]
    hd = d // XA_HEADS
    return pl.pallas_call(
        functools.partial(_xattn_kernel, scale=1.0 / math.sqrt(hd)),
        grid=(b, XA_HEADS, s // tq),
        in_specs=[pl.BlockSpec((None, tq, hd), lambda bi, hi, i: (bi, i, hi)),
                  pl.BlockSpec((None, n_mem, hd), lambda bi, hi, i: (bi, 0, hi)),
                  pl.BlockSpec((None, n_mem, hd), lambda bi, hi, i: (bi, 0, XA_HEADS + hi))],
        out_specs=pl.BlockSpec((None, tq, hd), lambda bi, hi, i: (bi, i, hi)),
        out_shape=jax.ShapeDtypeStruct((b, s, d), BF16),
        compiler_params=_cparams(("arbitrary", "arbitrary", "arbitrary")),
        name="xattn",
    )(q, kv, kv)


def _router_kernel(h_ref, g_ref, wr_ref, br_ref, hn_ref, id_ref, wt_ref, wh_ref, wl_ref):
    @pl.when(pl.program_id(0) == 0)
    def _():
        w = wr_ref[...]
        wh = w.astype(BF16)
        wh_ref[...] = wh
        wl_ref[...] = (w - wh.astype(F32)).astype(BF16)

    x = h_ref[...]
    hn = (x * lax.rsqrt(jnp.mean(x * x, axis=-1, keepdims=True) + EPS)) * g_ref[...]
    hn_ref[...] = hn
    xh = hn.astype(BF16)
    xl = (hn - xh.astype(F32)).astype(BF16)
    logits = (jnp.dot(xh, wh_ref[...], preferred_element_type=F32)
              + (jnp.dot(xh, wl_ref[...], preferred_element_type=F32)
                 + jnp.dot(xl, wh_ref[...], preferred_element_type=F32))) + br_ref[...]

    tm = logits.shape[0]
    lane = lax.broadcasted_iota(I32, (tm, LANES), 1)
    lane_f = lane.astype(F32)
    ninf = -jnp.inf

    def first_lane_of(v, vmax):
        return jnp.min(jnp.where(v == vmax, lane_f, float(LANES)), axis=1, keepdims=True)

    gl = jnp.where(lane < N_GROUPS, logits, ninf)
    gmax = jnp.max(gl, axis=1, keepdims=True)
    g_w = 1.0 / jnp.sum(jnp.exp(gl - gmax), axis=1, keepdims=True)
    g_idx = first_lane_of(gl, gmax)
    lo = float(N_GROUPS) + g_idx * float(EXPERTS_PER_GROUP)
    in_group = (lane_f >= lo) & (lane_f < lo + float(EXPERTS_PER_GROUP))
    el = jnp.where(in_group, logits, ninf)
    m1 = jnp.max(el, axis=1, keepdims=True)
    i1 = first_lane_of(el, m1)
    el2 = jnp.where(lane_f == i1, ninf, el)
    m2 = jnp.max(el2, axis=1, keepdims=True)
    i2 = first_lane_of(el2, m2)
    denom = jnp.sum(jnp.exp(el - m1), axis=1, keepdims=True)
    p1 = 1.0 / denom
    p2 = jnp.exp(m2 - m1) / denom
    psum = p1 + p2
    w1 = g_w * (p1 / psum)
    w2 = g_w * (p2 / psum)
    ids = jnp.where(lane == 0, i1 - float(N_GROUPS), jnp.where(lane == 1, i2 - float(N_GROUPS), 0.0))
    id_ref[...] = ids.astype(I32)
    wt_ref[...] = jnp.where(lane == 0, w1, jnp.where(lane == 1, w2, 0.0))


def _router(h, g, w_r, b_r):
    t, d = h.shape
    tm = ROUTE_TM
    return pl.pallas_call(
        _router_kernel,
        grid=(t // tm,),
        in_specs=[pl.BlockSpec((tm, d), lambda i: (i, 0)),
                  pl.BlockSpec((1, d), lambda i: (0, 0)),
                  pl.BlockSpec((d, LANES), lambda i: (0, 0)),
                  pl.BlockSpec((1, LANES), lambda i: (0, 0))],
        out_specs=[pl.BlockSpec((tm, d), lambda i: (i, 0)),
                   pl.BlockSpec((tm, LANES), lambda i: (i, 0)),
                   pl.BlockSpec((tm, LANES), lambda i: (i, 0))],
        out_shape=[jax.ShapeDtypeStruct((t, d), F32),
                   jax.ShapeDtypeStruct((t, LANES), I32),
                   jax.ShapeDtypeStruct((t, LANES), F32)],
        scratch_shapes=[pltpu.VMEM((d, LANES), BF16), pltpu.VMEM((d, LANES), BF16)],
        compiler_params=_cparams(("arbitrary",)),
        name="router",
    )(h, g.reshape(1, d).astype(F32), w_r, b_r)


def _meta_kernel(ids_ref, dest_ref, be_ref, nu_ref, rank_s, *, n_tok, n_blocks):
    tb = META_TB
    lane = lax.broadcasted_iota(I32, (tb, LANES), 1)
    lower = jnp.where(lax.broadcasted_iota(I32, (tb, tb), 0) > lax.broadcasted_iota(I32, (tb, tb), 1),
                      1.0, 0.0).astype(BF16)

    def onehots(b):
        ids = ids_ref[pl.ds(pl.multiple_of(b * tb, tb), tb), :]
        return lane == ids[:, 0:1], lane == ids[:, 1:2]

    def lanes01(v0, v1):
        return jnp.where(lane == 0, v0, jnp.where(lane == 1, v1, 0.0))

    def count(b, carry):
        o1, o2 = onehots(b)
        cnt = jnp.where(o1, 1.0, 0.0) + jnp.where(o2, 1.0, 0.0)
        before = jnp.dot(lower, cnt.astype(BF16), preferred_element_type=F32) + carry
        r1 = jnp.sum(jnp.where(o1, before, 0.0), axis=1, keepdims=True)
        r2 = jnp.sum(jnp.where(o2, before, 0.0), axis=1, keepdims=True)
        rank_s[pl.ds(pl.multiple_of(b * tb, tb), tb), :] = lanes01(r1, r2)
        return carry + jnp.sum(cnt, axis=0, keepdims=True)

    counts = lax.fori_loop(0, n_tok // tb, count, jnp.zeros((1, LANES), F32))

    nblk = jnp.floor((counts + (MOE_BLK - 1)) * (1.0 / MOE_BLK))
    upper = jnp.where(lax.broadcasted_iota(I32, (LANES, LANES), 0)
                      <= lax.broadcasted_iota(I32, (LANES, LANES), 1), 1.0, 0.0).astype(BF16)
    cum_end = jnp.dot(jnp.broadcast_to(nblk, (8, LANES)).astype(BF16), upper,
                      preferred_element_type=F32)[0:1, :]
    row_start = (cum_end - nblk) * float(MOE_BLK)

    def place(b, _):
        o1, o2 = onehots(b)
        s1 = jnp.sum(jnp.where(o1, row_start, 0.0), axis=1, keepdims=True)
        s2 = jnp.sum(jnp.where(o2, row_start, 0.0), axis=1, keepdims=True)
        sl = pl.ds(pl.multiple_of(b * tb, tb), tb)
        dest_ref[sl, :] = (rank_s[sl, :] + lanes01(s1, s2)).astype(I32)
        return 0

    lax.fori_loop(0, n_tok // tb, place, 0)

    blk = lax.broadcasted_iota(I32, (n_blocks, LANES), 0).astype(F32)
    elane = lax.broadcasted_iota(I32, (n_blocks, LANES), 1)
    done = jnp.where((elane < N_EXPERTS) & (cum_end <= blk), 1.0, 0.0)
    be = jnp.minimum(jnp.sum(done, axis=1, keepdims=True), float(N_EXPERTS - 1))
    be_ref[...] = jnp.broadcast_to(be, (n_blocks, LANES)).astype(I32)
    nu_ref[...] = jnp.broadcast_to(cum_end[:, N_EXPERTS - 1:N_EXPERTS], (8, LANES)).astype(I32)


def _moe_meta(ids, n_blocks):
    t = ids.shape[0]
    return pl.pallas_call(
        functools.partial(_meta_kernel, n_tok=t, n_blocks=n_blocks),
        out_shape=[jax.ShapeDtypeStruct((t, LANES), I32),
                   jax.ShapeDtypeStruct((n_blocks, LANES), I32),
                   jax.ShapeDtypeStruct((8, LANES), I32)],
        scratch_shapes=[pltpu.VMEM((t, LANES), F32)],
        compiler_params=pltpu.CompilerParams(vmem_limit_bytes=VMEM_LIMIT),
        name="moe_meta",
    )(ids)


def _dispatch_kernel(dest_ref, x_ref, xs_in_ref, xs_ref, sem):
    del xs_in_ref
    tb = x_ref.shape[0]

    def row_copy(r, d):
        return pltpu.make_async_copy(x_ref.at[pl.ds(r, 1)], xs_ref.at[pl.ds(d, 1)], sem)

    def issue(r, _):
        row_copy(r, dest_ref[0, 0, 2 * r]).start()
        row_copy(r, dest_ref[0, 0, 2 * r + 1]).start()
        return 0

    lax.fori_loop(0, tb, issue, 0)

    def drain(r, _):
        row_copy(r, dest_ref[0, 0, 2 * r]).wait()
        row_copy(r, dest_ref[0, 0, 2 * r + 1]).wait()
        return 0

    lax.fori_loop(0, tb, drain, 0)


def _dispatch(dest_blocks, hn, xs_zero):
    t, d = hn.shape
    tb = DISPATCH_TB
    return pl.pallas_call(
        _dispatch_kernel,
        grid=(t // tb,),
        in_specs=[pl.BlockSpec((1, 1, 2 * tb), lambda i: (i, 0, 0), memory_space=pltpu.SMEM),
                  pl.BlockSpec((tb, d), lambda i: (i, 0)),
                  pl.BlockSpec(memory_space=pl.ANY)],
        out_specs=pl.BlockSpec(memory_space=pl.ANY),
        out_shape=jax.ShapeDtypeStruct(xs_zero.shape, xs_zero.dtype),
        scratch_shapes=[pltpu.SemaphoreType.DMA(())],
        input_output_aliases={2: 0},
        compiler_params=_cparams(("arbitrary",)),
        name="moe_dispatch",
    )(dest_blocks, hn, xs_zero)


def _expert_kernel(be_ref, nu_ref, xs_ref, wg_ref, wu_ref, wd_ref, ys_ref, gacc, uacc):
    del be_ref
    i = pl.program_id(0)
    k = pl.program_id(1)

    @pl.when(i < nu_ref[0])
    def _():
        @pl.when(k == 0)
        def _():
            gacc[...] = jnp.zeros_like(gacc)
            uacc[...] = jnp.zeros_like(uacc)

        x = xs_ref[...].astype(BF16)
        gacc[...] += jnp.dot(x, wg_ref[...].astype(BF16), preferred_element_type=F32)
        uacc[...] += jnp.dot(x, wu_ref[...].astype(BF16), preferred_element_type=F32)

        @pl.when(k == MOE_KS - 1)
        def _():
            g = gacc[...]
            hb = (g * _sigmoid(g)) * uacc[...]
            ys_ref[...] = jnp.dot(hb.astype(BF16), wd_ref[...].astype(BF16),
                                  preferred_element_type=F32)


def _experts(block_expert, n_used, xs, w_gate, w_up, w_down):
    n_rows, d = xs.shape
    n_blocks = n_rows // MOE_BLK
    f = w_gate.shape[-1]
    tk = d // MOE_KS

    def blk(i, nu):
        return jnp.minimum(i, nu[0] - 1)

    def kk(i, k, nu):
        return jnp.where(i < nu[0], k, MOE_KS - 1)

    grid_spec = pltpu.PrefetchScalarGridSpec(
        num_scalar_prefetch=2,
        grid=(n_blocks, MOE_KS),
        in_specs=[
            pl.BlockSpec((MOE_BLK, tk), lambda i, k, be, nu: (blk(i, nu), kk(i, k, nu))),
            pl.BlockSpec((None, tk, f), lambda i, k, be, nu: (be[blk(i, nu)], kk(i, k, nu), 0)),
            pl.BlockSpec((None, tk, f), lambda i, k, be, nu: (be[blk(i, nu)], kk(i, k, nu), 0)),
            pl.BlockSpec((None, f, d), lambda i, k, be, nu: (be[blk(i, nu)], 0, 0)),
        ],
        out_specs=pl.BlockSpec((MOE_BLK, d), lambda i, k, be, nu: (blk(i, nu), 0)),
        scratch_shapes=[pltpu.VMEM((MOE_BLK, f), F32), pltpu.VMEM((MOE_BLK, f), F32)],
    )
    return pl.pallas_call(
        _expert_kernel,
        grid_spec=grid_spec,
        out_shape=jax.ShapeDtypeStruct((n_rows, d), F32),
        input_output_aliases={2: 0},
        compiler_params=_cparams(("arbitrary", "arbitrary")),
        name="moe_experts",
    )(block_expert, n_used, xs, w_gate, w_up, w_down)


def _combine_kernel(dest_ref, w_ref, h_ref, g_ref, ys_ref, o_ref, ybuf, sem, *, final_norm):
    tb = h_ref.shape[0]

    def row_copy(r, k, d):
        return pltpu.make_async_copy(ys_ref.at[pl.ds(d, 1)], ybuf.at[k, pl.ds(r, 1)], sem)

    def issue(r, _):
        row_copy(r, 0, dest_ref[0, 0, 2 * r]).start()
        row_copy(r, 1, dest_ref[0, 0, 2 * r + 1]).start()
        return 0

    lax.fori_loop(0, tb, issue, 0)

    def drain(r, _):
        row_copy(r, 0, dest_ref[0, 0, 2 * r]).wait()
        row_copy(r, 1, dest_ref[0, 0, 2 * r + 1]).wait()
        return 0

    lax.fori_loop(0, tb, drain, 0)

    w = w_ref[...]
    h = h_ref[...] + (w[:, 0:1] * ybuf[0] + w[:, 1:2] * ybuf[1])
    if final_norm:
        h = (h * lax.rsqrt(jnp.mean(h * h, axis=-1, keepdims=True) + EPS)) * g_ref[...]
    o_ref[...] = h


def _combine(dest_blocks, wts, h, g, ys, final_norm):
    t, d = h.shape
    tb = COMBINE_TB
    return pl.pallas_call(
        functools.partial(_combine_kernel, final_norm=final_norm),
        grid=(t // tb,),
        in_specs=[pl.BlockSpec((1, 1, 2 * tb), lambda i: (i, 0, 0), memory_space=pltpu.SMEM),
                  pl.BlockSpec((tb, LANES), lambda i: (i, 0)),
                  pl.BlockSpec((tb, d), lambda i: (i, 0)),
                  pl.BlockSpec((1, d), lambda i: (0, 0)),
                  pl.BlockSpec(memory_space=pl.ANY)],
        out_specs=pl.BlockSpec((tb, d), lambda i: (i, 0)),
        out_shape=jax.ShapeDtypeStruct((t, d), F32),
        scratch_shapes=[pltpu.VMEM((2, tb, d), F32), pltpu.SemaphoreType.DMA(())],
        compiler_params=_cparams(("arbitrary",)),
        name="moe_combine",
    )(dest_blocks, wts, h, g.reshape(1, d).astype(F32), ys)


def kernel(x, mem, norm_mix, w_in, conv_qk, b_gates, g_mlstm, w_proj_a, w_proj_b, w_out,
           norm_xattn, norm_mem, w_q_mem, w_kv_mem, w_o_mem, norm_moe,
           w_router_group, b_router_group, w_router_expert, b_router_expert,
           w_gate, w_up, w_down, norm_final):
    b, s, d = x.shape
    t = b * s
    n_mem = mem.shape[1]
    depth = w_in.shape[0]
    ml_qk_w = ML_HEADS * ML_QK
    ml_v_w = ML_HEADS * ML_V
    sb_w = SB_HEADS * SB_HD
    seg_a_w = 2 * ml_qk_w + 2 * ml_v_w
    n_gate_cols = 2 * ML_HEADS
    seg_b_w = 3 * sb_w + 2 * d
    n_blocks = (t * 2) // MOE_BLK + N_EXPERTS
    n_rows = n_blocks * MOE_BLK

    h = x.reshape(t, d)
    mem2 = mem.reshape(b * n_mem, d)
    for l in range(depth):
        xn = _rmsnorm(h, norm_mix[l], BF16)
        w_in_l = w_in[l]
        seg_a = _matmul(xn, w_in_l, seg_a_w, BF16, name="in_proj_a")
        w_if = jnp.pad(w_in_l[:, seg_a_w:seg_a_w + n_gate_cols], ((0, 0), (0, LANES - n_gate_cols)))
        gates = _matmul(xn, w_if, LANES, F32, tn=LANES, name="in_proj_gates")
        seg_b = _matmul(xn, w_in_l[:, seg_a_w + n_gate_cols:], seg_b_w, BF16, name="in_proj_b")
        gates_t = gates[:, :n_gate_cols].reshape(b, s, 2, ML_HEADS).transpose(0, 2, 3, 1)
        gates_t = gates_t.reshape(b, 2, ML_HEADS, s // ML_CHUNK, ML_CHUNK)
        hm = _mlstm(seg_a.reshape(b, s, seg_a_w), gates_t, conv_qk[l], b_gates[l], g_mlstm[l])
        hs = _stick_breaking(seg_b.reshape(b, s, seg_b_w))
        y = _merge(hm.reshape(t, ml_v_w), hs.reshape(t, sb_w), w_proj_a[l], w_proj_b[l],
                   seg_b, 3 * sb_w, 3 * sb_w + d, d)
        h = _matmul(y, w_out[l], d, F32, res=h, name="out_proj")
        hn = _rmsnorm(h, norm_xattn[l], BF16)
        q = _matmul(hn, w_q_mem[l], d, BF16, name="xattn_q")
        memn = _rmsnorm(mem2, norm_mem[l], BF16)
        kv = _matmul(memn, w_kv_mem[l], 2 * d, BF16, name="xattn_kv")
        o = _xattn(q.reshape(b, s, d), kv.reshape(b, n_mem, 2 * d))
        h = _matmul(o.reshape(t, d), w_o_mem[l], d, F32, res=h, name="xattn_o")
        w_r = jnp.pad(jnp.concatenate([w_router_group[l], w_router_expert[l]], axis=1),
                      ((0, 0), (0, LANES - N_GROUPS - N_EXPERTS)))
        b_r = jnp.pad(jnp.concatenate([b_router_group[l], b_router_expert[l]]),
                      (0, LANES - N_GROUPS - N_EXPERTS)).reshape(1, LANES).astype(F32)
        hn3, ids, wts = _router(h, norm_moe[l], w_r, b_r)
        dest, block_expert, n_used = _moe_meta(ids, n_blocks)
        xs = _dispatch(dest[:, :2].reshape(t // DISPATCH_TB, 1, 2 * DISPATCH_TB), hn3,
                       jnp.zeros((n_rows, d), F32))
        ys = _experts(block_expert[:, 0], n_used[0, :1], xs, w_gate[l], w_up[l], w_down[l])
        h = _combine(dest[:, :2].reshape(t // COMBINE_TB, 1, 2 * COMBINE_TB), wts, h,
                     norm_final, ys, final_norm=(l == depth - 1))
    return h.reshape(b, s, d)
```

```python
import functools
import math

import jax
import jax.numpy as jnp
from jax import lax
from jax.experimental import pallas as pl
from jax.experimental.pallas import tpu as pltpu

F32 = jnp.float32
BF16 = jnp.bfloat16
I32 = jnp.int32

EPS = 1e-6
ML_HEADS = 8
ML_QK = 128
ML_V = 256
ML_CHUNK = 128
CONV_W = 4
SB_HEADS = 16
SB_HD = 128
SB_BLOCK = 128
XA_HEADS = 4
N_GROUPS = 8
EXPERTS_PER_GROUP = 8
N_EXPERTS = N_GROUPS * EXPERTS_PER_GROUP
D_EXPERT = 512

LANES = 128
VMEM_LIMIT = 56 * 1024 * 1024

SB_TQ = 512
SB_KB = 256
MM_TM = 1024
MM_TN = 512
MM_CAST_ROWS = 512
MOE_BLK = 256
MOE_KS = 4
ROUTE_TM = 256
META_TB = 256
DISPATCH_TB = 256
COMBINE_TB = 128


def _cparams(sem, vmem=VMEM_LIMIT):
    return pltpu.CompilerParams(dimension_semantics=sem, vmem_limit_bytes=vmem)


def _sigmoid(x):
    return 1.0 / (1.0 + jnp.exp(-x))


def _neg_softplus(x):
    return -(jnp.maximum(x, 0.0) + jnp.log1p(jnp.exp(-jnp.abs(x))))


def _rmsnorm_kernel(x_ref, g_ref, o_ref):
    x = x_ref[...].astype(F32)
    ms = jnp.mean(x * x, axis=-1, keepdims=True)
    o_ref[...] = ((x * lax.rsqrt(ms + EPS)) * g_ref[...]).astype(o_ref.dtype)


def _rmsnorm(x, g, out_dtype, tm=256):
    m, d = x.shape
    return pl.pallas_call(
        _rmsnorm_kernel,
        grid=(m // tm,),
        in_specs=[pl.BlockSpec((tm, d), lambda i: (i, 0)),
                  pl.BlockSpec((1, d), lambda i: (0, 0))],
        out_specs=pl.BlockSpec((tm, d), lambda i: (i, 0)),
        out_shape=jax.ShapeDtypeStruct((m, d), out_dtype),
        compiler_params=_cparams(("arbitrary",)),
        name="rmsnorm",
    )(x, g.reshape(1, d).astype(F32))


def _mm_kernel(*refs, has_res, shift):
    a_ref, w_ref = refs[0], refs[1]
    wn_ref = refs[2] if shift else None
    r_ref = refs[2 + bool(shift)] if has_res else None
    o_ref, w16_ref = refs[-2], refs[-1]
    k, tn = w16_ref.shape

    @pl.when(pl.program_id(1) == 0)
    def _():
        if shift:
            for r0 in range(0, k, MM_CAST_ROWS):
                rows = pl.ds(r0, MM_CAST_ROWS)
                wide = jnp.concatenate([w_ref[rows, :], wn_ref[rows, :]], axis=1)
                w16_ref[rows, :] = wide[:, shift:shift + tn].astype(BF16)
        else:
            w16_ref[...] = w_ref[...].astype(BF16)

    acc = jnp.dot(a_ref[...], w16_ref[...], preferred_element_type=F32)
    if has_res:
        acc = r_ref[...] + acc
    o_ref[...] = acc.astype(o_ref.dtype)


def _matmul(a, w, n_cols, out_dtype, res=None, tn=MM_TN, col_start=0, name="matmul"):
    m, k = a.shape
    tm = min(MM_TM, m)
    grid = (n_cols // tn, m // tm)
    shift = col_start % LANES
    base_blk, rem = divmod(col_start - shift, tn)
    assert rem == 0
    in_specs = [pl.BlockSpec((tm, k), lambda j, i: (i, 0)),
                pl.BlockSpec((k, tn), lambda j, i: (0, base_blk + j))]
    args = [a, w]
    if shift:
        lane_tiles = tn // LANES
        in_specs.append(pl.BlockSpec((k, LANES), lambda j, i: (0, (base_blk + j + 1) * lane_tiles)))
        args.append(w)
    if res is not None:
        in_specs.append(pl.BlockSpec((tm, tn), lambda j, i: (i, j)))
        args.append(res)
    return pl.pallas_call(
        functools.partial(_mm_kernel, has_res=res is not None, shift=shift),
        grid=grid,
        in_specs=in_specs,
        out_specs=pl.BlockSpec((tm, tn), lambda j, i: (i, j)),
        out_shape=jax.ShapeDtypeStruct((m, n_cols), out_dtype),
        scratch_shapes=[pltpu.VMEM((k, tn), BF16)],
        compiler_params=_cparams(("arbitrary", "arbitrary")),
        name=name,
    )(*args)


def _mlstm_kernel(bg_ref, q_ref, k_ref, v_ref, o_ref, gt_ref, cwq_ref, cwk_ref, gm_ref,
                  out_ref, qpad, kpad, bc_s, li_s, *, seq):
    head = pl.program_id(1)
    n_chunks = seq // ML_CHUNK
    L = ML_CHUNK

    qpad[0:8, :] = jnp.zeros((8, ML_QK), F32)
    kpad[0:8, :] = jnp.zeros((8, ML_QK), F32)
    qpad[8:, :] = q_ref[...].astype(F32)
    kpad[8:, :] = k_ref[...].astype(F32)

    li = gt_ref[0] + bg_ref[head]
    lf = _neg_softplus(-(gt_ref[1] + bg_ref[ML_HEADS + head]))
    lane = lax.broadcasted_iota(I32, (n_chunks, L), 1)
    bc = lf
    for s in (1, 2, 4, 8, 16, 32, 64):
        bc = bc + jnp.where(lane >= s, pltpu.roll(bc, s, axis=1), 0.0)
    bc_s[...] = bc
    li_s[...] = li

    rows = lax.broadcasted_iota(I32, (L, L), 0)
    cols = lax.broadcasted_iota(I32, (L, L), 1)
    eye = rows == cols
    causal = cols <= rows
    cwq = cwq_ref[...]
    cwk = cwk_ref[...]
    gm = gm_ref[...]
    k_scale = 1.0 / math.sqrt(ML_QK)

    def to_col(row):
        return jnp.sum(jnp.where(eye, row, 0.0), axis=1, keepdims=True)

    def conv_silu(win, cw):
        y = (cw[0:1, :] * win[5:5 + L] + cw[1:2, :] * win[6:6 + L]
             + cw[2:3, :] * win[7:7 + L] + cw[3:4, :] * win[8:8 + L])
        return y * _sigmoid(y)

    def chunk(c, carry):
        c_st, n_st, m_st = carry
        r0 = pl.multiple_of(c * L, L)
        qb = conv_silu(qpad[pl.ds(r0, L + 8), :], cwq)
        kb = conv_silu(kpad[pl.ds(r0, L + 8), :], cwk) * k_scale
        vb = v_ref[pl.ds(r0, L), :]
        bc_row = bc_s[pl.ds(c, 1), :]
        li_row = li_s[pl.ds(c, 1), :]
        bc_col = to_col(bc_row)

        d = jnp.where(causal, bc_col - bc_row + li_row, -jnp.inf)
        inter = bc_col + m_st
        m_t = jnp.maximum(inter, jnp.max(d, axis=1, keepdims=True))
        w_intra = jnp.exp(d - m_t)
        w_inter = jnp.exp(inter - m_t)

        qb16 = qb.astype(BF16)
        kb16 = kb.astype(BF16)
        sc = lax.dot_general(qb16, kb16, (((1,), (1,)), ((), ())),
                             preferred_element_type=F32) * w_intra
        num = (w_inter * jnp.dot(qb16, c_st.astype(BF16), preferred_element_type=F32)
               + jnp.dot(sc.astype(BF16), vb, preferred_element_type=F32))
        den = (w_inter * jnp.sum(qb * n_st, axis=1, keepdims=True)
               + jnp.sum(sc, axis=1, keepdims=True))
        hh = num / jnp.maximum(jnp.abs(den), jnp.exp(-m_t))

        hn = hh * lax.rsqrt(jnp.mean(hh * hh, axis=1, keepdims=True) + EPS)
        og = _sigmoid(o_ref[pl.ds(r0, L), :].astype(F32))
        out_ref[pl.ds(r0, L), :] = (og * (hn * gm)).astype(out_ref.dtype)

        g = bc_row[:, L - 1:L]
        ds_row = g - bc_row + li_row
        m_new = jnp.maximum(g + m_st, jnp.max(ds_row, axis=1, keepdims=True))
        w_s = jnp.exp(ds_row - m_new)
        decay = jnp.exp(g + m_st - m_new)
        kw = kb * to_col(w_s)
        c_new = decay * c_st + jnp.dot(kw.T.astype(BF16), vb, preferred_element_type=F32)
        n_new = decay * n_st + jnp.sum(kw, axis=0, keepdims=True)
        return c_new, n_new, m_new

    init = (jnp.zeros((ML_QK, ML_V), F32), jnp.zeros((1, ML_QK), F32),
            jnp.full((1, 1), -1e30, F32))
    lax.fori_loop(0, n_chunks, chunk, init)


def _mlstm(seg_a, gates_t, conv_qk, b_gates, g_mlstm):
    b, s, _ = seg_a.shape
    h = ML_HEADS
    n_chunks = s // ML_CHUNK
    qk_blocks = h
    return pl.pallas_call(
        functools.partial(_mlstm_kernel, seq=s),
        grid=(b, h),
        in_specs=[
            pl.BlockSpec(memory_space=pltpu.SMEM),
            pl.BlockSpec((None, s, ML_QK), lambda bi, hi: (bi, 0, hi)),
            pl.BlockSpec((None, s, ML_QK), lambda bi, hi: (bi, 0, qk_blocks + hi)),
            pl.BlockSpec((None, s, ML_V), lambda bi, hi: (bi, 0, qk_blocks + hi)),
            pl.BlockSpec((None, s, ML_V), lambda bi, hi: (bi, 0, 2 * qk_blocks + hi)),
            pl.BlockSpec((None, 2, None, n_chunks, ML_CHUNK), lambda bi, hi: (bi, 0, hi, 0, 0)),
            pl.BlockSpec((CONV_W, ML_QK), lambda bi, hi: (0, hi)),
            pl.BlockSpec((CONV_W, ML_QK), lambda bi, hi: (0, qk_blocks + hi)),
            pl.BlockSpec((1, ML_V), lambda bi, hi: (0, hi)),
        ],
        out_specs=pl.BlockSpec((None, s, ML_V), lambda bi, hi: (bi, 0, hi)),
        out_shape=jax.ShapeDtypeStruct((b, s, h * ML_V), BF16),
        scratch_shapes=[pltpu.VMEM((s + 8, ML_QK), F32), pltpu.VMEM((s + 8, ML_QK), F32),
                        pltpu.VMEM((n_chunks, ML_CHUNK), F32),
                        pltpu.VMEM((n_chunks, ML_CHUNK), F32)],
        compiler_params=_cparams(("arbitrary", "arbitrary")),
        name="mlstm",
    )(b_gates.astype(F32), seg_a, seg_a, seg_a, seg_a, gates_t, conv_qk.astype(F32),
      conv_qk.astype(F32), g_mlstm.reshape(1, -1).astype(F32))


def _sb_kernel(q_ref, k_ref, v_ref, o_ref, acc_s, r_s, *, seq):
    L = SB_BLOCK
    tq, kb = SB_TQ, SB_KB
    scale = 1.0 / math.sqrt(SB_HD)
    rows = lax.broadcasted_iota(I32, (tq, kb), 0)
    cols = lax.broadcasted_iota(I32, (tq, kb), 1)
    ur = lax.broadcasted_iota(I32, (2 * L, 2 * L), 0) & (L - 1)
    uc = lax.broadcasted_iota(I32, (2 * L, 2 * L), 1)
    u = jnp.where((uc >= L) | (ur > uc), 1.0, 0.0).astype(BF16)

    def step(qi, c0, mask):
        kj = k_ref[pl.ds(c0, kb), :]
        vj = v_ref[pl.ds(c0, kb), :]
        z = lax.dot_general(qi, kj, (((1,), (1,)), ((), ())), preferred_element_type=F32) * scale
        l1m = -(jnp.maximum(z, 0.0) + jnp.log(1.0 + jnp.exp(-jnp.abs(z))))
        lm = l1m if mask is None else jnp.where(mask, l1m, 0.0)
        hi = lm.astype(BF16)
        lo = (lm - hi.astype(F32)).astype(BF16)
        cs_far = jnp.dot(jnp.concatenate([hi[:, :L], lo[:, :L]], axis=1), u, preferred_element_type=F32)
        cs_near = jnp.dot(jnp.concatenate([hi[:, L:], lo[:, L:]], axis=1), u, preferred_element_type=F32)
        r0 = r_s[...]
        r1 = r0 + cs_near[:, L:]
        rest = jnp.concatenate([cs_far[:, :L] + r1, cs_near[:, :L] + r0], axis=1)
        a = jnp.exp((z + l1m) + rest)
        if mask is not None:
            a = jnp.where(mask, a, 0.0)
        acc_s[...] += jnp.dot(a.astype(BF16), vj, preferred_element_type=F32)
        r_s[...] = r1 + cs_far[:, L:]

    def qtile(t, _):
        q0 = pl.multiple_of(t * tq, tq)
        qi = q_ref[pl.ds(q0, tq), :]
        acc_s[...] = jnp.zeros_like(acc_s)
        r_s[...] = jnp.zeros_like(r_s)
        for off in range(tq - kb, -1, -kb):
            step(qi, pl.multiple_of(q0 + off, kb), cols + off < rows)

        def earlier(jj, _):
            c0 = q0 - (jj + 1) * (2 * kb)
            step(qi, pl.multiple_of(c0 + kb, kb), None)
            step(qi, pl.multiple_of(c0, kb), None)
            return 0

        lax.fori_loop(0, t * (tq // (2 * kb)), earlier, 0)
        o_ref[pl.ds(q0, tq), :] = acc_s[...].astype(o_ref.dtype)
        return 0

    lax.fori_loop(0, seq // tq, qtile, 0)


def _stick_breaking(seg_b):
    b, s, _ = seg_b.shape
    h = SB_HEADS
    return pl.pallas_call(
        functools.partial(_sb_kernel, seq=s),
        grid=(b, h),
        in_specs=[pl.BlockSpec((None, s, SB_HD), lambda bi, hi: (bi, 0, hi)),
                  pl.BlockSpec((None, s, SB_HD), lambda bi, hi: (bi, 0, h + hi)),
                  pl.BlockSpec((None, s, SB_HD), lambda bi, hi: (bi, 0, 2 * h + hi))],
        out_specs=pl.BlockSpec((None, s, SB_HD), lambda bi, hi: (bi, 0, hi)),
        out_shape=jax.ShapeDtypeStruct((b, s, h * SB_HD), BF16),
        scratch_shapes=[pltpu.VMEM((SB_TQ, SB_HD), F32), pltpu.VMEM((SB_TQ, SB_BLOCK), F32)],
        compiler_params=_cparams(("arbitrary", "arbitrary")),
        name="stick_breaking",
    )(seg_b, seg_b, seg_b)


def _merge_kernel(hm_ref, hs_ref, wa_ref, wb_ref, ga_ref, gb_ref, o_ref, wa16, wb16):
    @pl.when(pl.program_id(1) == 0)
    def _():
        wa16[...] = wa_ref[...].astype(BF16)
        wb16[...] = wb_ref[...].astype(BF16)

    ya = jnp.dot(hm_ref[...], wa16[...], preferred_element_type=F32)
    yb = jnp.dot(hs_ref[...], wb16[...], preferred_element_type=F32)
    y = _sigmoid(ga_ref[...].astype(F32)) * ya + _sigmoid(gb_ref[...].astype(F32)) * yb
    o_ref[...] = y.astype(o_ref.dtype)


def _merge(hm, hs, w_a, w_b, seg_b, gate_a_col, gate_b_col, d_model):
    m, ka = hm.shape
    kb = hs.shape[1]
    tm, tn = MM_TM, MM_TN
    ga_blk = gate_a_col // tn
    gb_blk = gate_b_col // tn
    return pl.pallas_call(
        _merge_kernel,
        grid=(d_model // tn, m // tm),
        in_specs=[pl.BlockSpec((tm, ka), lambda j, i: (i, 0)),
                  pl.BlockSpec((tm, kb), lambda j, i: (i, 0)),
                  pl.BlockSpec((ka, tn), lambda j, i: (0, j)),
                  pl.BlockSpec((kb, tn), lambda j, i: (0, j)),
                  pl.BlockSpec((tm, tn), lambda j, i: (i, ga_blk + j)),
                  pl.BlockSpec((tm, tn), lambda j, i: (i, gb_blk + j))],
        out_specs=pl.BlockSpec((tm, tn), lambda j, i: (i, j)),
        out_shape=jax.ShapeDtypeStruct((m, d_model), BF16),
        scratch_shapes=[pltpu.VMEM((ka, tn), BF16), pltpu.VMEM((kb, tn), BF16)],
        compiler_params=_cparams(("arbitrary", "arbitrary")),
        name="merge",
    )(hm, hs, w_a, w_b, seg_b, seg_b)


def _xattn_kernel(q_ref, k_ref, v_ref, o_ref, *, scale):
    s = lax.dot_general(q_ref[...], k_ref[...], (((1,), (1,)), ((), ())),
                        preferred_element_type=F32) * scale
    p = jnp.exp(s - jnp.max(s, axis=1, keepdims=True))
    p = p / jnp.sum(p, axis=1, keepdims=True)
    o_ref[...] = jnp.dot(p.astype(BF16), v_ref[...], preferred_element_type=F32).astype(o_ref.dtype)


def _xattn(q, kv, tq=512):
    b, s, d = q.shape
    n_mem = kv.shape[1]
    hd = d // XA_HEADS
    return pl.pallas_call(
        functools.partial(_xattn_kernel, scale=1.0 / math.sqrt(hd)),
        grid=(b, XA_HEADS, s // tq),
        in_specs=[pl.BlockSpec((None, tq, hd), lambda bi, hi, i: (bi, i, hi)),
                  pl.BlockSpec((None, n_mem, hd), lambda bi, hi, i: (bi, 0, hi)),
                  pl.BlockSpec((None, n_mem, hd), lambda bi, hi, i: (bi, 0, XA_HEADS + hi))],
        out_specs=pl.BlockSpec((None, tq, hd), lambda bi, hi, i: (bi, i, hi)),
        out_shape=jax.ShapeDtypeStruct((b, s, d), BF16),
        compiler_params=_cparams(("arbitrary", "arbitrary", "arbitrary")),
        name="xattn",
    )(q, kv, kv)


def _router_kernel(h_ref, g_ref, wr_ref, br_ref, hn_ref, id_ref, wt_ref, wh_ref, wl_ref):
    @pl.when(pl.program_id(0) == 0)
    def _():
        w = wr_ref[...]
        wh = w.astype(BF16)
        wh_ref[...] = wh
        wl_ref[...] = (w - wh.astype(F32)).astype(BF16)

    x = h_ref[...]
    hn = (x * lax.rsqrt(jnp.mean(x * x, axis=-1, keepdims=True) + EPS)) * g_ref[...]
    hn_ref[...] = hn
    xh = hn.astype(BF16)
    xl = (hn - xh.astype(F32)).astype(BF16)
    logits = (jnp.dot(xh, wh_ref[...], preferred_element_type=F32)
              + (jnp.dot(xh, wl_ref[...], preferred_element_type=F32)
                 + jnp.dot(xl, wh_ref[...], preferred_element_type=F32))) + br_ref[...]

    tm = logits.shape[0]
    lane = lax.broadcasted_iota(I32, (tm, LANES), 1)
    lane_f = lane.astype(F32)
    ninf = -jnp.inf

    def first_lane_of(v, vmax):
        return jnp.min(jnp.where(v == vmax, lane_f, float(LANES)), axis=1, keepdims=True)

    gl = jnp.where(lane < N_GROUPS, logits, ninf)
    gmax = jnp.max(gl, axis=1, keepdims=True)
    g_w = 1.0 / jnp.sum(jnp.exp(gl - gmax), axis=1, keepdims=True)
    g_idx = first_lane_of(gl, gmax)
    lo = float(N_GROUPS) + g_idx * float(EXPERTS_PER_GROUP)
    in_group = (lane_f >= lo) & (lane_f < lo + float(EXPERTS_PER_GROUP))
    el = jnp.where(in_group, logits, ninf)
    m1 = jnp.max(el, axis=1, keepdims=True)
    i1 = first_lane_of(el, m1)
    el2 = jnp.where(lane_f == i1, ninf, el)
    m2 = jnp.max(el2, axis=1, keepdims=True)
    i2 = first_lane_of(el2, m2)
    denom = jnp.sum(jnp.exp(el - m1), axis=1, keepdims=True)
    p1 = 1.0 / denom
    p2 = jnp.exp(m2 - m1) / denom
    psum = p1 + p2
    w1 = g_w * (p1 / psum)
    w2 = g_w * (p2 / psum)
    ids = jnp.where(lane == 0, i1 - float(N_GROUPS), jnp.where(lane == 1, i2 - float(N_GROUPS), 0.0))
    id_ref[...] = ids.astype(I32)
    wt_ref[...] = jnp.where(lane == 0, w1, jnp.where(lane == 1, w2, 0.0))


def _router(h, g, w_r, b_r):
    t, d = h.shape
    tm = ROUTE_TM
    return pl.pallas_call(
        _router_kernel,
        grid=(t // tm,),
        in_specs=[pl.BlockSpec((tm, d), lambda i: (i, 0)),
                  pl.BlockSpec((1, d), lambda i: (0, 0)),
                  pl.BlockSpec((d, LANES), lambda i: (0, 0)),
                  pl.BlockSpec((1, LANES), lambda i: (0, 0))],
        out_specs=[pl.BlockSpec((tm, d), lambda i: (i, 0)),
                   pl.BlockSpec((tm, LANES), lambda i: (i, 0)),
                   pl.BlockSpec((tm, LANES), lambda i: (i, 0))],
        out_shape=[jax.ShapeDtypeStruct((t, d), F32),
                   jax.ShapeDtypeStruct((t, LANES), I32),
                   jax.ShapeDtypeStruct((t, LANES), F32)],
        scratch_shapes=[pltpu.VMEM((d, LANES), BF16), pltpu.VMEM((d, LANES), BF16)],
        compiler_params=_cparams(("arbitrary",)),
        name="router",
    )(h, g.reshape(1, d).astype(F32), w_r, b_r)


def _meta_kernel(ids_ref, dest_ref, be_ref, nu_ref, rank_s, *, n_tok, n_blocks):
    tb = META_TB
    lane = lax.broadcasted_iota(I32, (tb, LANES), 1)
    lower = jnp.where(lax.broadcasted_iota(I32, (tb, tb), 0) > lax.broadcasted_iota(I32, (tb, tb), 1),
                      1.0, 0.0).astype(BF16)

    def onehots(b):
        ids = ids_ref[pl.ds(pl.multiple_of(b * tb, tb), tb), :]
        return lane == ids[:, 0:1], lane == ids[:, 1:2]

    def lanes01(v0, v1):
        return jnp.where(lane == 0, v0, jnp.where(lane == 1, v1, 0.0))

    def count(b, carry):
        o1, o2 = onehots(b)
        cnt = jnp.where(o1, 1.0, 0.0) + jnp.where(o2, 1.0, 0.0)
        before = jnp.dot(lower, cnt.astype(BF16), preferred_element_type=F32) + carry
        r1 = jnp.sum(jnp.where(o1, before, 0.0), axis=1, keepdims=True)
        r2 = jnp.sum(jnp.where(o2, before, 0.0), axis=1, keepdims=True)
        rank_s[pl.ds(pl.multiple_of(b * tb, tb), tb), :] = lanes01(r1, r2)
        return carry + jnp.sum(cnt, axis=0, keepdims=True)

    counts = lax.fori_loop(0, n_tok // tb, count, jnp.zeros((1, LANES), F32))

    nblk = jnp.floor((counts + (MOE_BLK - 1)) * (1.0 / MOE_BLK))
    upper = jnp.where(lax.broadcasted_iota(I32, (LANES, LANES), 0)
                      <= lax.broadcasted_iota(I32, (LANES, LANES), 1), 1.0, 0.0).astype(BF16)
    cum_end = jnp.dot(jnp.broadcast_to(nblk, (8, LANES)).astype(BF16), upper,
                      preferred_element_type=F32)[0:1, :]
    row_start = (cum_end - nblk) * float(MOE_BLK)

    def place(b, _):
        o1, o2 = onehots(b)
        s1 = jnp.sum(jnp.where(o1, row_start, 0.0), axis=1, keepdims=True)
        s2 = jnp.sum(jnp.where(o2, row_start, 0.0), axis=1, keepdims=True)
        sl = pl.ds(pl.multiple_of(b * tb, tb), tb)
        dest_ref[sl, :] = (rank_s[sl, :] + lanes01(s1, s2)).astype(I32)
        return 0

    lax.fori_loop(0, n_tok // tb, place, 0)

    blk = lax.broadcasted_iota(I32, (n_blocks, LANES), 0).astype(F32)
    elane = lax.broadcasted_iota(I32, (n_blocks, LANES), 1)
    done = jnp.where((elane < N_EXPERTS) & (cum_end <= blk), 1.0, 0.0)
    be = jnp.minimum(jnp.sum(done, axis=1, keepdims=True), float(N_EXPERTS - 1))
    be_ref[...] = jnp.broadcast_to(be, (n_blocks, LANES)).astype(I32)
    nu_ref[...] = jnp.broadcast_to(cum_end[:, N_EXPERTS - 1:N_EXPERTS], (8, LANES)).astype(I32)


def _moe_meta(ids, n_blocks):
    t = ids.shape[0]
    return pl.pallas_call(
        functools.partial(_meta_kernel, n_tok=t, n_blocks=n_blocks),
        out_shape=[jax.ShapeDtypeStruct((t, LANES), I32),
                   jax.ShapeDtypeStruct((n_blocks, LANES), I32),
                   jax.ShapeDtypeStruct((8, LANES), I32)],
        scratch_shapes=[pltpu.VMEM((t, LANES), F32)],
        compiler_params=pltpu.CompilerParams(vmem_limit_bytes=VMEM_LIMIT),
        name="moe_meta",
    )(ids)


def _dispatch_kernel(dest_ref, x_ref, xs_in_ref, xs_ref, sem):
    del xs_in_ref
    tb = x_ref.shape[0]

    def row_copy(r, d):
        return pltpu.make_async_copy(x_ref.at[pl.ds(r, 1)], xs_ref.at[pl.ds(d, 1)], sem)

    def issue(r, _):
        row_copy(r, dest_ref[0, 0, 2 * r]).start()
        row_copy(r, dest_ref[0, 0, 2 * r + 1]).start()
        return 0

    lax.fori_loop(0, tb, issue, 0)

    def drain(r, _):
        row_copy(r, dest_ref[0, 0, 2 * r]).wait()
        row_copy(r, dest_ref[0, 0, 2 * r + 1]).wait()
        return 0

    lax.fori_loop(0, tb, drain, 0)


def _dispatch(dest_blocks, hn, xs_zero):
    t, d = hn.shape
    tb = DISPATCH_TB
    return pl.pallas_call(
        _dispatch_kernel,
        grid=(t // tb,),
        in_specs=[pl.BlockSpec((1, 1, 2 * tb), lambda i: (i, 0, 0), memory_space=pltpu.SMEM),
                  pl.BlockSpec((tb, d), lambda i: (i, 0)),
                  pl.BlockSpec(memory_space=pl.ANY)],
        out_specs=pl.BlockSpec(memory_space=pl.ANY),
        out_shape=jax.ShapeDtypeStruct(xs_zero.shape, xs_zero.dtype),
        scratch_shapes=[pltpu.SemaphoreType.DMA(())],
        input_output_aliases={2: 0},
        compiler_params=_cparams(("arbitrary",)),
        name="moe_dispatch",
    )(dest_blocks, hn, xs_zero)


def _expert_kernel(be_ref, nu_ref, xs_ref, wg_ref, wu_ref, wd_ref, ys_ref, gacc, uacc):
    del be_ref
    i = pl.program_id(0)
    k = pl.program_id(1)

    @pl.when(i < nu_ref[0])
    def _():
        @pl.when(k == 0)
        def _():
            gacc[...] = jnp.zeros_like(gacc)
            uacc[...] = jnp.zeros_like(uacc)

        x = xs_ref[...].astype(BF16)
        gacc[...] += jnp.dot(x, wg_ref[...].astype(BF16), preferred_element_type=F32)
        uacc[...] += jnp.dot(x, wu_ref[...].astype(BF16), preferred_element_type=F32)

        @pl.when(k == MOE_KS - 1)
        def _():
            g = gacc[...]
            hb = (g * _sigmoid(g)) * uacc[...]
            ys_ref[...] = jnp.dot(hb.astype(BF16), wd_ref[...].astype(BF16),
                                  preferred_element_type=F32)


def _experts(block_expert, n_used, xs, w_gate, w_up, w_down):
    n_rows, d = xs.shape
    n_blocks = n_rows // MOE_BLK
    f = w_gate.shape[-1]
    tk = d // MOE_KS

    def blk(i, nu):
        return jnp.minimum(i, nu[0] - 1)

    def kk(i, k, nu):
        return jnp.where(i < nu[0], k, MOE_KS - 1)

    grid_spec = pltpu.PrefetchScalarGridSpec(
        num_scalar_prefetch=2,
        grid=(n_blocks, MOE_KS),
        in_specs=[
            pl.BlockSpec((MOE_BLK, tk), lambda i, k, be, nu: (blk(i, nu), kk(i, k, nu))),
            pl.BlockSpec((None, tk, f), lambda i, k, be, nu: (be[blk(i, nu)], kk(i, k, nu), 0)),
            pl.BlockSpec((None, tk, f), lambda i, k, be, nu: (be[blk(i, nu)], kk(i, k, nu), 0)),
            pl.BlockSpec((None, f, d), lambda i, k, be, nu: (be[blk(i, nu)], 0, 0)),
        ],
        out_specs=pl.BlockSpec((MOE_BLK, d), lambda i, k, be, nu: (blk(i, nu), 0)),
        scratch_shapes=[pltpu.VMEM((MOE_BLK, f), F32), pltpu.VMEM((MOE_BLK, f), F32)],
    )
    return pl.pallas_call(
        _expert_kernel,
        grid_spec=grid_spec,
        out_shape=jax.ShapeDtypeStruct((n_rows, d), F32),
        input_output_aliases={2: 0},
        compiler_params=_cparams(("arbitrary", "arbitrary")),
        name="moe_experts",
    )(block_expert, n_used, xs, w_gate, w_up, w_down)


def _combine_kernel(dest_ref, w_ref, h_ref, g_ref, ys_ref, o_ref, ybuf, sem, *, final_norm):
    tb = h_ref.shape[0]

    def row_copy(r, k, d):
        return pltpu.make_async_copy(ys_ref.at[pl.ds(d, 1)], ybuf.at[k, pl.ds(r, 1)], sem)

    def issue(r, _):
        row_copy(r, 0, dest_ref[0, 0, 2 * r]).start()
        row_copy(r, 1, dest_ref[0, 0, 2 * r + 1]).start()
        return 0

    lax.fori_loop(0, tb, issue, 0)

    def drain(r, _):
        row_copy(r, 0, dest_ref[0, 0, 2 * r]).wait()
        row_copy(r, 1, dest_ref[0, 0, 2 * r + 1]).wait()
        return 0

    lax.fori_loop(0, tb, drain, 0)

    w = w_ref[...]
    h = h_ref[...] + (w[:, 0:1] * ybuf[0] + w[:, 1:2] * ybuf[1])
    if final_norm:
        h = (h * lax.rsqrt(jnp.mean(h * h, axis=-1, keepdims=True) + EPS)) * g_ref[...]
    o_ref[...] = h


def _combine(dest_blocks, wts, h, g, ys, final_norm):
    t, d = h.shape
    tb = COMBINE_TB
    return pl.pallas_call(
        functools.partial(_combine_kernel, final_norm=final_norm),
        grid=(t // tb,),
        in_specs=[pl.BlockSpec((1, 1, 2 * tb), lambda i: (i, 0, 0), memory_space=pltpu.SMEM),
                  pl.BlockSpec((tb, LANES), lambda i: (i, 0)),
                  pl.BlockSpec((tb, d), lambda i: (i, 0)),
                  pl.BlockSpec((1, d), lambda i: (0, 0)),
                  pl.BlockSpec(memory_space=pl.ANY)],
        out_specs=pl.BlockSpec((tb, d), lambda i: (i, 0)),
        out_shape=jax.ShapeDtypeStruct((t, d), F32),
        scratch_shapes=[pltpu.VMEM((2, tb, d), F32), pltpu.SemaphoreType.DMA(())],
        compiler_params=_cparams(("arbitrary",)),
        name="moe_combine",
    )(dest_blocks, wts, h, g.reshape(1, d).astype(F32), ys)


def kernel(x, mem, norm_mix, w_in, conv_qk, b_gates, g_mlstm, w_proj_a, w_proj_b, w_out,
           norm_xattn, norm_mem, w_q_mem, w_kv_mem, w_o_mem, norm_moe,
           w_router_group, b_router_group, w_router_expert, b_router_expert,
           w_gate, w_up, w_down, norm_final):
    b, s, d = x.shape
    t = b * s
    n_mem = mem.shape[1]
    depth = w_in.shape[0]
    ml_qk_w = ML_HEADS * ML_QK
    ml_v_w = ML_HEADS * ML_V
    sb_w = SB_HEADS * SB_HD
    seg_a_w = 2 * ml_qk_w + 2 * ml_v_w
    n_gate_cols = 2 * ML_HEADS
    seg_b_w = 3 * sb_w + 2 * d
    n_blocks = (t * 2) // MOE_BLK + N_EXPERTS
    n_rows = n_blocks * MOE_BLK

    h = x.reshape(t, d)
    mem2 = mem.reshape(b * n_mem, d)
    for l in range(depth):
        xn = _rmsnorm(h, norm_mix[l], BF16)
        w_in_l = w_in[l]
        seg_a = _matmul(xn, w_in_l, seg_a_w, BF16, name="in_proj_a")
        w_if = jnp.pad(w_in_l[:, seg_a_w:seg_a_w + n_gate_cols], ((0, 0), (0, LANES - n_gate_cols)))
        gates = _matmul(xn, w_if, LANES, F32, tn=LANES, name="in_proj_gates")
        seg_b = _matmul(xn, w_in_l, seg_b_w, BF16, col_start=seg_a_w + n_gate_cols, name="in_proj_b")
        gates_t = gates[:, :n_gate_cols].reshape(b, s, 2, ML_HEADS).transpose(0, 2, 3, 1)
        gates_t = gates_t.reshape(b, 2, ML_HEADS, s // ML_CHUNK, ML_CHUNK)
        hm = _mlstm(seg_a.reshape(b, s, seg_a_w), gates_t, conv_qk[l], b_gates[l], g_mlstm[l])
        hs = _stick_breaking(seg_b.reshape(b, s, seg_b_w))
        y = _merge(hm.reshape(t, ml_v_w), hs.reshape(t, sb_w), w_proj_a[l], w_proj_b[l],
                   seg_b, 3 * sb_w, 3 * sb_w + d, d)
        h = _matmul(y, w_out[l], d, F32, res=h, name="out_proj")
        hn = _rmsnorm(h, norm_xattn[l], BF16)
        q = _matmul(hn, w_q_mem[l], d, BF16, name="xattn_q")
        memn = _rmsnorm(mem2, norm_mem[l], BF16)
        kv = _matmul(memn, w_kv_mem[l], 2 * d, BF16, name="xattn_kv")
        o = _xattn(q.reshape(b, s, d), kv.reshape(b, n_mem, 2 * d))
        h = _matmul(o.reshape(t, d), w_o_mem[l], d, F32, res=h, name="xattn_o")
        w_r = jnp.pad(jnp.concatenate([w_router_group[l], w_router_expert[l]], axis=1),
                      ((0, 0), (0, LANES - N_GROUPS - N_EXPERTS)))
        b_r = jnp.pad(jnp.concatenate([b_router_group[l], b_router_expert[l]]),
                      (0, LANES - N_GROUPS - N_EXPERTS)).reshape(1, LANES).astype(F32)
        hn3, ids, wts = _router(h, norm_moe[l], w_r, b_r)
        dest, block_expert, n_used = _moe_meta(ids, n_blocks)
        xs = _dispatch(dest[:, :2].reshape(t // DISPATCH_TB, 1, 2 * DISPATCH_TB), hn3,
                       jnp.zeros((n_rows, d), F32))
        ys = _experts(block_expert[:, 0], n_used[0, :1], xs, w_gate[l], w_up[l], w_down[l])
        h = _combine(dest[:, :2].reshape(t // COMBINE_TB, 1, 2 * COMBINE_TB), wts, h,
                     norm_final, ys, final_norm=(l == depth - 1))
    return h.reshape(b, s, d)
```

```python
import functools
import math

import jax
import jax.numpy as jnp
from jax import lax
from jax.experimental import pallas as pl
from jax.experimental.pallas import tpu as pltpu

F32 = jnp.float32
BF16 = jnp.bfloat16
I32 = jnp.int32

EPS = 1e-6
ML_HEADS = 8
ML_QK = 128
ML_V = 256
ML_CHUNK = 128
CONV_W = 4
SB_HEADS = 16
SB_HD = 128
SB_BLOCK = 128
XA_HEADS = 4
N_GROUPS = 8
EXPERTS_PER_GROUP = 8
N_EXPERTS = N_GROUPS * EXPERTS_PER_GROUP
D_EXPERT = 512

LANES = 128
VMEM_LIMIT = 56 * 1024 * 1024

SB_TQ = 512
SB_KB = 256
MM_TM = 1024
MM_TN = 512
MM_CAST_ROWS = 512
MOE_ALIGN = 8
MOE_SBLK = 512
MOE_SUB = 128
MOE_KS = 4
ROUTE_TM = 256
META_TB = 256
DISPATCH_TB = 256
COMBINE_TB = 128


def _cparams(sem, vmem=VMEM_LIMIT):
    return pltpu.CompilerParams(dimension_semantics=sem, vmem_limit_bytes=vmem)


def _sigmoid(x):
    return 1.0 / (1.0 + jnp.exp(-x))


def _neg_softplus(x):
    return -(jnp.maximum(x, 0.0) + jnp.log1p(jnp.exp(-jnp.abs(x))))


def _rmsnorm_kernel(x_ref, g_ref, o_ref):
    x = x_ref[...].astype(F32)
    ms = jnp.mean(x * x, axis=-1, keepdims=True)
    o_ref[...] = ((x * lax.rsqrt(ms + EPS)) * g_ref[...]).astype(o_ref.dtype)


def _rmsnorm(x, g, out_dtype, tm=256):
    m, d = x.shape
    return pl.pallas_call(
        _rmsnorm_kernel,
        grid=(m // tm,),
        in_specs=[pl.BlockSpec((tm, d), lambda i: (i, 0)),
                  pl.BlockSpec((1, d), lambda i: (0, 0))],
        out_specs=pl.BlockSpec((tm, d), lambda i: (i, 0)),
        out_shape=jax.ShapeDtypeStruct((m, d), out_dtype),
        compiler_params=_cparams(("arbitrary",)),
        name="rmsnorm",
    )(x, g.reshape(1, d).astype(F32))


def _mm_kernel(*refs, has_res, shift):
    a_ref, w_ref = refs[0], refs[1]
    wn_ref = refs[2] if shift else None
    r_ref = refs[2 + bool(shift)] if has_res else None
    o_ref, w16_ref = refs[-2], refs[-1]
    k, tn = w16_ref.shape

    @pl.when(pl.program_id(1) == 0)
    def _():
        if shift:
            for r0 in range(0, k, MM_CAST_ROWS):
                rows = pl.ds(r0, MM_CAST_ROWS)
                wide = jnp.concatenate([w_ref[rows, :], wn_ref[rows, :]], axis=1)
                w16_ref[rows, :] = wide[:, shift:shift + tn].astype(BF16)
        else:
            w16_ref[...] = w_ref[...].astype(BF16)

    acc = jnp.dot(a_ref[...], w16_ref[...], preferred_element_type=F32)
    if has_res:
        acc = r_ref[...] + acc
    o_ref[...] = acc.astype(o_ref.dtype)


def _matmul(a, w, n_cols, out_dtype, res=None, tn=MM_TN, col_start=0, name="matmul"):
    m, k = a.shape
    tm = min(MM_TM, m)
    grid = (n_cols // tn, m // tm)
    shift = col_start % LANES
    base_blk, rem = divmod(col_start - shift, tn)
    assert rem == 0
    in_specs = [pl.BlockSpec((tm, k), lambda j, i: (i, 0)),
                pl.BlockSpec((k, tn), lambda j, i: (0, base_blk + j))]
    args = [a, w]
    if shift:
        lane_tiles = tn // LANES
        in_specs.append(pl.BlockSpec((k, LANES), lambda j, i: (0, (base_blk + j + 1) * lane_tiles)))
        args.append(w)
    if res is not None:
        in_specs.append(pl.BlockSpec((tm, tn), lambda j, i: (i, j)))
        args.append(res)
    return pl.pallas_call(
        functools.partial(_mm_kernel, has_res=res is not None, shift=shift),
        grid=grid,
        in_specs=in_specs,
        out_specs=pl.BlockSpec((tm, tn), lambda j, i: (i, j)),
        out_shape=jax.ShapeDtypeStruct((m, n_cols), out_dtype),
        scratch_shapes=[pltpu.VMEM((k, tn), BF16)],
        compiler_params=_cparams(("arbitrary", "arbitrary")),
        name=name,
    )(*args)


def _mlstm_kernel(bg_ref, q_ref, k_ref, v_ref, o_ref, gt_ref, cwq_ref, cwk_ref, gm_ref,
                  out_ref, qpad, kpad, bc_s, li_s, *, seq):
    head = pl.program_id(1)
    n_chunks = seq // ML_CHUNK
    L = ML_CHUNK

    qpad[0:8, :] = jnp.zeros((8, ML_QK), F32)
    kpad[0:8, :] = jnp.zeros((8, ML_QK), F32)
    qpad[8:, :] = q_ref[...].astype(F32)
    kpad[8:, :] = k_ref[...].astype(F32)

    li = gt_ref[0] + bg_ref[head]
    lf = _neg_softplus(-(gt_ref[1] + bg_ref[ML_HEADS + head]))
    lane = lax.broadcasted_iota(I32, (n_chunks, L), 1)
    bc = lf
    for s in (1, 2, 4, 8, 16, 32, 64):
        bc = bc + jnp.where(lane >= s, pltpu.roll(bc, s, axis=1), 0.0)
    bc_s[...] = bc
    li_s[...] = li

    rows = lax.broadcasted_iota(I32, (L, L), 0)
    cols = lax.broadcasted_iota(I32, (L, L), 1)
    eye = rows == cols
    causal = cols <= rows
    cwq = cwq_ref[...]
    cwk = cwk_ref[...]
    gm = gm_ref[...]
    k_scale = 1.0 / math.sqrt(ML_QK)

    def to_col(row):
        return jnp.sum(jnp.where(eye, row, 0.0), axis=1, keepdims=True)

    def conv_silu(win, cw):
        y = (cw[0:1, :] * win[5:5 + L] + cw[1:2, :] * win[6:6 + L]
             + cw[2:3, :] * win[7:7 + L] + cw[3:4, :] * win[8:8 + L])
        return y * _sigmoid(y)

    def chunk(c, carry):
        c_st, n_st, m_st = carry
        r0 = pl.multiple_of(c * L, L)
        qb = conv_silu(qpad[pl.ds(r0, L + 8), :], cwq)
        kb = conv_silu(kpad[pl.ds(r0, L + 8), :], cwk) * k_scale
        vb = v_ref[pl.ds(r0, L), :]
        bc_row = bc_s[pl.ds(c, 1), :]
        li_row = li_s[pl.ds(c, 1), :]
        bc_col = to_col(bc_row)

        d = jnp.where(causal, bc_col - bc_row + li_row, -jnp.inf)
        inter = bc_col + m_st
        m_t = jnp.maximum(inter, jnp.max(d, axis=1, keepdims=True))
        w_intra = jnp.exp(d - m_t)
        w_inter = jnp.exp(inter - m_t)

        qb16 = qb.astype(BF16)
        kb16 = kb.astype(BF16)
        sc = lax.dot_general(qb16, kb16, (((1,), (1,)), ((), ())),
                             preferred_element_type=F32) * w_intra
        num = (w_inter * jnp.dot(qb16, c_st.astype(BF16), preferred_element_type=F32)
               + jnp.dot(sc.astype(BF16), vb, preferred_element_type=F32))
        den = (w_inter * jnp.sum(qb * n_st, axis=1, keepdims=True)
               + jnp.sum(sc, axis=1, keepdims=True))
        hh = num / jnp.maximum(jnp.abs(den), jnp.exp(-m_t))

        hn = hh * lax.rsqrt(jnp.mean(hh * hh, axis=1, keepdims=True) + EPS)
        og = _sigmoid(o_ref[pl.ds(r0, L), :].astype(F32))
        out_ref[pl.ds(r0, L), :] = (og * (hn * gm)).astype(out_ref.dtype)

        g = bc_row[:, L - 1:L]
        ds_row = g - bc_row + li_row
        m_new = jnp.maximum(g + m_st, jnp.max(ds_row, axis=1, keepdims=True))
        w_s = jnp.exp(ds_row - m_new)
        decay = jnp.exp(g + m_st - m_new)
        kw = kb * to_col(w_s)
        c_new = decay * c_st + jnp.dot(kw.T.astype(BF16), vb, preferred_element_type=F32)
        n_new = decay * n_st + jnp.sum(kw, axis=0, keepdims=True)
        return c_new, n_new, m_new

    init = (jnp.zeros((ML_QK, ML_V), F32), jnp.zeros((1, ML_QK), F32),
            jnp.full((1, 1), -1e30, F32))
    lax.fori_loop(0, n_chunks, chunk, init)


def _mlstm(seg_a, gates_t, conv_qk, b_gates, g_mlstm):
    b, s, _ = seg_a.shape
    h = ML_HEADS
    n_chunks = s // ML_CHUNK
    qk_blocks = h
    return pl.pallas_call(
        functools.partial(_mlstm_kernel, seq=s),
        grid=(b, h),
        in_specs=[
            pl.BlockSpec(memory_space=pltpu.SMEM),
            pl.BlockSpec((None, s, ML_QK), lambda bi, hi: (bi, 0, hi)),
            pl.BlockSpec((None, s, ML_QK), lambda bi, hi: (bi, 0, qk_blocks + hi)),
            pl.BlockSpec((None, s, ML_V), lambda bi, hi: (bi, 0, qk_blocks + hi)),
            pl.BlockSpec((None, s, ML_V), lambda bi, hi: (bi, 0, 2 * qk_blocks + hi)),
            pl.BlockSpec((None, 2, None, n_chunks, ML_CHUNK), lambda bi, hi: (bi, 0, hi, 0, 0)),
            pl.BlockSpec((CONV_W, ML_QK), lambda bi, hi: (0, hi)),
            pl.BlockSpec((CONV_W, ML_QK), lambda bi, hi: (0, qk_blocks + hi)),
            pl.BlockSpec((1, ML_V), lambda bi, hi: (0, hi)),
        ],
        out_specs=pl.BlockSpec((None, s, ML_V), lambda bi, hi: (bi, 0, hi)),
        out_shape=jax.ShapeDtypeStruct((b, s, h * ML_V), BF16),
        scratch_shapes=[pltpu.VMEM((s + 8, ML_QK), F32), pltpu.VMEM((s + 8, ML_QK), F32),
                        pltpu.VMEM((n_chunks, ML_CHUNK), F32),
                        pltpu.VMEM((n_chunks, ML_CHUNK), F32)],
        compiler_params=_cparams(("arbitrary", "arbitrary")),
        name="mlstm",
    )(b_gates.astype(F32), seg_a, seg_a, seg_a, seg_a, gates_t, conv_qk.astype(F32),
      conv_qk.astype(F32), g_mlstm.reshape(1, -1).astype(F32))


def _sb_kernel(q_ref, k_ref, v_ref, o_ref, acc_s, r_s, *, seq):
    L = SB_BLOCK
    tq, kb = SB_TQ, SB_KB
    scale = 1.0 / math.sqrt(SB_HD)
    rows = lax.broadcasted_iota(I32, (tq, kb), 0)
    cols = lax.broadcasted_iota(I32, (tq, kb), 1)
    ur = lax.broadcasted_iota(I32, (2 * L, 2 * L), 0) & (L - 1)
    uc = lax.broadcasted_iota(I32, (2 * L, 2 * L), 1)
    u = jnp.where((uc >= L) | (ur > uc), 1.0, 0.0).astype(BF16)

    def step(qi, c0, mask):
        kj = k_ref[pl.ds(c0, kb), :]
        vj = v_ref[pl.ds(c0, kb), :]
        z = lax.dot_general(qi, kj, (((1,), (1,)), ((), ())), preferred_element_type=F32) * scale
        l1m = -(jnp.maximum(z, 0.0) + jnp.log(1.0 + jnp.exp(-jnp.abs(z))))
        lm = l1m if mask is None else jnp.where(mask, l1m, 0.0)
        hi = lm.astype(BF16)
        lo = (lm - hi.astype(F32)).astype(BF16)
        cs_far = jnp.dot(jnp.concatenate([hi[:, :L], lo[:, :L]], axis=1), u, preferred_element_type=F32)
        cs_near = jnp.dot(jnp.concatenate([hi[:, L:], lo[:, L:]], axis=1), u, preferred_element_type=F32)
        r0 = r_s[...]
        r1 = r0 + cs_near[:, L:]
        rest = jnp.concatenate([cs_far[:, :L] + r1, cs_near[:, :L] + r0], axis=1)
        a = jnp.exp((z + l1m) + rest)
        if mask is not None:
            a = jnp.where(mask, a, 0.0)
        acc_s[...] += jnp.dot(a.astype(BF16), vj, preferred_element_type=F32)
        r_s[...] = r1 + cs_far[:, L:]

    def qtile(t, _):
        q0 = pl.multiple_of(t * tq, tq)
        qi = q_ref[pl.ds(q0, tq), :]
        acc_s[...] = jnp.zeros_like(acc_s)
        r_s[...] = jnp.zeros_like(r_s)
        for off in range(tq - kb, -1, -kb):
            step(qi, pl.multiple_of(q0 + off, kb), cols + off < rows)

        def earlier(jj, _):
            c0 = q0 - (jj + 1) * (2 * kb)
            step(qi, pl.multiple_of(c0 + kb, kb), None)
            step(qi, pl.multiple_of(c0, kb), None)
            return 0

        lax.fori_loop(0, t * (tq // (2 * kb)), earlier, 0)
        o_ref[pl.ds(q0, tq), :] = acc_s[...].astype(o_ref.dtype)
        return 0

    lax.fori_loop(0, seq // tq, qtile, 0)


def _stick_breaking(seg_b):
    b, s, _ = seg_b.shape
    h = SB_HEADS
    return pl.pallas_call(
        functools.partial(_sb_kernel, seq=s),
        grid=(b, h),
        in_specs=[pl.BlockSpec((None, s, SB_HD), lambda bi, hi: (bi, 0, hi)),
                  pl.BlockSpec((None, s, SB_HD), lambda bi, hi: (bi, 0, h + hi)),
                  pl.BlockSpec((None, s, SB_HD), lambda bi, hi: (bi, 0, 2 * h + hi))],
        out_specs=pl.BlockSpec((None, s, SB_HD), lambda bi, hi: (bi, 0, hi)),
        out_shape=jax.ShapeDtypeStruct((b, s, h * SB_HD), BF16),
        scratch_shapes=[pltpu.VMEM((SB_TQ, SB_HD), F32), pltpu.VMEM((SB_TQ, SB_BLOCK), F32)],
        compiler_params=_cparams(("arbitrary", "arbitrary")),
        name="stick_breaking",
    )(seg_b, seg_b, seg_b)


def _merge_kernel(hm_ref, hs_ref, wa_ref, wb_ref, ga_ref, gb_ref, o_ref, wa16, wb16):
    @pl.when(pl.program_id(1) == 0)
    def _():
        wa16[...] = wa_ref[...].astype(BF16)
        wb16[...] = wb_ref[...].astype(BF16)

    ya = jnp.dot(hm_ref[...], wa16[...], preferred_element_type=F32)
    yb = jnp.dot(hs_ref[...], wb16[...], preferred_element_type=F32)
    y = _sigmoid(ga_ref[...].astype(F32)) * ya + _sigmoid(gb_ref[...].astype(F32)) * yb
    o_ref[...] = y.astype(o_ref.dtype)


def _merge(hm, hs, w_a, w_b, seg_b, gate_a_col, gate_b_col, d_model):
    m, ka = hm.shape
    kb = hs.shape[1]
    tm, tn = MM_TM, MM_TN
    ga_blk = gate_a_col // tn
    gb_blk = gate_b_col // tn
    return pl.pallas_call(
        _merge_kernel,
        grid=(d_model // tn, m // tm),
        in_specs=[pl.BlockSpec((tm, ka), lambda j, i: (i, 0)),
                  pl.BlockSpec((tm, kb), lambda j, i: (i, 0)),
                  pl.BlockSpec((ka, tn), lambda j, i: (0, j)),
                  pl.BlockSpec((kb, tn), lambda j, i: (0, j)),
                  pl.BlockSpec((tm, tn), lambda j, i: (i, ga_blk + j)),
                  pl.BlockSpec((tm, tn), lambda j, i: (i, gb_blk + j))],
        out_specs=pl.BlockSpec((tm, tn), lambda j, i: (i, j)),
        out_shape=jax.ShapeDtypeStruct((m, d_model), BF16),
        scratch_shapes=[pltpu.VMEM((ka, tn), BF16), pltpu.VMEM((kb, tn), BF16)],
        compiler_params=_cparams(("arbitrary", "arbitrary")),
        name="merge",
    )(hm, hs, w_a, w_b, seg_b, seg_b)


def _xattn_kernel(q_ref, k_ref, v_ref, o_ref, *, scale):
    s = lax.dot_general(q_ref[...], k_ref[...], (((1,), (1,)), ((), ())),
                        preferred_element_type=F32) * scale
    p = jnp.exp(s - jnp.max(s, axis=1, keepdims=True))
    p = p / jnp.sum(p, axis=1, keepdims=True)
    o_ref[...] = jnp.dot(p.astype(BF16), v_ref[...], preferred_element_type=F32).astype(o_ref.dtype)


def _xattn(q, kv, tq=512):
    b, s, d = q.shape
    n_mem = kv.shape[1]
    hd = d // XA_HEADS
    return pl.pallas_call(
        functools.partial(_xattn_kernel, scale=1.0 / math.sqrt(hd)),
        grid=(b, XA_HEADS, s // tq),
        in_specs=[pl.BlockSpec((None, tq, hd), lambda bi, hi, i: (bi, i, hi)),
                  pl.BlockSpec((None, n_mem, hd), lambda bi, hi, i: (bi, 0, hi)),
                  pl.BlockSpec((None, n_mem, hd), lambda bi, hi, i: (bi, 0, XA_HEADS + hi))],
        out_specs=pl.BlockSpec((None, tq, hd), lambda bi, hi, i: (bi, i, hi)),
        out_shape=jax.ShapeDtypeStruct((b, s, d), BF16),
        compiler_params=_cparams(("arbitrary", "arbitrary", "arbitrary")),
        name="xattn",
    )(q, kv, kv)


def _router_kernel(h_ref, g_ref, wr_ref, br_ref, hn_ref, id_ref, wt_ref, wh_ref, wl_ref):
    @pl.when(pl.program_id(0) == 0)
    def _():
        w = wr_ref[...]
        wh = w.astype(BF16)
        wh_ref[...] = wh
        wl_ref[...] = (w - wh.astype(F32)).astype(BF16)

    x = h_ref[...]
    hn = (x * lax.rsqrt(jnp.mean(x * x, axis=-1, keepdims=True) + EPS)) * g_ref[...]
    hn_ref[...] = hn
    xh = hn.astype(BF16)
    xl = (hn - xh.astype(F32)).astype(BF16)
    logits = (jnp.dot(xh, wh_ref[...], preferred_element_type=F32)
              + (jnp.dot(xh, wl_ref[...], preferred_element_type=F32)
                 + jnp.dot(xl, wh_ref[...], preferred_element_type=F32))) + br_ref[...]

    tm = logits.shape[0]
    lane = lax.broadcasted_iota(I32, (tm, LANES), 1)
    lane_f = lane.astype(F32)
    ninf = -jnp.inf

    def first_lane_of(v, vmax):
        return jnp.min(jnp.where(v == vmax, lane_f, float(LANES)), axis=1, keepdims=True)

    gl = jnp.where(lane < N_GROUPS, logits, ninf)
    gmax = jnp.max(gl, axis=1, keepdims=True)
    g_w = 1.0 / jnp.sum(jnp.exp(gl - gmax), axis=1, keepdims=True)
    g_idx = first_lane_of(gl, gmax)
    lo = float(N_GROUPS) + g_idx * float(EXPERTS_PER_GROUP)
    in_group = (lane_f >= lo) & (lane_f < lo + float(EXPERTS_PER_GROUP))
    el = jnp.where(in_group, logits, ninf)
    m1 = jnp.max(el, axis=1, keepdims=True)
    i1 = first_lane_of(el, m1)
    el2 = jnp.where(lane_f == i1, ninf, el)
    m2 = jnp.max(el2, axis=1, keepdims=True)
    i2 = first_lane_of(el2, m2)
    denom = jnp.sum(jnp.exp(el - m1), axis=1, keepdims=True)
    p1 = 1.0 / denom
    p2 = jnp.exp(m2 - m1) / denom
    psum = p1 + p2
    w1 = g_w * (p1 / psum)
    w2 = g_w * (p2 / psum)
    ids = jnp.where(lane == 0, i1 - float(N_GROUPS), jnp.where(lane == 1, i2 - float(N_GROUPS), 0.0))
    id_ref[...] = ids.astype(I32)
    wt_ref[...] = jnp.where(lane == 0, w1, jnp.where(lane == 1, w2, 0.0))


def _router(h, g, w_r, b_r):
    t, d = h.shape
    tm = ROUTE_TM
    return pl.pallas_call(
        _router_kernel,
        grid=(t // tm,),
        in_specs=[pl.BlockSpec((tm, d), lambda i: (i, 0)),
                  pl.BlockSpec((1, d), lambda i: (0, 0)),
                  pl.BlockSpec((d, LANES), lambda i: (0, 0)),
                  pl.BlockSpec((1, LANES), lambda i: (0, 0))],
        out_specs=[pl.BlockSpec((tm, d), lambda i: (i, 0)),
                   pl.BlockSpec((tm, LANES), lambda i: (i, 0)),
                   pl.BlockSpec((tm, LANES), lambda i: (i, 0))],
        out_shape=[jax.ShapeDtypeStruct((t, d), F32),
                   jax.ShapeDtypeStruct((t, LANES), I32),
                   jax.ShapeDtypeStruct((t, LANES), F32)],
        scratch_shapes=[pltpu.VMEM((d, LANES), BF16), pltpu.VMEM((d, LANES), BF16)],
        compiler_params=_cparams(("arbitrary",)),
        name="router",
    )(h, g.reshape(1, d).astype(F32), w_r, b_r)


def _lane_cumsum(x):
    lane = lax.broadcasted_iota(I32, x.shape, 1)
    s = 1
    while s < LANES:
        x = x + jnp.where(lane >= s, pltpu.roll(x, s, axis=1), 0.0)
        s *= 2
    return x


def _meta_kernel(ids_ref, dest_ref, blk_ref, misc_ref, rank_s, *, n_tok, n_blocks):
    tb = META_TB
    lane = lax.broadcasted_iota(I32, (tb, LANES), 1)
    lower = jnp.where(lax.broadcasted_iota(I32, (tb, tb), 0) > lax.broadcasted_iota(I32, (tb, tb), 1),
                      1.0, 0.0).astype(BF16)

    def onehots(b):
        ids = ids_ref[pl.ds(pl.multiple_of(b * tb, tb), tb), :]
        return lane == ids[:, 0:1], lane == ids[:, 1:2]

    def lanes01(v0, v1):
        return jnp.where(lane == 0, v0, jnp.where(lane == 1, v1, 0.0))

    def count(b, carry):
        o1, o2 = onehots(b)
        cnt = jnp.where(o1, 1.0, 0.0) + jnp.where(o2, 1.0, 0.0)
        before = jnp.dot(lower, cnt.astype(BF16), preferred_element_type=F32) + carry
        r1 = jnp.sum(jnp.where(o1, before, 0.0), axis=1, keepdims=True)
        r2 = jnp.sum(jnp.where(o2, before, 0.0), axis=1, keepdims=True)
        rank_s[pl.ds(pl.multiple_of(b * tb, tb), tb), :] = lanes01(r1, r2)
        return carry + jnp.sum(cnt, axis=0, keepdims=True)

    counts = lax.fori_loop(0, n_tok // tb, count, jnp.zeros((1, LANES), F32))

    cnt8 = jnp.broadcast_to(counts, (8, LANES))
    whole = jnp.floor(cnt8 * (1.0 / MOE_ALIGN)) * MOE_ALIGN
    seg = jnp.where(cnt8 > whole, whole + MOE_ALIGN, whole)
    seg_start = _lane_cumsum(seg) - seg
    nblk = jnp.floor((cnt8 + (MOE_SBLK - 1)) * (1.0 / MOE_SBLK))
    blk_end = _lane_cumsum(nblk)
    blk_start = blk_end - nblk
    row_start = seg_start[0:1, :]

    def place(b, _):
        o1, o2 = onehots(b)
        s1 = jnp.sum(jnp.where(o1, row_start, 0.0), axis=1, keepdims=True)
        s2 = jnp.sum(jnp.where(o2, row_start, 0.0), axis=1, keepdims=True)
        sl = pl.ds(pl.multiple_of(b * tb, tb), tb)
        dest_ref[sl, :] = (rank_s[sl, :] + lanes01(s1, s2)).astype(I32)
        return 0

    lax.fori_loop(0, n_tok // tb, place, 0)

    step = lax.broadcasted_iota(I32, (n_blocks, LANES), 0).astype(F32)
    elane = lax.broadcasted_iota(I32, (n_blocks, LANES), 1)
    mine = (elane < N_EXPERTS) & (blk_start[0:1, :] <= step) & (step < blk_end[0:1, :])
    done_rows = (step - blk_start[0:1, :]) * float(MOE_SBLK)

    def pick(v):
        return jnp.sum(jnp.where(mine, v, 0.0), axis=1, keepdims=True)

    s_exp = pick(elane.astype(F32))
    s_row = pick(row_start + done_rows)
    s_val = pick(jnp.minimum(counts - done_rows, float(MOE_SBLK)))
    blk_ref[...] = jnp.where(elane == 0, s_exp, jnp.where(elane == 1, s_row, jnp.where(
        elane == 2, s_val, 0.0))).astype(I32)

    lane8 = lax.broadcasted_iota(I32, (8, LANES), 1)
    sub8 = lax.broadcasted_iota(I32, (8, LANES), 0)
    pad_row = jnp.where((cnt8 > whole) & (lane8 < N_EXPERTS), seg_start + whole, -1.0)
    n_used = jnp.broadcast_to(blk_end[:, N_EXPERTS - 1:N_EXPERTS], (8, LANES))
    misc_ref[...] = jnp.where(sub8 == 0, pad_row, jnp.where(sub8 == 1, n_used, 0.0)).astype(I32)


def _moe_meta(ids, n_blocks):
    t = ids.shape[0]
    assert MOE_ALIGN & (MOE_ALIGN - 1) == 0 and MOE_SBLK & (MOE_SBLK - 1) == 0
    return pl.pallas_call(
        functools.partial(_meta_kernel, n_tok=t, n_blocks=n_blocks),
        out_shape=[jax.ShapeDtypeStruct((t, LANES), I32),
                   jax.ShapeDtypeStruct((n_blocks, LANES), I32),
                   jax.ShapeDtypeStruct((8, LANES), I32)],
        scratch_shapes=[pltpu.VMEM((t, LANES), F32)],
        compiler_params=pltpu.CompilerParams(vmem_limit_bytes=VMEM_LIMIT),
        name="moe_meta",
    )(ids)


def _dispatch_kernel(dest_ref, misc_ref, x_ref, xs_ref, zbuf, sem, *, n_assign):
    tb = x_ref.shape[0]
    zrows = zbuf.shape[0]

    @pl.when(pl.program_id(0) == 0)
    def _():
        zbuf[...] = jnp.zeros_like(zbuf)

        def tail_copy(j):
            return pltpu.make_async_copy(zbuf, xs_ref.at[pl.ds(n_assign + j * zrows, zrows)], sem)

        n_tail = (xs_ref.shape[0] - n_assign) // zrows
        for j in range(n_tail):
            tail_copy(j).start()
        for j in range(n_tail):
            tail_copy(j).wait()

        def pad_copy(row):
            return pltpu.make_async_copy(zbuf.at[pl.ds(0, MOE_ALIGN)],
                                         xs_ref.at[pl.ds(row, MOE_ALIGN)], sem)

        def pads(fn):
            def body(e, _):
                row = misc_ref[0, e]

                @pl.when(row >= 0)
                def _():
                    fn(pad_copy(pl.multiple_of(row, MOE_ALIGN)))
                return 0
            lax.fori_loop(0, N_EXPERTS, body, 0)

        pads(lambda c: c.start())
        pads(lambda c: c.wait())

    def row_copy(r, d):
        return pltpu.make_async_copy(x_ref.at[pl.ds(r, 1)], xs_ref.at[pl.ds(d, 1)], sem)

    def issue(r, _):
        row_copy(r, dest_ref[0, 0, 2 * r]).start()
        row_copy(r, dest_ref[0, 0, 2 * r + 1]).start()
        return 0

    lax.fori_loop(0, tb, issue, 0)

    def drain(r, _):
        row_copy(r, dest_ref[0, 0, 2 * r]).wait()
        row_copy(r, dest_ref[0, 0, 2 * r + 1]).wait()
        return 0

    lax.fori_loop(0, tb, drain, 0)


def _dispatch(dest_blocks, misc, hn, xs_rows):
    t, d = hn.shape
    tb = DISPATCH_TB
    n_assign = 2 * t
    assert (xs_rows - n_assign) % MOE_SUB == 0
    return pl.pallas_call(
        functools.partial(_dispatch_kernel, n_assign=n_assign),
        grid=(t // tb,),
        in_specs=[pl.BlockSpec((1, 1, 2 * tb), lambda i: (i, 0, 0), memory_space=pltpu.SMEM),
                  pl.BlockSpec(memory_space=pltpu.SMEM),
                  pl.BlockSpec((tb, d), lambda i: (i, 0))],
        out_specs=pl.BlockSpec(memory_space=pl.ANY),
        out_shape=jax.ShapeDtypeStruct((xs_rows, d), F32),
        scratch_shapes=[pltpu.VMEM((MOE_SUB, d), F32), pltpu.SemaphoreType.DMA(())],
        compiler_params=_cparams(("arbitrary",)),
        name="moe_dispatch",
    )(dest_blocks, misc, hn)


def _expert_kernel(be_ref, rs_ref, nv_ref, nu_ref, xs_ref, wg_ref, wu_ref, wd_ref, ys_ref,
                   gacc, uacc, wg16, wu16, wd16, ybuf, sem, *, n_assign):
    del be_ref
    i = pl.program_id(0)
    k = pl.program_id(1)
    n_used = nu_ref[0]
    sub_shift = MOE_SUB.bit_length() - 1

    def out_copies(j, fn):
        row0 = rs_ref[j]
        rows = ((nv_ref[j] + (MOE_ALIGN - 1)) // MOE_ALIGN) * MOE_ALIGN
        n_full = rows >> sub_shift
        rem = rows & (MOE_SUB - 1)

        def piece_copy(off, size):
            src_off = off if isinstance(off, int) else pl.multiple_of(off, MOE_ALIGN)
            return pltpu.make_async_copy(
                ybuf.at[pl.ds(src_off, size)],
                ys_ref.at[pl.ds(pl.multiple_of(row0 + off, MOE_ALIGN), size)], sem)

        for s in range(MOE_SBLK // MOE_SUB):
            @pl.when(s < n_full)
            def _():
                fn(piece_copy(s * MOE_SUB, MOE_SUB))
        piece = MOE_SUB // 2
        while piece >= MOE_ALIGN:
            off = n_full * MOE_SUB + (rem & (MOE_SUB - 2 * piece))

            @pl.when((rem & piece) != 0)
            def _():
                fn(piece_copy(off, piece))
            piece //= 2

    @pl.when((i == 0) & (k == 0))
    def _():
        ybuf[...] = jnp.zeros_like(ybuf)

        def tail_copy(j):
            return pltpu.make_async_copy(ybuf, ys_ref.at[pl.ds(n_assign + j * MOE_SBLK, MOE_SBLK)], sem)

        n_tail = (ys_ref.shape[0] - n_assign) // MOE_SBLK
        for j in range(n_tail):
            tail_copy(j).start()
        for j in range(n_tail):
            tail_copy(j).wait()

    @pl.when(i < n_used)
    def _():
        n_sub = (nv_ref[i] + (MOE_SUB - 1)) >> sub_shift
        wg16[...] = wg_ref[...].astype(BF16)
        wu16[...] = wu_ref[...].astype(BF16)

        @pl.when(k == 0)
        def _():
            gacc[...] = jnp.zeros_like(gacc)
            uacc[...] = jnp.zeros_like(uacc)

        def up(r, _):
            rows = pl.ds(pl.multiple_of(r * MOE_SUB, MOE_SUB), MOE_SUB)
            x = xs_ref[rows, :].astype(BF16)
            gacc[rows, :] += jnp.dot(x, wg16[...], preferred_element_type=F32)
            uacc[rows, :] += jnp.dot(x, wu16[...], preferred_element_type=F32)
            return 0

        lax.fori_loop(0, n_sub, up, 0)

        @pl.when(k == MOE_KS - 1)
        def _():
            wd16[...] = wd_ref[...].astype(BF16)

            @pl.when(i > 0)
            def _():
                out_copies(i - 1, lambda c: c.wait())

            def down(r, _):
                rows = pl.ds(pl.multiple_of(r * MOE_SUB, MOE_SUB), MOE_SUB)
                g = gacc[rows, :]
                hb = (g * _sigmoid(g)) * uacc[rows, :]
                ybuf[rows, :] = jnp.dot(hb.astype(BF16), wd16[...], preferred_element_type=F32)
                return 0

            lax.fori_loop(0, n_sub, down, 0)
            out_copies(i, lambda c: c.start())

            @pl.when(i == n_used - 1)
            def _():
                out_copies(i, lambda c: c.wait())


def _experts(step_expert, step_row, step_valid, n_used, xs, w_gate, w_up, w_down, n_assign):
    xs_rows, d = xs.shape
    n_blocks = step_expert.shape[0]
    f = w_gate.shape[-1]
    tk = d // MOE_KS
    assert (xs_rows - n_assign) % MOE_SBLK == 0

    def blk(i, nu):
        return jnp.minimum(i, nu[0] - 1)

    def kk(i, k, nu):
        return jnp.where(i < nu[0], k, MOE_KS - 1)

    grid_spec = pltpu.PrefetchScalarGridSpec(
        num_scalar_prefetch=4,
        grid=(n_blocks, MOE_KS),
        in_specs=[
            pl.BlockSpec((pl.Element(MOE_SBLK), pl.Element(tk)),
                         lambda i, k, be, rs, nv, nu: (pl.multiple_of(rs[blk(i, nu)], MOE_ALIGN),
                                                       kk(i, k, nu) * tk)),
            pl.BlockSpec((None, tk, f), lambda i, k, be, rs, nv, nu: (be[blk(i, nu)], kk(i, k, nu), 0)),
            pl.BlockSpec((None, tk, f), lambda i, k, be, rs, nv, nu: (be[blk(i, nu)], kk(i, k, nu), 0)),
            pl.BlockSpec((None, f, d), lambda i, k, be, rs, nv, nu: (be[blk(i, nu)], 0, 0)),
        ],
        out_specs=pl.BlockSpec(memory_space=pl.ANY),
        scratch_shapes=[pltpu.VMEM((MOE_SBLK, f), F32), pltpu.VMEM((MOE_SBLK, f), F32),
                        pltpu.VMEM((tk, f), BF16), pltpu.VMEM((tk, f), BF16),
                        pltpu.VMEM((f, d), BF16), pltpu.VMEM((MOE_SBLK, d), F32),
                        pltpu.SemaphoreType.DMA(())],
    )
    return pl.pallas_call(
        functools.partial(_expert_kernel, n_assign=n_assign),
        grid_spec=grid_spec,
        out_shape=jax.ShapeDtypeStruct((xs_rows, d), F32),
        compiler_params=_cparams(("arbitrary", "arbitrary")),
        name="moe_experts",
    )(step_expert, step_row, step_valid, n_used, xs, w_gate, w_up, w_down)


def _combine_kernel(dest_ref, dest_next_ref, w_ref, h_ref, g_ref, ys_ref, o_ref, ybuf, sems, *,
                    final_norm):
    tb = h_ref.shape[0]
    i = pl.program_id(0)
    slot = i % 2

    def gathers(rows_ref, s, fn):
        def body(r, _):
            for j in range(2):
                fn(pltpu.make_async_copy(ys_ref.at[pl.ds(rows_ref[0, 0, 2 * r + j], 1)],
                                         ybuf.at[s, j, pl.ds(r, 1)], sems.at[s]))
            return 0
        lax.fori_loop(0, tb, body, 0)

    @pl.when(i == 0)
    def _():
        gathers(dest_ref, 0, lambda c: c.start())

    @pl.when(i + 1 < pl.num_programs(0))
    def _():
        gathers(dest_next_ref, 1 - slot, lambda c: c.start())

    gathers(dest_ref, slot, lambda c: c.wait())

    w = w_ref[...]
    h = h_ref[...] + (w[:, 0:1] * ybuf[slot, 0] + w[:, 1:2] * ybuf[slot, 1])
    if final_norm:
        h = (h * lax.rsqrt(jnp.mean(h * h, axis=-1, keepdims=True) + EPS)) * g_ref[...]
    o_ref[...] = h


def _combine(dest_blocks, wts, h, g, ys, final_norm):
    t, d = h.shape
    tb = COMBINE_TB
    n_tiles = t // tb
    return pl.pallas_call(
        functools.partial(_combine_kernel, final_norm=final_norm),
        grid=(n_tiles,),
        in_specs=[pl.BlockSpec((1, 1, 2 * tb), lambda i: (i, 0, 0), memory_space=pltpu.SMEM),
                  pl.BlockSpec((1, 1, 2 * tb), lambda i: (jnp.minimum(i + 1, n_tiles - 1), 0, 0),
                               memory_space=pltpu.SMEM),
                  pl.BlockSpec((tb, LANES), lambda i: (i, 0)),
                  pl.BlockSpec((tb, d), lambda i: (i, 0)),
                  pl.BlockSpec((1, d), lambda i: (0, 0)),
                  pl.BlockSpec(memory_space=pl.ANY)],
        out_specs=pl.BlockSpec((tb, d), lambda i: (i, 0)),
        out_shape=jax.ShapeDtypeStruct((t, d), F32),
        scratch_shapes=[pltpu.VMEM((2, 2, tb, d), F32), pltpu.SemaphoreType.DMA((2,))],
        compiler_params=_cparams(("arbitrary",)),
        name="moe_combine",
    )(dest_blocks, dest_blocks, wts, h, g.reshape(1, d).astype(F32), ys)


def kernel(x, mem, norm_mix, w_in, conv_qk, b_gates, g_mlstm, w_proj_a, w_proj_b, w_out,
           norm_xattn, norm_mem, w_q_mem, w_kv_mem, w_o_mem, norm_moe,
           w_router_group, b_router_group, w_router_expert, b_router_expert,
           w_gate, w_up, w_down, norm_final):
    b, s, d = x.shape
    t = b * s
    n_mem = mem.shape[1]
    depth = w_in.shape[0]
    ml_qk_w = ML_HEADS * ML_QK
    ml_v_w = ML_HEADS * ML_V
    sb_w = SB_HEADS * SB_HD
    seg_a_w = 2 * ml_qk_w + 2 * ml_v_w
    n_gate_cols = 2 * ML_HEADS
    seg_b_w = 3 * sb_w + 2 * d
    n_assign = 2 * t
    n_blocks = n_assign // MOE_SBLK + N_EXPERTS
    xs_rows = n_assign + N_EXPERTS * MOE_ALIGN + MOE_SBLK

    h = x.reshape(t, d)
    mem2 = mem.reshape(b * n_mem, d)
    for l in range(depth):
        xn = _rmsnorm(h, norm_mix[l], BF16)
        w_in_l = w_in[l]
        seg_a = _matmul(xn, w_in_l, seg_a_w, BF16, name="in_proj_a")
        w_if = jnp.pad(w_in_l[:, seg_a_w:seg_a_w + n_gate_cols], ((0, 0), (0, LANES - n_gate_cols)))
        gates = _matmul(xn, w_if, LANES, F32, tn=LANES, name="in_proj_gates")
        seg_b = _matmul(xn, w_in_l, seg_b_w, BF16, col_start=seg_a_w + n_gate_cols, name="in_proj_b")
        gates_t = gates[:, :n_gate_cols].reshape(b, s, 2, ML_HEADS).transpose(0, 2, 3, 1)
        gates_t = gates_t.reshape(b, 2, ML_HEADS, s // ML_CHUNK, ML_CHUNK)
        hm = _mlstm(seg_a.reshape(b, s, seg_a_w), gates_t, conv_qk[l], b_gates[l], g_mlstm[l])
        hs = _stick_breaking(seg_b.reshape(b, s, seg_b_w))
        y = _merge(hm.reshape(t, ml_v_w), hs.reshape(t, sb_w), w_proj_a[l], w_proj_b[l],
                   seg_b, 3 * sb_w, 3 * sb_w + d, d)
        h = _matmul(y, w_out[l], d, F32, res=h, name="out_proj")
        hn = _rmsnorm(h, norm_xattn[l], BF16)
        q = _matmul(hn, w_q_mem[l], d, BF16, name="xattn_q")
        memn = _rmsnorm(mem2, norm_mem[l], BF16)
        kv = _matmul(memn, w_kv_mem[l], 2 * d, BF16, name="xattn_kv")
        o = _xattn(q.reshape(b, s, d), kv.reshape(b, n_mem, 2 * d))
        h = _matmul(o.reshape(t, d), w_o_mem[l], d, F32, res=h, name="xattn_o")
        w_r = jnp.pad(jnp.concatenate([w_router_group[l], w_router_expert[l]], axis=1),
                      ((0, 0), (0, LANES - N_GROUPS - N_EXPERTS)))
        b_r = jnp.pad(jnp.concatenate([b_router_group[l], b_router_expert[l]]),
                      (0, LANES - N_GROUPS - N_EXPERTS)).reshape(1, LANES).astype(F32)
        hn3, ids, wts = _router(h, norm_moe[l], w_r, b_r)
        dest, steps, misc = _moe_meta(ids, n_blocks)
        xs = _dispatch(dest[:, :2].reshape(t // DISPATCH_TB, 1, 2 * DISPATCH_TB), misc, hn3, xs_rows)
        ys = _experts(steps[:, 0], steps[:, 1], steps[:, 2], misc[1, :1], xs,
                      w_gate[l], w_up[l], w_down[l], n_assign)
        h = _combine(dest[:, :2].reshape(t // COMBINE_TB, 1, 2 * COMBINE_TB), wts, h,
                     norm_final, ys, final_norm=(l == depth - 1))
    return h.reshape(b, s, d)
```

```python
import functools
import math

import jax
import jax.numpy as jnp
from jax import lax
from jax.experimental import pallas as pl
from jax.experimental.pallas import tpu as pltpu

F32 = jnp.float32
BF16 = jnp.bfloat16
I32 = jnp.int32

EPS = 1e-6
ML_HEADS = 8
ML_QK = 128
ML_V = 256
ML_CHUNK = 128
CONV_W = 4
ML_HPS = 2
SB_HEADS = 16
SB_HD = 128
SB_BLOCK = 128
XA_HEADS = 4
N_GROUPS = 8
EXPERTS_PER_GROUP = 8
N_EXPERTS = N_GROUPS * EXPERTS_PER_GROUP
D_EXPERT = 512

LANES = 128
SUBLANES = 8
VMEM_LIMIT = 56 * 1024 * 1024

SB_TQ = 512
SB_KB = 256
MM_TM = 1024
MM_TN = 512
MM_CAST_ROWS = 512
MOE_ALIGN = 8
MOE_SBLK = 512
MOE_SUB = 128
MOE_KS = 4
ROUTE_TM = 256
META_TB = 256
DISPATCH_TB = 256
COMBINE_TB = 128


def _cparams(sem, vmem=VMEM_LIMIT):
    return pltpu.CompilerParams(dimension_semantics=sem, vmem_limit_bytes=vmem)


def _sigmoid(x):
    return 1.0 / (1.0 + jnp.exp(-x))


def _neg_softplus(x):
    return -(jnp.maximum(x, 0.0) + jnp.log1p(jnp.exp(-jnp.abs(x))))


def _rmsnorm_kernel(x_ref, g_ref, o_ref):
    x = x_ref[...].astype(F32)
    ms = jnp.mean(x * x, axis=-1, keepdims=True)
    o_ref[...] = ((x * lax.rsqrt(ms + EPS)) * g_ref[...]).astype(o_ref.dtype)


def _rmsnorm(x, g, out_dtype, tm=256):
    m, d = x.shape
    return pl.pallas_call(
        _rmsnorm_kernel,
        grid=(m // tm,),
        in_specs=[pl.BlockSpec((tm, d), lambda i: (i, 0)),
                  pl.BlockSpec((1, d), lambda i: (0, 0))],
        out_specs=pl.BlockSpec((tm, d), lambda i: (i, 0)),
        out_shape=jax.ShapeDtypeStruct((m, d), out_dtype),
        compiler_params=_cparams(("arbitrary",)),
        name="rmsnorm",
    )(x, g.reshape(1, d).astype(F32))


def _mm_kernel(*refs, has_res, transposed):
    a_ref, w_ref = refs[0], refs[1]
    r_ref = refs[2] if has_res else None
    o_ref, w16_ref = refs[-2], refs[-1]
    k, tn = w16_ref.shape

    @pl.when(pl.program_id(1) == 0)
    def _():
        if transposed:
            for k0 in range(0, k, MM_CAST_ROWS):
                ks = pl.ds(k0, MM_CAST_ROWS)
                w16_ref[ks, :] = w_ref[:, ks].T.astype(BF16)
        else:
            w16_ref[...] = w_ref[...].astype(BF16)

    acc = jnp.dot(a_ref[...], w16_ref[...], preferred_element_type=F32)
    if has_res:
        acc = r_ref[...] + acc
    o_ref[...] = acc.astype(o_ref.dtype)


def _matmul(a, w, n_cols, out_dtype, res=None, tn=MM_TN, col_start=0, transposed=False,
            name="matmul"):
    m, k = a.shape
    tm = min(MM_TM, m)
    grid = (n_cols // tn, m // tm)
    if transposed:
        assert col_start % SUBLANES == 0
        w_spec = pl.BlockSpec((pl.Element(tn), pl.Element(k)),
                              lambda j, i: (pl.multiple_of(col_start + j * tn, SUBLANES), 0))
    else:
        base_blk, rem = divmod(col_start, tn)
        assert rem == 0
        w_spec = pl.BlockSpec((k, tn), lambda j, i: (0, base_blk + j))
    in_specs = [pl.BlockSpec((tm, k), lambda j, i: (i, 0)), w_spec]
    args = [a, w]
    if res is not None:
        in_specs.append(pl.BlockSpec((tm, tn), lambda j, i: (i, j)))
        args.append(res)
    return pl.pallas_call(
        functools.partial(_mm_kernel, has_res=res is not None, transposed=transposed),
        grid=grid,
        in_specs=in_specs,
        out_specs=pl.BlockSpec((tm, tn), lambda j, i: (i, j)),
        out_shape=jax.ShapeDtypeStruct((m, n_cols), out_dtype),
        scratch_shapes=[pltpu.VMEM((k, tn), BF16)],
        compiler_params=_cparams(("arbitrary", "arbitrary")),
        name=name,
    )(*args)


def _mlstm_kernel(bg_ref, q_ref, k_ref, v_ref, o_ref, gt_ref, cwq_ref, cwk_ref, gm_ref,
                  out_ref, qpad, kpad, bc_s, li_s, *, seq):
    head0 = pl.program_id(1) * ML_HPS
    n_chunks = seq // ML_CHUNK
    L = ML_CHUNK

    qpad[0:8, :] = jnp.zeros((8, ML_HPS * ML_QK), F32)
    kpad[0:8, :] = jnp.zeros((8, ML_HPS * ML_QK), F32)
    qpad[8:, :] = q_ref[...].astype(F32)
    kpad[8:, :] = k_ref[...].astype(F32)

    lane = lax.broadcasted_iota(I32, (n_chunks, L), 1)
    for hh in range(ML_HPS):
        li_s[hh] = gt_ref[0, hh] + bg_ref[head0 + hh]
        bc = _neg_softplus(-(gt_ref[1, hh] + bg_ref[ML_HEADS + head0 + hh]))
        for s in (1, 2, 4, 8, 16, 32, 64):
            bc = bc + jnp.where(lane >= s, pltpu.roll(bc, s, axis=1), 0.0)
        bc_s[hh] = bc

    rows = lax.broadcasted_iota(I32, (L, L), 0)
    cols = lax.broadcasted_iota(I32, (L, L), 1)
    eye = rows == cols
    causal = cols <= rows
    k_scale = 1.0 / math.sqrt(ML_QK)

    def to_col(row):
        return jnp.sum(jnp.where(eye, row, 0.0), axis=1, keepdims=True)

    def conv_silu(win, cw):
        y = (cw[0:1, :] * win[5:5 + L] + cw[1:2, :] * win[6:6 + L]
             + cw[2:3, :] * win[7:7 + L] + cw[3:4, :] * win[8:8 + L])
        return y * _sigmoid(y)

    def head_chunk(hh, c, r0, carry):
        c_st, n_st, m_st = carry
        qk_cols = pl.ds(hh * ML_QK, ML_QK)
        v_cols = pl.ds(hh * ML_V, ML_V)
        qb = conv_silu(qpad[pl.ds(r0, L + 8), qk_cols], cwq_ref[:, qk_cols])
        kb = conv_silu(kpad[pl.ds(r0, L + 8), qk_cols], cwk_ref[:, qk_cols]) * k_scale
        vb = v_ref[pl.ds(r0, L), v_cols]
        bc_row = bc_s[hh, pl.ds(c, 1), :]
        li_row = li_s[hh, pl.ds(c, 1), :]
        bc_col = to_col(bc_row)

        d = jnp.where(causal, bc_col - bc_row + li_row, -jnp.inf)
        inter = bc_col + m_st
        m_t = jnp.maximum(inter, jnp.max(d, axis=1, keepdims=True))
        w_intra = jnp.exp(d - m_t)
        w_inter = jnp.exp(inter - m_t)

        qb16 = qb.astype(BF16)
        kb16 = kb.astype(BF16)
        sc = lax.dot_general(qb16, kb16, (((1,), (1,)), ((), ())),
                             preferred_element_type=F32) * w_intra
        num = (w_inter * jnp.dot(qb16, c_st.astype(BF16), preferred_element_type=F32)
               + jnp.dot(sc.astype(BF16), vb, preferred_element_type=F32))
        den = (w_inter * jnp.sum(qb * n_st, axis=1, keepdims=True)
               + jnp.sum(sc, axis=1, keepdims=True))
        hval = num / jnp.maximum(jnp.abs(den), jnp.exp(-m_t))

        hn = hval * lax.rsqrt(jnp.mean(hval * hval, axis=1, keepdims=True) + EPS)
        og = _sigmoid(o_ref[pl.ds(r0, L), v_cols].astype(F32))
        out_ref[pl.ds(r0, L), v_cols] = (og * (hn * gm_ref[:, v_cols])).astype(out_ref.dtype)

        g = bc_row[:, L - 1:L]
        ds_row = g - bc_row + li_row
        m_new = jnp.maximum(g + m_st, jnp.max(ds_row, axis=1, keepdims=True))
        w_s = jnp.exp(ds_row - m_new)
        decay = jnp.exp(g + m_st - m_new)
        kw = kb * to_col(w_s)
        c_new = decay * c_st + jnp.dot(kw.T.astype(BF16), vb, preferred_element_type=F32)
        n_new = decay * n_st + jnp.sum(kw, axis=0, keepdims=True)
        return c_new, n_new, m_new

    def chunk(c, carry):
        r0 = pl.multiple_of(c * L, L)
        return tuple(head_chunk(hh, c, r0, carry[hh]) for hh in range(ML_HPS))

    init = (jnp.zeros((ML_QK, ML_V), F32), jnp.zeros((1, ML_QK), F32),
            jnp.full((1, 1), -1e30, F32))
    lax.fori_loop(0, n_chunks, chunk, (init,) * ML_HPS)


def _mlstm(seg_a, gates_t, conv_qk, b_gates, g_mlstm):
    b, s, _ = seg_a.shape
    h = ML_HEADS
    hp = ML_HPS
    n_chunks = s // ML_CHUNK
    groups = h // hp
    qk_w, v_w = hp * ML_QK, hp * ML_V
    return pl.pallas_call(
        functools.partial(_mlstm_kernel, seq=s),
        grid=(b, groups),
        in_specs=[
            pl.BlockSpec(memory_space=pltpu.SMEM),
            pl.BlockSpec((None, s, qk_w), lambda bi, gi: (bi, 0, gi)),
            pl.BlockSpec((None, s, qk_w), lambda bi, gi: (bi, 0, groups + gi)),
            pl.BlockSpec((None, s, v_w), lambda bi, gi: (bi, 0, groups + gi)),
            pl.BlockSpec((None, s, v_w), lambda bi, gi: (bi, 0, 2 * groups + gi)),
            pl.BlockSpec((None, 2, hp, n_chunks, ML_CHUNK), lambda bi, gi: (bi, 0, gi, 0, 0)),
            pl.BlockSpec((CONV_W, qk_w), lambda bi, gi: (0, gi)),
            pl.BlockSpec((CONV_W, qk_w), lambda bi, gi: (0, groups + gi)),
            pl.BlockSpec((1, v_w), lambda bi, gi: (0, gi)),
        ],
        out_specs=pl.BlockSpec((None, s, v_w), lambda bi, gi: (bi, 0, gi)),
        out_shape=jax.ShapeDtypeStruct((b, s, h * ML_V), BF16),
        scratch_shapes=[pltpu.VMEM((s + 8, qk_w), F32), pltpu.VMEM((s + 8, qk_w), F32),
                        pltpu.VMEM((hp, n_chunks, ML_CHUNK), F32),
                        pltpu.VMEM((hp, n_chunks, ML_CHUNK), F32)],
        compiler_params=_cparams(("arbitrary", "arbitrary")),
        name="mlstm",
    )(b_gates.astype(F32), seg_a, seg_a, seg_a, seg_a, gates_t, conv_qk.astype(F32),
      conv_qk.astype(F32), g_mlstm.reshape(1, -1).astype(F32))


def _sb_kernel(q_ref, k_ref, v_ref, o_ref, acc_s, r_s, *, seq):
    L = SB_BLOCK
    tq, kb = SB_TQ, SB_KB
    scale = 1.0 / math.sqrt(SB_HD)
    rows = lax.broadcasted_iota(I32, (tq, kb), 0)
    cols = lax.broadcasted_iota(I32, (tq, kb), 1)
    ur = lax.broadcasted_iota(I32, (L, 2 * L), 0)
    uc = lax.broadcasted_iota(I32, (L, 2 * L), 1)
    u = jnp.where((uc >= L) | (ur > uc), 1.0, 0.0).astype(BF16)

    def step(qi, c0, mask, row_lo=0):
        kj = k_ref[pl.ds(c0, kb), :]
        vj = v_ref[pl.ds(c0, kb), :]
        tile_rows = pl.ds(row_lo, tq - row_lo)
        z = lax.dot_general(qi[row_lo:], kj, (((1,), (1,)), ((), ())),
                            preferred_element_type=F32) * scale
        l1m = -(jnp.maximum(z, 0.0) + jnp.log(1.0 + jnp.exp(-jnp.abs(z))))
        lm = l1m if mask is None else jnp.where(mask, l1m, 0.0)
        lm16 = lm.astype(BF16)
        cs_far = jnp.dot(lm16[:, :L], u, preferred_element_type=F32)
        cs_near = jnp.dot(lm16[:, L:], u, preferred_element_type=F32)
        r0 = r_s[tile_rows, :]
        r1 = r0 + cs_near[:, L:]
        rest = jnp.concatenate([cs_far[:, :L] + r1, cs_near[:, :L] + r0], axis=1)
        a = jnp.exp((z + l1m) + rest)
        if mask is not None:
            a = jnp.where(mask, a, 0.0)
        acc_s[tile_rows, :] += jnp.dot(a.astype(BF16), vj, preferred_element_type=F32)
        r_s[tile_rows, :] = r1 + cs_far[:, L:]

    def qtile(t, _):
        q0 = pl.multiple_of(t * tq, tq)
        qi = q_ref[pl.ds(q0, tq), :]
        acc_s[...] = jnp.zeros_like(acc_s)
        r_s[...] = jnp.zeros_like(r_s)
        for off in range(tq - kb, -1, -kb):
            step(qi, pl.multiple_of(q0 + off, kb), (cols + off < rows)[off:], row_lo=off)

        def earlier(jj, _):
            c0 = q0 - (jj + 1) * (2 * kb)
            step(qi, pl.multiple_of(c0 + kb, kb), None)
            step(qi, pl.multiple_of(c0, kb), None)
            return 0

        lax.fori_loop(0, t * (tq // (2 * kb)), earlier, 0)
        o_ref[pl.ds(q0, tq), :] = acc_s[...].astype(o_ref.dtype)
        return 0

    lax.fori_loop(0, seq // tq, qtile, 0)


def _stick_breaking(seg_b):
    b, s, _ = seg_b.shape
    h = SB_HEADS
    return pl.pallas_call(
        functools.partial(_sb_kernel, seq=s),
        grid=(b, h),
        in_specs=[pl.BlockSpec((None, s, SB_HD), lambda bi, hi: (bi, 0, hi)),
                  pl.BlockSpec((None, s, SB_HD), lambda bi, hi: (bi, 0, h + hi)),
                  pl.BlockSpec((None, s, SB_HD), lambda bi, hi: (bi, 0, 2 * h + hi))],
        out_specs=pl.BlockSpec((None, s, SB_HD), lambda bi, hi: (bi, 0, hi)),
        out_shape=jax.ShapeDtypeStruct((b, s, h * SB_HD), BF16),
        scratch_shapes=[pltpu.VMEM((SB_TQ, SB_HD), F32), pltpu.VMEM((SB_TQ, SB_BLOCK), F32)],
        compiler_params=_cparams(("arbitrary", "arbitrary")),
        name="stick_breaking",
    )(seg_b, seg_b, seg_b)


def _merge_kernel(hm_ref, hs_ref, wa_ref, wb_ref, ga_ref, gb_ref, o_ref, wa16, wb16):
    @pl.when(pl.program_id(1) == 0)
    def _():
        wa16[...] = wa_ref[...].astype(BF16)
        wb16[...] = wb_ref[...].astype(BF16)

    ya = jnp.dot(hm_ref[...], wa16[...], preferred_element_type=F32)
    yb = jnp.dot(hs_ref[...], wb16[...], preferred_element_type=F32)
    y = _sigmoid(ga_ref[...].astype(F32)) * ya + _sigmoid(gb_ref[...].astype(F32)) * yb
    o_ref[...] = y.astype(o_ref.dtype)


def _merge(hm, hs, w_a, w_b, seg_b, gate_a_col, gate_b_col, d_model):
    m, ka = hm.shape
    kb = hs.shape[1]
    tm, tn = MM_TM, MM_TN
    ga_blk = gate_a_col // tn
    gb_blk = gate_b_col // tn
    return pl.pallas_call(
        _merge_kernel,
        grid=(d_model // tn, m // tm),
        in_specs=[pl.BlockSpec((tm, ka), lambda j, i: (i, 0)),
                  pl.BlockSpec((tm, kb), lambda j, i: (i, 0)),
                  pl.BlockSpec((ka, tn), lambda j, i: (0, j)),
                  pl.BlockSpec((kb, tn), lambda j, i: (0, j)),
                  pl.BlockSpec((tm, tn), lambda j, i: (i, ga_blk + j)),
                  pl.BlockSpec((tm, tn), lambda j, i: (i, gb_blk + j))],
        out_specs=pl.BlockSpec((tm, tn), lambda j, i: (i, j)),
        out_shape=jax.ShapeDtypeStruct((m, d_model), BF16),
        scratch_shapes=[pltpu.VMEM((ka, tn), BF16), pltpu.VMEM((kb, tn), BF16)],
        compiler_params=_cparams(("arbitrary", "arbitrary")),
        name="merge",
    )(hm, hs, w_a, w_b, seg_b, seg_b)


def _xattn_kernel(q_ref, k_ref, v_ref, o_ref, *, scale):
    s = lax.dot_general(q_ref[...], k_ref[...], (((1,), (1,)), ((), ())),
                        preferred_element_type=F32) * scale
    p = jnp.exp(s - jnp.max(s, axis=1, keepdims=True))
    p = p / jnp.sum(p, axis=1, keepdims=True)
    o_ref[...] = jnp.dot(p.astype(BF16), v_ref[...], preferred_element_type=F32).astype(o_ref.dtype)


def _xattn(q, kv, tq=512):
    b, s, d = q.shape
    n_mem = kv.shape[1]
    hd = d // XA_HEADS
    return pl.pallas_call(
        functools.partial(_xattn_kernel, scale=1.0 / math.sqrt(hd)),
        grid=(b, XA_HEADS, s // tq),
        in_specs=[pl.BlockSpec((None, tq, hd), lambda bi, hi, i: (bi, i, hi)),
                  pl.BlockSpec((None, n_mem, hd), lambda bi, hi, i: (bi, 0, hi)),
                  pl.BlockSpec((None, n_mem, hd), lambda bi, hi, i: (bi, 0, XA_HEADS + hi))],
        out_specs=pl.BlockSpec((None, tq, hd), lambda bi, hi, i: (bi, i, hi)),
        out_shape=jax.ShapeDtypeStruct((b, s, d), BF16),
        compiler_params=_cparams(("arbitrary", "arbitrary", "arbitrary")),
        name="xattn",
    )(q, kv, kv)


def _router_kernel(h_ref, g_ref, wr_ref, br_ref, hn_ref, id_ref, wt_ref, wh_ref, wl_ref):
    @pl.when(pl.program_id(0) == 0)
    def _():
        w = wr_ref[...]
        wh = w.astype(BF16)
        wh_ref[...] = wh
        wl_ref[...] = (w - wh.astype(F32)).astype(BF16)

    x = h_ref[...]
    hn = (x * lax.rsqrt(jnp.mean(x * x, axis=-1, keepdims=True) + EPS)) * g_ref[...]
    hn_ref[...] = hn
    xh = hn.astype(BF16)
    xl = (hn - xh.astype(F32)).astype(BF16)
    logits = (jnp.dot(xh, wh_ref[...], preferred_element_type=F32)
              + (jnp.dot(xh, wl_ref[...], preferred_element_type=F32)
                 + jnp.dot(xl, wh_ref[...], preferred_element_type=F32))) + br_ref[...]

    tm = logits.shape[0]
    lane = lax.broadcasted_iota(I32, (tm, LANES), 1)
    lane_f = lane.astype(F32)
    ninf = -jnp.inf

    def first_lane_of(v, vmax):
        return jnp.min(jnp.where(v == vmax, lane_f, float(LANES)), axis=1, keepdims=True)

    gl = jnp.where(lane < N_GROUPS, logits, ninf)
    gmax = jnp.max(gl, axis=1, keepdims=True)
    g_w = 1.0 / jnp.sum(jnp.exp(gl - gmax), axis=1, keepdims=True)
    g_idx = first_lane_of(gl, gmax)
    lo = float(N_GROUPS) + g_idx * float(EXPERTS_PER_GROUP)
    in_group = (lane_f >= lo) & (lane_f < lo + float(EXPERTS_PER_GROUP))
    el = jnp.where(in_group, logits, ninf)
    m1 = jnp.max(el, axis=1, keepdims=True)
    i1 = first_lane_of(el, m1)
    el2 = jnp.where(lane_f == i1, ninf, el)
    m2 = jnp.max(el2, axis=1, keepdims=True)
    i2 = first_lane_of(el2, m2)
    denom = jnp.sum(jnp.exp(el - m1), axis=1, keepdims=True)
    p1 = 1.0 / denom
    p2 = jnp.exp(m2 - m1) / denom
    psum = p1 + p2
    w1 = g_w * (p1 / psum)
    w2 = g_w * (p2 / psum)
    ids = jnp.where(lane == 0, i1 - float(N_GROUPS), jnp.where(lane == 1, i2 - float(N_GROUPS), 0.0))
    id_ref[...] = ids.astype(I32)
    wt_ref[...] = jnp.where(lane == 0, w1, jnp.where(lane == 1, w2, 0.0))


def _router(h, g, w_r, b_r):
    t, d = h.shape
    tm = ROUTE_TM
    return pl.pallas_call(
        _router_kernel,
        grid=(t // tm,),
        in_specs=[pl.BlockSpec((tm, d), lambda i: (i, 0)),
                  pl.BlockSpec((1, d), lambda i: (0, 0)),
                  pl.BlockSpec((d, LANES), lambda i: (0, 0)),
                  pl.BlockSpec((1, LANES), lambda i: (0, 0))],
        out_specs=[pl.BlockSpec((tm, d), lambda i: (i, 0)),
                   pl.BlockSpec((tm, LANES), lambda i: (i, 0)),
                   pl.BlockSpec((tm, LANES), lambda i: (i, 0))],
        out_shape=[jax.ShapeDtypeStruct((t, d), F32),
                   jax.ShapeDtypeStruct((t, LANES), I32),
                   jax.ShapeDtypeStruct((t, LANES), F32)],
        scratch_shapes=[pltpu.VMEM((d, LANES), BF16), pltpu.VMEM((d, LANES), BF16)],
        compiler_params=_cparams(("arbitrary",)),
        name="router",
    )(h, g.reshape(1, d).astype(F32), w_r, b_r)


def _lane_cumsum(x):
    lane = lax.broadcasted_iota(I32, x.shape, 1)
    s = 1
    while s < LANES:
        x = x + jnp.where(lane >= s, pltpu.roll(x, s, axis=1), 0.0)
        s *= 2
    return x


def _meta_kernel(ids_ref, dest_ref, blk_ref, misc_ref, rank_s, *, n_tok, n_blocks):
    tb = META_TB
    lane = lax.broadcasted_iota(I32, (tb, LANES), 1)
    lower = jnp.where(lax.broadcasted_iota(I32, (tb, tb), 0) > lax.broadcasted_iota(I32, (tb, tb), 1),
                      1.0, 0.0).astype(BF16)

    def onehots(b):
        ids = ids_ref[pl.ds(pl.multiple_of(b * tb, tb), tb), :]
        return lane == ids[:, 0:1], lane == ids[:, 1:2]

    def lanes01(v0, v1):
        return jnp.where(lane == 0, v0, jnp.where(lane == 1, v1, 0.0))

    def count(b, carry):
        o1, o2 = onehots(b)
        cnt = jnp.where(o1, 1.0, 0.0) + jnp.where(o2, 1.0, 0.0)
        before = jnp.dot(lower, cnt.astype(BF16), preferred_element_type=F32) + carry
        r1 = jnp.sum(jnp.where(o1, before, 0.0), axis=1, keepdims=True)
        r2 = jnp.sum(jnp.where(o2, before, 0.0), axis=1, keepdims=True)
        rank_s[pl.ds(pl.multiple_of(b * tb, tb), tb), :] = lanes01(r1, r2)
        return carry + jnp.sum(cnt, axis=0, keepdims=True)

    counts = lax.fori_loop(0, n_tok // tb, count, jnp.zeros((1, LANES), F32))

    cnt8 = jnp.broadcast_to(counts, (8, LANES))
    whole = jnp.floor(cnt8 * (1.0 / MOE_ALIGN)) * MOE_ALIGN
    seg = jnp.where(cnt8 > whole, whole + MOE_ALIGN, whole)
    seg_start = _lane_cumsum(seg) - seg
    nblk = jnp.floor((cnt8 + (MOE_SBLK - 1)) * (1.0 / MOE_SBLK))
    blk_end = _lane_cumsum(nblk)
    blk_start = blk_end - nblk
    row_start = seg_start[0:1, :]

    def place(b, _):
        o1, o2 = onehots(b)
        s1 = jnp.sum(jnp.where(o1, row_start, 0.0), axis=1, keepdims=True)
        s2 = jnp.sum(jnp.where(o2, row_start, 0.0), axis=1, keepdims=True)
        sl = pl.ds(pl.multiple_of(b * tb, tb), tb)
        dest_ref[sl, :] = (rank_s[sl, :] + lanes01(s1, s2)).astype(I32)
        return 0

    lax.fori_loop(0, n_tok // tb, place, 0)

    step = lax.broadcasted_iota(I32, (n_blocks, LANES), 0).astype(F32)
    elane = lax.broadcasted_iota(I32, (n_blocks, LANES), 1)
    mine = (elane < N_EXPERTS) & (blk_start[0:1, :] <= step) & (step < blk_end[0:1, :])
    done_rows = (step - blk_start[0:1, :]) * float(MOE_SBLK)

    def pick(v):
        return jnp.sum(jnp.where(mine, v, 0.0), axis=1, keepdims=True)

    s_exp = pick(elane.astype(F32))
    s_row = pick(row_start + done_rows)
    s_val = pick(jnp.minimum(counts - done_rows, float(MOE_SBLK)))
    blk_ref[...] = jnp.where(elane == 0, s_exp, jnp.where(elane == 1, s_row, jnp.where(
        elane == 2, s_val, 0.0))).astype(I32)

    lane8 = lax.broadcasted_iota(I32, (8, LANES), 1)
    sub8 = lax.broadcasted_iota(I32, (8, LANES), 0)
    pad_row = jnp.where((cnt8 > whole) & (lane8 < N_EXPERTS), seg_start + whole, -1.0)
    n_used = jnp.broadcast_to(blk_end[:, N_EXPERTS - 1:N_EXPERTS], (8, LANES))
    misc_ref[...] = jnp.where(sub8 == 0, pad_row, jnp.where(sub8 == 1, n_used, 0.0)).astype(I32)


def _moe_meta(ids, n_blocks):
    t = ids.shape[0]
    assert MOE_ALIGN & (MOE_ALIGN - 1) == 0 and MOE_SBLK & (MOE_SBLK - 1) == 0
    return pl.pallas_call(
        functools.partial(_meta_kernel, n_tok=t, n_blocks=n_blocks),
        out_shape=[jax.ShapeDtypeStruct((t, LANES), I32),
                   jax.ShapeDtypeStruct((n_blocks, LANES), I32),
                   jax.ShapeDtypeStruct((8, LANES), I32)],
        scratch_shapes=[pltpu.VMEM((t, LANES), F32)],
        compiler_params=pltpu.CompilerParams(vmem_limit_bytes=VMEM_LIMIT),
        name="moe_meta",
    )(ids)


def _dispatch_kernel(dest_ref, misc_ref, x_ref, xs_ref, zbuf, sem, *, n_assign):
    tb = x_ref.shape[0]
    zrows = zbuf.shape[0]

    @pl.when(pl.program_id(0) == 0)
    def _():
        zbuf[...] = jnp.zeros_like(zbuf)

        def tail_copy(j):
            return pltpu.make_async_copy(zbuf, xs_ref.at[pl.ds(n_assign + j * zrows, zrows)], sem)

        n_tail = (xs_ref.shape[0] - n_assign) // zrows
        for j in range(n_tail):
            tail_copy(j).start()
        for j in range(n_tail):
            tail_copy(j).wait()

        def pad_copy(row):
            return pltpu.make_async_copy(zbuf.at[pl.ds(0, MOE_ALIGN)],
                                         xs_ref.at[pl.ds(row, MOE_ALIGN)], sem)

        def pads(fn):
            def body(e, _):
                row = misc_ref[0, e]

                @pl.when(row >= 0)
                def _():
                    fn(pad_copy(pl.multiple_of(row, MOE_ALIGN)))
                return 0
            lax.fori_loop(0, N_EXPERTS, body, 0)

        pads(lambda c: c.start())
        pads(lambda c: c.wait())

    def row_copy(r, d):
        return pltpu.make_async_copy(x_ref.at[pl.ds(r, 1)], xs_ref.at[pl.ds(d, 1)], sem)

    def issue(r, _):
        row_copy(r, dest_ref[0, 0, 2 * r]).start()
        row_copy(r, dest_ref[0, 0, 2 * r + 1]).start()
        return 0

    lax.fori_loop(0, tb, issue, 0)

    def drain(r, _):
        row_copy(r, dest_ref[0, 0, 2 * r]).wait()
        row_copy(r, dest_ref[0, 0, 2 * r + 1]).wait()
        return 0

    lax.fori_loop(0, tb, drain, 0)


def _dispatch(dest_blocks, misc, hn, xs_rows):
    t, d = hn.shape
    tb = DISPATCH_TB
    n_assign = 2 * t
    assert (xs_rows - n_assign) % MOE_SUB == 0
    return pl.pallas_call(
        functools.partial(_dispatch_kernel, n_assign=n_assign),
        grid=(t // tb,),
        in_specs=[pl.BlockSpec((1, 1, 2 * tb), lambda i: (i, 0, 0), memory_space=pltpu.SMEM),
                  pl.BlockSpec(memory_space=pltpu.SMEM),
                  pl.BlockSpec((tb, d), lambda i: (i, 0))],
        out_specs=pl.BlockSpec(memory_space=pl.ANY),
        out_shape=jax.ShapeDtypeStruct((xs_rows, d), F32),
        scratch_shapes=[pltpu.VMEM((MOE_SUB, d), F32), pltpu.SemaphoreType.DMA(())],
        compiler_params=_cparams(("arbitrary",)),
        name="moe_dispatch",
    )(dest_blocks, misc, hn)


def _expert_kernel(be_ref, rs_ref, nv_ref, nu_ref, xs_ref, wg_ref, wu_ref, wd_ref, ys_ref,
                   gacc, uacc, wg16, wu16, wd16, ybuf, sem, *, n_assign):
    del be_ref
    i = pl.program_id(0)
    k = pl.program_id(1)
    n_used = nu_ref[0]
    sub_shift = MOE_SUB.bit_length() - 1

    def out_copies(j, fn):
        row0 = rs_ref[j]
        rows = ((nv_ref[j] + (MOE_ALIGN - 1)) // MOE_ALIGN) * MOE_ALIGN
        n_full = rows >> sub_shift
        rem = rows & (MOE_SUB - 1)

        def piece_copy(off, size):
            src_off = off if isinstance(off, int) else pl.multiple_of(off, MOE_ALIGN)
            return pltpu.make_async_copy(
                ybuf.at[pl.ds(src_off, size)],
                ys_ref.at[pl.ds(pl.multiple_of(row0 + off, MOE_ALIGN), size)], sem)

        for s in range(MOE_SBLK // MOE_SUB):
            @pl.when(s < n_full)
            def _():
                fn(piece_copy(s * MOE_SUB, MOE_SUB))
        piece = MOE_SUB // 2
        while piece >= MOE_ALIGN:
            off = n_full * MOE_SUB + (rem & (MOE_SUB - 2 * piece))

            @pl.when((rem & piece) != 0)
            def _():
                fn(piece_copy(off, piece))
            piece //= 2

    @pl.when((i == 0) & (k == 0))
    def _():
        ybuf[...] = jnp.zeros_like(ybuf)

        def tail_copy(j):
            return pltpu.make_async_copy(ybuf, ys_ref.at[pl.ds(n_assign + j * MOE_SBLK, MOE_SBLK)], sem)

        n_tail = (ys_ref.shape[0] - n_assign) // MOE_SBLK
        for j in range(n_tail):
            tail_copy(j).start()
        for j in range(n_tail):
            tail_copy(j).wait()

    @pl.when(i < n_used)
    def _():
        n_sub = (nv_ref[i] + (MOE_SUB - 1)) >> sub_shift
        wg16[...] = wg_ref[...].astype(BF16)
        wu16[...] = wu_ref[...].astype(BF16)

        @pl.when(k == 0)
        def _():
            gacc[...] = jnp.zeros_like(gacc)
            uacc[...] = jnp.zeros_like(uacc)

        def up(r, _):
            rows = pl.ds(pl.multiple_of(r * MOE_SUB, MOE_SUB), MOE_SUB)
            x = xs_ref[rows, :].astype(BF16)
            gacc[rows, :] += jnp.dot(x, wg16[...], preferred_element_type=F32)
            uacc[rows, :] += jnp.dot(x, wu16[...], preferred_element_type=F32)
            return 0

        lax.fori_loop(0, n_sub, up, 0)

        @pl.when(k == MOE_KS - 1)
        def _():
            wd16[...] = wd_ref[...].astype(BF16)

            @pl.when(i > 0)
            def _():
                out_copies(i - 1, lambda c: c.wait())

            def down(r, _):
                rows = pl.ds(pl.multiple_of(r * MOE_SUB, MOE_SUB), MOE_SUB)
                g = gacc[rows, :]
                hb = (g * _sigmoid(g)) * uacc[rows, :]
                ybuf[rows, :] = jnp.dot(hb.astype(BF16), wd16[...], preferred_element_type=F32)
                return 0

            lax.fori_loop(0, n_sub, down, 0)
            out_copies(i, lambda c: c.start())

            @pl.when(i == n_used - 1)
            def _():
                out_copies(i, lambda c: c.wait())


def _experts(step_expert, step_row, step_valid, n_used, xs, w_gate, w_up, w_down, n_assign):
    xs_rows, d = xs.shape
    n_blocks = step_expert.shape[0]
    f = w_gate.shape[-1]
    tk = d // MOE_KS
    assert (xs_rows - n_assign) % MOE_SBLK == 0

    def blk(i, nu):
        return jnp.minimum(i, nu[0] - 1)

    def kk(i, k, nu):
        return jnp.where(i < nu[0], k, MOE_KS - 1)

    grid_spec = pltpu.PrefetchScalarGridSpec(
        num_scalar_prefetch=4,
        grid=(n_blocks, MOE_KS),
        in_specs=[
            pl.BlockSpec((pl.Element(MOE_SBLK), pl.Element(tk)),
                         lambda i, k, be, rs, nv, nu: (pl.multiple_of(rs[blk(i, nu)], MOE_ALIGN),
                                                       kk(i, k, nu) * tk)),
            pl.BlockSpec((None, tk, f), lambda i, k, be, rs, nv, nu: (be[blk(i, nu)], kk(i, k, nu), 0)),
            pl.BlockSpec((None, tk, f), lambda i, k, be, rs, nv, nu: (be[blk(i, nu)], kk(i, k, nu), 0)),
            pl.BlockSpec((None, f, d), lambda i, k, be, rs, nv, nu: (be[blk(i, nu)], 0, 0)),
        ],
        out_specs=pl.BlockSpec(memory_space=pl.ANY),
        scratch_shapes=[pltpu.VMEM((MOE_SBLK, f), F32), pltpu.VMEM((MOE_SBLK, f), F32),
                        pltpu.VMEM((tk, f), BF16), pltpu.VMEM((tk, f), BF16),
                        pltpu.VMEM((f, d), BF16), pltpu.VMEM((MOE_SBLK, d), F32),
                        pltpu.SemaphoreType.DMA(())],
    )
    return pl.pallas_call(
        functools.partial(_expert_kernel, n_assign=n_assign),
        grid_spec=grid_spec,
        out_shape=jax.ShapeDtypeStruct((xs_rows, d), F32),
        compiler_params=_cparams(("arbitrary", "arbitrary")),
        name="moe_experts",
    )(step_expert, step_row, step_valid, n_used, xs, w_gate, w_up, w_down)


def _combine_kernel(dest_ref, dest_next_ref, w_ref, h_ref, g_ref, ys_ref, o_ref, ybuf, sems, *,
                    final_norm):
    tb = h_ref.shape[0]
    i = pl.program_id(0)
    slot = i % 2

    def gathers(rows_ref, s, fn):
        def body(r, _):
            for j in range(2):
                fn(pltpu.make_async_copy(ys_ref.at[pl.ds(rows_ref[0, 0, 2 * r + j], 1)],
                                         ybuf.at[s, j, pl.ds(r, 1)], sems.at[s]))
            return 0
        lax.fori_loop(0, tb, body, 0)

    @pl.when(i == 0)
    def _():
        gathers(dest_ref, 0, lambda c: c.start())

    @pl.when(i + 1 < pl.num_programs(0))
    def _():
        gathers(dest_next_ref, 1 - slot, lambda c: c.start())

    gathers(dest_ref, slot, lambda c: c.wait())

    w = w_ref[...]
    h = h_ref[...] + (w[:, 0:1] * ybuf[slot, 0] + w[:, 1:2] * ybuf[slot, 1])
    if final_norm:
        h = (h * lax.rsqrt(jnp.mean(h * h, axis=-1, keepdims=True) + EPS)) * g_ref[...]
    o_ref[...] = h


def _combine(dest_blocks, wts, h, g, ys, final_norm):
    t, d = h.shape
    tb = COMBINE_TB
    n_tiles = t // tb
    return pl.pallas_call(
        functools.partial(_combine_kernel, final_norm=final_norm),
        grid=(n_tiles,),
        in_specs=[pl.BlockSpec((1, 1, 2 * tb), lambda i: (i, 0, 0), memory_space=pltpu.SMEM),
                  pl.BlockSpec((1, 1, 2 * tb), lambda i: (jnp.minimum(i + 1, n_tiles - 1), 0, 0),
                               memory_space=pltpu.SMEM),
                  pl.BlockSpec((tb, LANES), lambda i: (i, 0)),
                  pl.BlockSpec((tb, d), lambda i: (i, 0)),
                  pl.BlockSpec((1, d), lambda i: (0, 0)),
                  pl.BlockSpec(memory_space=pl.ANY)],
        out_specs=pl.BlockSpec((tb, d), lambda i: (i, 0)),
        out_shape=jax.ShapeDtypeStruct((t, d), F32),
        scratch_shapes=[pltpu.VMEM((2, 2, tb, d), F32), pltpu.SemaphoreType.DMA((2,))],
        compiler_params=_cparams(("arbitrary",)),
        name="moe_combine",
    )(dest_blocks, dest_blocks, wts, h, g.reshape(1, d).astype(F32), ys)


def kernel(x, mem, norm_mix, w_in, conv_qk, b_gates, g_mlstm, w_proj_a, w_proj_b, w_out,
           norm_xattn, norm_mem, w_q_mem, w_kv_mem, w_o_mem, norm_moe,
           w_router_group, b_router_group, w_router_expert, b_router_expert,
           w_gate, w_up, w_down, norm_final):
    b, s, d = x.shape
    t = b * s
    n_mem = mem.shape[1]
    depth = w_in.shape[0]
    ml_qk_w = ML_HEADS * ML_QK
    ml_v_w = ML_HEADS * ML_V
    sb_w = SB_HEADS * SB_HD
    seg_a_w = 2 * ml_qk_w + 2 * ml_v_w
    n_gate_cols = 2 * ML_HEADS
    seg_b_w = 3 * sb_w + 2 * d
    n_assign = 2 * t
    n_blocks = n_assign // MOE_SBLK + N_EXPERTS
    xs_rows = n_assign + N_EXPERTS * MOE_ALIGN + MOE_SBLK

    h = x.reshape(t, d)
    mem2 = mem.reshape(b * n_mem, d)
    for l in range(depth):
        xn = _rmsnorm(h, norm_mix[l], BF16)
        w_in_t = jnp.swapaxes(w_in[l], 0, 1)
        seg_a = _matmul(xn, w_in_t, seg_a_w, BF16, transposed=True, name="in_proj_a")
        gates = _matmul(xn, w_in_t, LANES, F32, tn=LANES, col_start=seg_a_w, transposed=True,
                        name="in_proj_gates")
        seg_b = _matmul(xn, w_in_t, seg_b_w, BF16, col_start=seg_a_w + n_gate_cols, transposed=True,
                        name="in_proj_b")
        gates_t = gates[:, :n_gate_cols].reshape(b, s, 2, ML_HEADS).transpose(0, 2, 3, 1)
        gates_t = gates_t.reshape(b, 2, ML_HEADS, s // ML_CHUNK, ML_CHUNK)
        hm = _mlstm(seg_a.reshape(b, s, seg_a_w), gates_t, conv_qk[l], b_gates[l], g_mlstm[l])
        hs = _stick_breaking(seg_b.reshape(b, s, seg_b_w))
        y = _merge(hm.reshape(t, ml_v_w), hs.reshape(t, sb_w), w_proj_a[l], w_proj_b[l],
                   seg_b, 3 * sb_w, 3 * sb_w + d, d)
        h = _matmul(y, w_out[l], d, F32, res=h, name="out_proj")
        hn = _rmsnorm(h, norm_xattn[l], BF16)
        q = _matmul(hn, w_q_mem[l], d, BF16, name="xattn_q")
        memn = _rmsnorm(mem2, norm_mem[l], BF16)
        kv = _matmul(memn, w_kv_mem[l], 2 * d, BF16, name="xattn_kv")
        o = _xattn(q.reshape(b, s, d), kv.reshape(b, n_mem, 2 * d))
        h = _matmul(o.reshape(t, d), w_o_mem[l], d, F32, res=h, name="xattn_o")
        w_r = jnp.pad(jnp.concatenate([w_router_group[l], w_router_expert[l]], axis=1),
                      ((0, 0), (0, LANES - N_GROUPS - N_EXPERTS)))
        b_r = jnp.pad(jnp.concatenate([b_router_group[l], b_router_expert[l]]),
                      (0, LANES - N_GROUPS - N_EXPERTS)).reshape(1, LANES).astype(F32)
        hn3, ids, wts = _router(h, norm_moe[l], w_r, b_r)
        dest, steps, misc = _moe_meta(ids, n_blocks)
        xs = _dispatch(dest[:, :2].reshape(t // DISPATCH_TB, 1, 2 * DISPATCH_TB), misc, hn3, xs_rows)
        ys = _experts(steps[:, 0], steps[:, 1], steps[:, 2], misc[1, :1], xs,
                      w_gate[l], w_up[l], w_down[l], n_assign)
        h = _combine(dest[:, :2].reshape(t // COMBINE_TB, 1, 2 * COMBINE_TB), wts, h,
                     norm_final, ys, final_norm=(l == depth - 1))
    return h.reshape(b, s, d)
```

```python
import functools
import math

import jax
import jax.numpy as jnp
from jax import lax
from jax.experimental import pallas as pl
from jax.experimental.pallas import tpu as pltpu

F32 = jnp.float32
BF16 = jnp.bfloat16
I32 = jnp.int32
U32 = jnp.uint32

EPS = 1e-6
ML_HEADS = 8
ML_QK = 128
ML_V = 256
ML_CHUNK = 128
CONV_W = 4
ML_HPS = 2
SB_HEADS = 16
SB_HD = 128
SB_BLOCK = 128
XA_HEADS = 4
N_GROUPS = 8
EXPERTS_PER_GROUP = 8
N_EXPERTS = N_GROUPS * EXPERTS_PER_GROUP
D_EXPERT = 512

LANES = 128
SUBLANES = 8
VMEM_LIMIT = 56 * 1024 * 1024

SB_TQ = 512
SB_KB = 256
MM_TM = 1024
MM_TN = 512
MM_CAST_ROWS = 512
MOE_ALIGN = 8
MOE_SBLK = 512
MOE_SUB = 128
MOE_KS = 4
ROUTE_TM = 256
META_TB = 256
DISPATCH_TB = 256
COMBINE_TB = 128


def _cparams(sem, vmem=VMEM_LIMIT):
    return pltpu.CompilerParams(dimension_semantics=sem, vmem_limit_bytes=vmem)


def _sigmoid(x):
    return 1.0 / (1.0 + jnp.exp(-x))


def _pack_bf16_pair(lo, hi):
    lo_bits = lax.bitcast_convert_type(lo.astype(BF16).astype(F32), U32)
    hi_bits = lax.bitcast_convert_type(hi.astype(BF16).astype(F32), U32)
    return (lo_bits >> 16) | (hi_bits & jnp.uint32(0xFFFF0000))


def _unpack_bf16_pair(p):
    lo = lax.bitcast_convert_type(p << 16, F32)
    hi = lax.bitcast_convert_type(p & jnp.uint32(0xFFFF0000), F32)
    return lo, hi


def _neg_softplus(x):
    return -(jnp.maximum(x, 0.0) + jnp.log1p(jnp.exp(-jnp.abs(x))))


def _rmsnorm_kernel(x_ref, g_ref, o_ref):
    x = x_ref[...].astype(F32)
    ms = jnp.mean(x * x, axis=-1, keepdims=True)
    o_ref[...] = ((x * lax.rsqrt(ms + EPS)) * g_ref[...]).astype(o_ref.dtype)


def _rmsnorm(x, g, out_dtype, tm=256):
    m, d = x.shape
    return pl.pallas_call(
        _rmsnorm_kernel,
        grid=(m // tm,),
        in_specs=[pl.BlockSpec((tm, d), lambda i: (i, 0)),
                  pl.BlockSpec((1, d), lambda i: (0, 0))],
        out_specs=pl.BlockSpec((tm, d), lambda i: (i, 0)),
        out_shape=jax.ShapeDtypeStruct((m, d), out_dtype),
        compiler_params=_cparams(("arbitrary",)),
        name="rmsnorm",
    )(x, g.reshape(1, d).astype(F32))


def _mm_kernel(*refs, has_res, transposed):
    a_ref, w_ref = refs[0], refs[1]
    r_ref = refs[2] if has_res else None
    o_ref, w16_ref = refs[-2], refs[-1]
    k, tn = w16_ref.shape

    @pl.when(pl.program_id(1) == 0)
    def _():
        if transposed:
            for k0 in range(0, k, MM_CAST_ROWS):
                ks = pl.ds(k0, MM_CAST_ROWS)
                w16_ref[ks, :] = w_ref[:, ks].T.astype(BF16)
        else:
            w16_ref[...] = w_ref[...].astype(BF16)

    acc = jnp.dot(a_ref[...], w16_ref[...], preferred_element_type=F32)
    if has_res:
        acc = r_ref[...] + acc
    o_ref[...] = acc.astype(o_ref.dtype)


def _matmul(a, w, n_cols, out_dtype, res=None, tn=MM_TN, col_start=0, transposed=False,
            name="matmul"):
    m, k = a.shape
    tm = min(MM_TM, m)
    grid = (n_cols // tn, m // tm)
    if transposed:
        assert col_start % SUBLANES == 0
        w_spec = pl.BlockSpec((pl.Element(tn), pl.Element(k)),
                              lambda j, i: (pl.multiple_of(col_start + j * tn, SUBLANES), 0))
    else:
        base_blk, rem = divmod(col_start, tn)
        assert rem == 0
        w_spec = pl.BlockSpec((k, tn), lambda j, i: (0, base_blk + j))
    in_specs = [pl.BlockSpec((tm, k), lambda j, i: (i, 0)), w_spec]
    args = [a, w]
    if res is not None:
        in_specs.append(pl.BlockSpec((tm, tn), lambda j, i: (i, j)))
        args.append(res)
    return pl.pallas_call(
        functools.partial(_mm_kernel, has_res=res is not None, transposed=transposed),
        grid=grid,
        in_specs=in_specs,
        out_specs=pl.BlockSpec((tm, tn), lambda j, i: (i, j)),
        out_shape=jax.ShapeDtypeStruct((m, n_cols), out_dtype),
        scratch_shapes=[pltpu.VMEM((k, tn), BF16)],
        compiler_params=_cparams(("arbitrary", "arbitrary")),
        name=name,
    )(*args)


def _mlstm_kernel(bg_ref, q_ref, k_ref, v_ref, o_ref, gt_ref, cwq_ref, cwk_ref, gm_ref,
                  out_ref, qpad, kpad, bc_s, li_s, *, seq):
    head0 = pl.program_id(1) * ML_HPS
    n_chunks = seq // ML_CHUNK
    L = ML_CHUNK

    qpad[0:8, :] = jnp.zeros((8, ML_HPS * ML_QK), F32)
    kpad[0:8, :] = jnp.zeros((8, ML_HPS * ML_QK), F32)
    qpad[8:, :] = q_ref[...].astype(F32)
    kpad[8:, :] = k_ref[...].astype(F32)

    lane = lax.broadcasted_iota(I32, (n_chunks, L), 1)
    for hh in range(ML_HPS):
        li_s[hh] = gt_ref[0, hh] + bg_ref[head0 + hh]
        bc = _neg_softplus(-(gt_ref[1, hh] + bg_ref[ML_HEADS + head0 + hh]))
        for s in (1, 2, 4, 8, 16, 32, 64):
            bc = bc + jnp.where(lane >= s, pltpu.roll(bc, s, axis=1), 0.0)
        bc_s[hh] = bc

    rows = lax.broadcasted_iota(I32, (L, L), 0)
    cols = lax.broadcasted_iota(I32, (L, L), 1)
    eye = rows == cols
    causal = cols <= rows
    k_scale = 1.0 / math.sqrt(ML_QK)

    def to_col(row):
        return jnp.sum(jnp.where(eye, row, 0.0), axis=1, keepdims=True)

    def conv_silu(win, cw):
        y = (cw[0:1, :] * win[5:5 + L] + cw[1:2, :] * win[6:6 + L]
             + cw[2:3, :] * win[7:7 + L] + cw[3:4, :] * win[8:8 + L])
        return y * _sigmoid(y)

    def head_chunk(hh, c, r0, carry):
        c_st, n_st, m_st = carry
        qk_cols = pl.ds(hh * ML_QK, ML_QK)
        v_cols = pl.ds(hh * ML_V, ML_V)
        qb = conv_silu(qpad[pl.ds(r0, L + 8), qk_cols], cwq_ref[:, qk_cols])
        kb = conv_silu(kpad[pl.ds(r0, L + 8), qk_cols], cwk_ref[:, qk_cols]) * k_scale
        vb = v_ref[pl.ds(r0, L), v_cols]
        bc_row = bc_s[hh, pl.ds(c, 1), :]
        li_row = li_s[hh, pl.ds(c, 1), :]
        bc_col = to_col(bc_row)

        d = jnp.where(causal, bc_col - bc_row + li_row, -jnp.inf)
        inter = bc_col + m_st
        m_t = jnp.maximum(inter, jnp.max(d, axis=1, keepdims=True))
        w_intra = jnp.exp(d - m_t)
        w_inter = jnp.exp(inter - m_t)

        qb16 = qb.astype(BF16)
        kb16 = kb.astype(BF16)
        sc = lax.dot_general(qb16, kb16, (((1,), (1,)), ((), ())),
                             preferred_element_type=F32) * w_intra
        num = (w_inter * jnp.dot(qb16, c_st.astype(BF16), preferred_element_type=F32)
               + jnp.dot(sc.astype(BF16), vb, preferred_element_type=F32))
        den = (w_inter * jnp.sum(qb * n_st, axis=1, keepdims=True)
               + jnp.sum(sc, axis=1, keepdims=True))
        hval = num / jnp.maximum(jnp.abs(den), jnp.exp(-m_t))

        hn = hval * lax.rsqrt(jnp.mean(hval * hval, axis=1, keepdims=True) + EPS)
        og = _sigmoid(o_ref[pl.ds(r0, L), v_cols].astype(F32))
        out_ref[pl.ds(r0, L), v_cols] = (og * (hn * gm_ref[:, v_cols])).astype(out_ref.dtype)

        g = bc_row[:, L - 1:L]
        ds_row = g - bc_row + li_row
        m_new = jnp.maximum(g + m_st, jnp.max(ds_row, axis=1, keepdims=True))
        w_s = jnp.exp(ds_row - m_new)
        decay = jnp.exp(g + m_st - m_new)
        kw = kb * to_col(w_s)
        c_new = decay * c_st + jnp.dot(kw.T.astype(BF16), vb, preferred_element_type=F32)
        n_new = decay * n_st + jnp.sum(kw, axis=0, keepdims=True)
        return c_new, n_new, m_new

    def chunk(c, carry):
        r0 = pl.multiple_of(c * L, L)
        return tuple(head_chunk(hh, c, r0, carry[hh]) for hh in range(ML_HPS))

    init = (jnp.zeros((ML_QK, ML_V), F32), jnp.zeros((1, ML_QK), F32),
            jnp.full((1, 1), -1e30, F32))
    lax.fori_loop(0, n_chunks, chunk, (init,) * ML_HPS)


def _mlstm(seg_a, gates_t, conv_qk, b_gates, g_mlstm):
    b, s, _ = seg_a.shape
    h = ML_HEADS
    hp = ML_HPS
    n_chunks = s // ML_CHUNK
    groups = h // hp
    qk_w, v_w = hp * ML_QK, hp * ML_V
    return pl.pallas_call(
        functools.partial(_mlstm_kernel, seq=s),
        grid=(b, groups),
        in_specs=[
            pl.BlockSpec(memory_space=pltpu.SMEM),
            pl.BlockSpec((None, s, qk_w), lambda bi, gi: (bi, 0, gi)),
            pl.BlockSpec((None, s, qk_w), lambda bi, gi: (bi, 0, groups + gi)),
            pl.BlockSpec((None, s, v_w), lambda bi, gi: (bi, 0, groups + gi)),
            pl.BlockSpec((None, s, v_w), lambda bi, gi: (bi, 0, 2 * groups + gi)),
            pl.BlockSpec((None, 2, hp, n_chunks, ML_CHUNK), lambda bi, gi: (bi, 0, gi, 0, 0)),
            pl.BlockSpec((CONV_W, qk_w), lambda bi, gi: (0, gi)),
            pl.BlockSpec((CONV_W, qk_w), lambda bi, gi: (0, groups + gi)),
            pl.BlockSpec((1, v_w), lambda bi, gi: (0, gi)),
        ],
        out_specs=pl.BlockSpec((None, s, v_w), lambda bi, gi: (bi, 0, gi)),
        out_shape=jax.ShapeDtypeStruct((b, s, h * ML_V), BF16),
        scratch_shapes=[pltpu.VMEM((s + 8, qk_w), F32), pltpu.VMEM((s + 8, qk_w), F32),
                        pltpu.VMEM((hp, n_chunks, ML_CHUNK), F32),
                        pltpu.VMEM((hp, n_chunks, ML_CHUNK), F32)],
        compiler_params=_cparams(("arbitrary", "arbitrary")),
        name="mlstm",
    )(b_gates.astype(F32), seg_a, seg_a, seg_a, seg_a, gates_t, conv_qk.astype(F32),
      conv_qk.astype(F32), g_mlstm.reshape(1, -1).astype(F32))


def _sb_kernel(q_ref, k_ref, v_ref, o_ref, acc_s, r_s, *, seq):
    L = SB_BLOCK
    tq, kb = SB_TQ, SB_KB
    scale = 1.0 / math.sqrt(SB_HD)
    rows = lax.broadcasted_iota(I32, (tq, kb), 0)
    cols = lax.broadcasted_iota(I32, (tq, kb), 1)
    ur = lax.broadcasted_iota(I32, (L, 2 * L), 0)
    uc = lax.broadcasted_iota(I32, (L, 2 * L), 1)
    u = jnp.where((uc >= L) | (ur > uc), 1.0, 0.0).astype(BF16)

    def step(qi, c0, mask, row_lo=0):
        kj = k_ref[pl.ds(c0, kb), :]
        vj = v_ref[pl.ds(c0, kb), :]
        tile_rows = pl.ds(row_lo, tq - row_lo)
        z = lax.dot_general(qi[row_lo:], kj, (((1,), (1,)), ((), ())),
                            preferred_element_type=F32) * scale
        l1m = -(jnp.maximum(z, 0.0) + jnp.log(1.0 + jnp.exp(-jnp.abs(z))))
        lm = l1m if mask is None else jnp.where(mask, l1m, 0.0)
        lm16 = lm.astype(BF16)
        cs_far = jnp.dot(lm16[:, :L], u, preferred_element_type=F32)
        cs_near = jnp.dot(lm16[:, L:], u, preferred_element_type=F32)
        r0 = r_s[tile_rows, :]
        r1 = r0 + cs_near[:, L:]
        rest = jnp.concatenate([cs_far[:, :L] + r1, cs_near[:, :L] + r0], axis=1)
        a = jnp.exp((z + l1m) + rest)
        if mask is not None:
            a = jnp.where(mask, a, 0.0)
        acc_s[tile_rows, :] += jnp.dot(a.astype(BF16), vj, preferred_element_type=F32)
        r_s[tile_rows, :] = r1 + cs_far[:, L:]

    def qtile(t, _):
        q0 = pl.multiple_of(t * tq, tq)
        qi = q_ref[pl.ds(q0, tq), :]
        acc_s[...] = jnp.zeros_like(acc_s)
        r_s[...] = jnp.zeros_like(r_s)
        for off in range(tq - kb, -1, -kb):
            step(qi, pl.multiple_of(q0 + off, kb), (cols + off < rows)[off:], row_lo=off)

        def earlier(jj, _):
            c0 = q0 - (jj + 1) * (2 * kb)
            step(qi, pl.multiple_of(c0 + kb, kb), None)
            step(qi, pl.multiple_of(c0, kb), None)
            return 0

        lax.fori_loop(0, t * (tq // (2 * kb)), earlier, 0)
        o_ref[pl.ds(q0, tq), :] = acc_s[...].astype(o_ref.dtype)
        return 0

    lax.fori_loop(0, seq // tq, qtile, 0)


def _stick_breaking(seg_b):
    b, s, _ = seg_b.shape
    h = SB_HEADS
    return pl.pallas_call(
        functools.partial(_sb_kernel, seq=s),
        grid=(b, h),
        in_specs=[pl.BlockSpec((None, s, SB_HD), lambda bi, hi: (bi, 0, hi)),
                  pl.BlockSpec((None, s, SB_HD), lambda bi, hi: (bi, 0, h + hi)),
                  pl.BlockSpec((None, s, SB_HD), lambda bi, hi: (bi, 0, 2 * h + hi))],
        out_specs=pl.BlockSpec((None, s, SB_HD), lambda bi, hi: (bi, 0, hi)),
        out_shape=jax.ShapeDtypeStruct((b, s, h * SB_HD), BF16),
        scratch_shapes=[pltpu.VMEM((SB_TQ, SB_HD), F32), pltpu.VMEM((SB_TQ, SB_BLOCK), F32)],
        compiler_params=_cparams(("arbitrary", "arbitrary")),
        name="stick_breaking",
    )(seg_b, seg_b, seg_b)


def _merge_kernel(hm_ref, hs_ref, wa_ref, wb_ref, ga_ref, gb_ref, o_ref, wa16, wb16):
    @pl.when(pl.program_id(1) == 0)
    def _():
        wa16[...] = wa_ref[...].astype(BF16)
        wb16[...] = wb_ref[...].astype(BF16)

    ya = jnp.dot(hm_ref[...], wa16[...], preferred_element_type=F32)
    yb = jnp.dot(hs_ref[...], wb16[...], preferred_element_type=F32)
    y = _sigmoid(ga_ref[...].astype(F32)) * ya + _sigmoid(gb_ref[...].astype(F32)) * yb
    o_ref[...] = y.astype(o_ref.dtype)


def _merge(hm, hs, w_a, w_b, seg_b, gate_a_col, gate_b_col, d_model):
    m, ka = hm.shape
    kb = hs.shape[1]
    tm, tn = MM_TM, MM_TN
    ga_blk = gate_a_col // tn
    gb_blk = gate_b_col // tn
    return pl.pallas_call(
        _merge_kernel,
        grid=(d_model // tn, m // tm),
        in_specs=[pl.BlockSpec((tm, ka), lambda j, i: (i, 0)),
                  pl.BlockSpec((tm, kb), lambda j, i: (i, 0)),
                  pl.BlockSpec((ka, tn), lambda j, i: (0, j)),
                  pl.BlockSpec((kb, tn), lambda j, i: (0, j)),
                  pl.BlockSpec((tm, tn), lambda j, i: (i, ga_blk + j)),
                  pl.BlockSpec((tm, tn), lambda j, i: (i, gb_blk + j))],
        out_specs=pl.BlockSpec((tm, tn), lambda j, i: (i, j)),
        out_shape=jax.ShapeDtypeStruct((m, d_model), BF16),
        scratch_shapes=[pltpu.VMEM((ka, tn), BF16), pltpu.VMEM((kb, tn), BF16)],
        compiler_params=_cparams(("arbitrary", "arbitrary")),
        name="merge",
    )(hm, hs, w_a, w_b, seg_b, seg_b)


def _xattn_kernel(q_ref, k_ref, v_ref, o_ref, *, scale):
    s = lax.dot_general(q_ref[...], k_ref[...], (((1,), (1,)), ((), ())),
                        preferred_element_type=F32) * scale
    p = jnp.exp(s - jnp.max(s, axis=1, keepdims=True))
    p = p / jnp.sum(p, axis=1, keepdims=True)
    o_ref[...] = jnp.dot(p.astype(BF16), v_ref[...], preferred_element_type=F32).astype(o_ref.dtype)


def _xattn(q, kv, tq=1024):
    b, s, d = q.shape
    n_mem = kv.shape[1]
    hd = d // XA_HEADS
    return pl.pallas_call(
        functools.partial(_xattn_kernel, scale=1.0 / math.sqrt(hd)),
        grid=(b, XA_HEADS, s // tq),
        in_specs=[pl.BlockSpec((None, tq, hd), lambda bi, hi, i: (bi, i, hi)),
                  pl.BlockSpec((None, n_mem, hd), lambda bi, hi, i: (bi, 0, hi)),
                  pl.BlockSpec((None, n_mem, hd), lambda bi, hi, i: (bi, 0, XA_HEADS + hi))],
        out_specs=pl.BlockSpec((None, tq, hd), lambda bi, hi, i: (bi, i, hi)),
        out_shape=jax.ShapeDtypeStruct((b, s, d), BF16),
        compiler_params=_cparams(("arbitrary", "arbitrary", "arbitrary")),
        name="xattn",
    )(q, kv, kv)


def _router_kernel(h_ref, g_ref, wr_ref, br_ref, hn_ref, id_ref, wt_ref, wh_ref, wl_ref):
    @pl.when(pl.program_id(0) == 0)
    def _():
        w = wr_ref[...]
        wh = w.astype(BF16)
        wh_ref[...] = wh
        wl_ref[...] = (w - wh.astype(F32)).astype(BF16)

    x = h_ref[...]
    hn = (x * lax.rsqrt(jnp.mean(x * x, axis=-1, keepdims=True) + EPS)) * g_ref[...]
    tk = hn.shape[1] // MOE_KS
    for c in range(MOE_KS):
        hn_ref[:, c * (tk // 2):(c + 1) * (tk // 2)] = _pack_bf16_pair(
            hn[:, c * tk:c * tk + tk // 2], hn[:, c * tk + tk // 2:(c + 1) * tk])
    xh = hn.astype(BF16)
    xl = (hn - xh.astype(F32)).astype(BF16)
    logits = (jnp.dot(xh, wh_ref[...], preferred_element_type=F32)
              + (jnp.dot(xh, wl_ref[...], preferred_element_type=F32)
                 + jnp.dot(xl, wh_ref[...], preferred_element_type=F32))) + br_ref[...]

    tm = logits.shape[0]
    lane = lax.broadcasted_iota(I32, (tm, LANES), 1)
    lane_f = lane.astype(F32)
    ninf = -jnp.inf

    def first_lane_of(v, vmax):
        return jnp.min(jnp.where(v == vmax, lane_f, float(LANES)), axis=1, keepdims=True)

    gl = jnp.where(lane < N_GROUPS, logits, ninf)
    gmax = jnp.max(gl, axis=1, keepdims=True)
    g_w = 1.0 / jnp.sum(jnp.exp(gl - gmax), axis=1, keepdims=True)
    g_idx = first_lane_of(gl, gmax)
    lo = float(N_GROUPS) + g_idx * float(EXPERTS_PER_GROUP)
    in_group = (lane_f >= lo) & (lane_f < lo + float(EXPERTS_PER_GROUP))
    el = jnp.where(in_group, logits, ninf)
    m1 = jnp.max(el, axis=1, keepdims=True)
    i1 = first_lane_of(el, m1)
    el2 = jnp.where(lane_f == i1, ninf, el)
    m2 = jnp.max(el2, axis=1, keepdims=True)
    i2 = first_lane_of(el2, m2)
    denom = jnp.sum(jnp.exp(el - m1), axis=1, keepdims=True)
    p1 = 1.0 / denom
    p2 = jnp.exp(m2 - m1) / denom
    psum = p1 + p2
    w1 = g_w * (p1 / psum)
    w2 = g_w * (p2 / psum)
    ids = jnp.where(lane == 0, i1 - float(N_GROUPS), jnp.where(lane == 1, i2 - float(N_GROUPS), 0.0))
    id_ref[...] = ids.astype(I32)
    wt_ref[...] = jnp.where(lane == 0, w1, jnp.where(lane == 1, w2, 0.0))


def _router(h, g, w_r, b_r):
    t, d = h.shape
    tm = ROUTE_TM
    return pl.pallas_call(
        _router_kernel,
        grid=(t // tm,),
        in_specs=[pl.BlockSpec((tm, d), lambda i: (i, 0)),
                  pl.BlockSpec((1, d), lambda i: (0, 0)),
                  pl.BlockSpec((d, LANES), lambda i: (0, 0)),
                  pl.BlockSpec((1, LANES), lambda i: (0, 0))],
        out_specs=[pl.BlockSpec((tm, d // 2), lambda i: (i, 0)),
                   pl.BlockSpec((tm, LANES), lambda i: (i, 0)),
                   pl.BlockSpec((tm, LANES), lambda i: (i, 0))],
        out_shape=[jax.ShapeDtypeStruct((t, d // 2), U32),
                   jax.ShapeDtypeStruct((t, LANES), I32),
                   jax.ShapeDtypeStruct((t, LANES), F32)],
        scratch_shapes=[pltpu.VMEM((d, LANES), BF16), pltpu.VMEM((d, LANES), BF16)],
        compiler_params=_cparams(("arbitrary",)),
        name="router",
    )(h, g.reshape(1, d).astype(F32), w_r, b_r)


def _lane_cumsum(x):
    lane = lax.broadcasted_iota(I32, x.shape, 1)
    s = 1
    while s < LANES:
        x = x + jnp.where(lane >= s, pltpu.roll(x, s, axis=1), 0.0)
        s *= 2
    return x


def _meta_kernel(ids_ref, dest_ref, blk_ref, misc_ref, rank_s, *, n_tok, n_blocks):
    tb = META_TB
    lane = lax.broadcasted_iota(I32, (tb, LANES), 1)
    lower = jnp.where(lax.broadcasted_iota(I32, (tb, tb), 0) > lax.broadcasted_iota(I32, (tb, tb), 1),
                      1.0, 0.0).astype(BF16)

    def onehots(b):
        ids = ids_ref[pl.ds(pl.multiple_of(b * tb, tb), tb), :]
        return lane == ids[:, 0:1], lane == ids[:, 1:2]

    def lanes01(v0, v1):
        return jnp.where(lane == 0, v0, jnp.where(lane == 1, v1, 0.0))

    def count(b, carry):
        o1, o2 = onehots(b)
        cnt = jnp.where(o1, 1.0, 0.0) + jnp.where(o2, 1.0, 0.0)
        before = jnp.dot(lower, cnt.astype(BF16), preferred_element_type=F32) + carry
        r1 = jnp.sum(jnp.where(o1, before, 0.0), axis=1, keepdims=True)
        r2 = jnp.sum(jnp.where(o2, before, 0.0), axis=1, keepdims=True)
        rank_s[pl.ds(pl.multiple_of(b * tb, tb), tb), :] = lanes01(r1, r2)
        return carry + jnp.sum(cnt, axis=0, keepdims=True)

    counts = lax.fori_loop(0, n_tok // tb, count, jnp.zeros((1, LANES), F32))

    cnt8 = jnp.broadcast_to(counts, (8, LANES))
    whole = jnp.floor(cnt8 * (1.0 / MOE_ALIGN)) * MOE_ALIGN
    seg = jnp.where(cnt8 > whole, whole + MOE_ALIGN, whole)
    seg_start = _lane_cumsum(seg) - seg
    nblk = jnp.floor((cnt8 + (MOE_SBLK - 1)) * (1.0 / MOE_SBLK))
    blk_end = _lane_cumsum(nblk)
    blk_start = blk_end - nblk
    row_start = seg_start[0:1, :]

    def place(b, _):
        o1, o2 = onehots(b)
        s1 = jnp.sum(jnp.where(o1, row_start, 0.0), axis=1, keepdims=True)
        s2 = jnp.sum(jnp.where(o2, row_start, 0.0), axis=1, keepdims=True)
        sl = pl.ds(pl.multiple_of(b * tb, tb), tb)
        dest_ref[sl, :] = (rank_s[sl, :] + lanes01(s1, s2)).astype(I32)
        return 0

    lax.fori_loop(0, n_tok // tb, place, 0)

    step = lax.broadcasted_iota(I32, (n_blocks, LANES), 0).astype(F32)
    elane = lax.broadcasted_iota(I32, (n_blocks, LANES), 1)
    mine = (elane < N_EXPERTS) & (blk_start[0:1, :] <= step) & (step < blk_end[0:1, :])
    done_rows = (step - blk_start[0:1, :]) * float(MOE_SBLK)

    def pick(v):
        return jnp.sum(jnp.where(mine, v, 0.0), axis=1, keepdims=True)

    s_exp = pick(elane.astype(F32))
    s_row = pick(row_start + done_rows)
    s_val = pick(jnp.minimum(counts - done_rows, float(MOE_SBLK)))
    blk_ref[...] = jnp.where(elane == 0, s_exp, jnp.where(elane == 1, s_row, jnp.where(
        elane == 2, s_val, 0.0))).astype(I32)

    lane8 = lax.broadcasted_iota(I32, (8, LANES), 1)
    sub8 = lax.broadcasted_iota(I32, (8, LANES), 0)
    pad_row = jnp.where((cnt8 > whole) & (lane8 < N_EXPERTS), seg_start + whole, -1.0)
    n_used = jnp.broadcast_to(blk_end[:, N_EXPERTS - 1:N_EXPERTS], (8, LANES))
    misc_ref[...] = jnp.where(sub8 == 0, pad_row, jnp.where(sub8 == 1, n_used, 0.0)).astype(I32)


def _moe_meta(ids, n_blocks):
    t = ids.shape[0]
    assert MOE_ALIGN & (MOE_ALIGN - 1) == 0 and MOE_SBLK & (MOE_SBLK - 1) == 0
    return pl.pallas_call(
        functools.partial(_meta_kernel, n_tok=t, n_blocks=n_blocks),
        out_shape=[jax.ShapeDtypeStruct((t, LANES), I32),
                   jax.ShapeDtypeStruct((n_blocks, LANES), I32),
                   jax.ShapeDtypeStruct((8, LANES), I32)],
        scratch_shapes=[pltpu.VMEM((t, LANES), F32)],
        compiler_params=pltpu.CompilerParams(vmem_limit_bytes=VMEM_LIMIT),
        name="moe_meta",
    )(ids)


def _dispatch_kernel(dest_ref, misc_ref, x_ref, xs_ref, zbuf, sem, *, n_assign):
    tb = x_ref.shape[0]
    zrows = zbuf.shape[0]

    @pl.when(pl.program_id(0) == 0)
    def _():
        zbuf[...] = jnp.zeros_like(zbuf)

        def tail_copy(j):
            return pltpu.make_async_copy(zbuf, xs_ref.at[pl.ds(n_assign + j * zrows, zrows)], sem)

        n_tail = (xs_ref.shape[0] - n_assign) // zrows
        for j in range(n_tail):
            tail_copy(j).start()
        for j in range(n_tail):
            tail_copy(j).wait()

        def pad_copy(row):
            return pltpu.make_async_copy(zbuf.at[pl.ds(0, MOE_ALIGN)],
                                         xs_ref.at[pl.ds(row, MOE_ALIGN)], sem)

        def pads(fn):
            def body(e, _):
                row = misc_ref[0, e]

                @pl.when(row >= 0)
                def _():
                    fn(pad_copy(pl.multiple_of(row, MOE_ALIGN)))
                return 0
            lax.fori_loop(0, N_EXPERTS, body, 0)

        pads(lambda c: c.start())
        pads(lambda c: c.wait())

    def row_copy(r, d):
        return pltpu.make_async_copy(x_ref.at[pl.ds(r, 1)], xs_ref.at[pl.ds(d, 1)], sem)

    def scatters(fn):
        def body(g, _):
            r0 = pl.multiple_of(g * SUBLANES, SUBLANES)
            for q in range(SUBLANES):
                for j in range(2):
                    fn(row_copy(r0 + q, dest_ref[0, 0, 2 * (r0 + q) + j]))
            return 0
        lax.fori_loop(0, tb // SUBLANES, body, 0)

    scatters(lambda c: c.start())
    scatters(lambda c: c.wait())


def _dispatch(dest_blocks, misc, hn, xs_rows):
    t, d = hn.shape
    tb = DISPATCH_TB
    n_assign = 2 * t
    assert (xs_rows - n_assign) % MOE_SUB == 0
    return pl.pallas_call(
        functools.partial(_dispatch_kernel, n_assign=n_assign),
        grid=(t // tb,),
        in_specs=[pl.BlockSpec((1, 1, 2 * tb), lambda i: (i, 0, 0), memory_space=pltpu.SMEM),
                  pl.BlockSpec(memory_space=pltpu.SMEM),
                  pl.BlockSpec((tb, d), lambda i: (i, 0))],
        out_specs=pl.BlockSpec(memory_space=pl.ANY),
        out_shape=jax.ShapeDtypeStruct((xs_rows, d), hn.dtype),
        scratch_shapes=[pltpu.VMEM((MOE_SUB, d), hn.dtype), pltpu.SemaphoreType.DMA(())],
        compiler_params=_cparams(("arbitrary",)),
        name="moe_dispatch",
    )(dest_blocks, misc, hn)


def _expert_kernel(be_ref, rs_ref, nv_ref, nu_ref, xs_ref, wg_ref, wu_ref, wd_ref, ys_ref,
                   gacc, uacc, wg16, wu16, wd16, ybuf, sem, *, n_assign):
    del be_ref
    i = pl.program_id(0)
    k = pl.program_id(1)
    n_used = nu_ref[0]
    sub_shift = MOE_SUB.bit_length() - 1

    def out_copies(j, fn):
        row0 = rs_ref[j]
        rows = ((nv_ref[j] + (MOE_ALIGN - 1)) // MOE_ALIGN) * MOE_ALIGN
        n_full = rows >> sub_shift
        rem = rows & (MOE_SUB - 1)

        def piece_copy(off, size):
            src_off = off if isinstance(off, int) else pl.multiple_of(off, MOE_ALIGN)
            return pltpu.make_async_copy(
                ybuf.at[pl.ds(src_off, size)],
                ys_ref.at[pl.ds(pl.multiple_of(row0 + off, MOE_ALIGN), size)], sem)

        for s in range(MOE_SBLK // MOE_SUB):
            @pl.when(s < n_full)
            def _():
                fn(piece_copy(s * MOE_SUB, MOE_SUB))
        piece = MOE_SUB // 2
        while piece >= MOE_ALIGN:
            off = n_full * MOE_SUB + (rem & (MOE_SUB - 2 * piece))

            @pl.when((rem & piece) != 0)
            def _():
                fn(piece_copy(off, piece))
            piece //= 2

    @pl.when((i == 0) & (k == 0))
    def _():
        ybuf[...] = jnp.zeros_like(ybuf)

        def tail_copy(j):
            return pltpu.make_async_copy(ybuf, ys_ref.at[pl.ds(n_assign + j * MOE_SBLK, MOE_SBLK)], sem)

        n_tail = (ys_ref.shape[0] - n_assign) // MOE_SBLK
        for j in range(n_tail):
            tail_copy(j).start()
        for j in range(n_tail):
            tail_copy(j).wait()

    @pl.when(i < n_used)
    def _():
        n_sub = (nv_ref[i] + (MOE_SUB - 1)) >> sub_shift
        wg16[...] = wg_ref[...].astype(BF16)
        wu16[...] = wu_ref[...].astype(BF16)

        @pl.when(k == 0)
        def _():
            gacc[...] = jnp.zeros_like(gacc)
            uacc[...] = jnp.zeros_like(uacc)

        def up(r, _):
            rows = pl.ds(pl.multiple_of(r * MOE_SUB, MOE_SUB), MOE_SUB)
            x = jnp.concatenate(_unpack_bf16_pair(xs_ref[rows, :]), axis=1).astype(BF16)
            gacc[rows, :] += jnp.dot(x, wg16[...], preferred_element_type=F32)
            uacc[rows, :] += jnp.dot(x, wu16[...], preferred_element_type=F32)
            return 0

        lax.fori_loop(0, n_sub, up, 0)

        @pl.when(k == MOE_KS - 1)
        def _():
            wd16[...] = wd_ref[...].astype(BF16)

            @pl.when(i > 0)
            def _():
                out_copies(i - 1, lambda c: c.wait())

            def down(r, _):
                rows = pl.ds(pl.multiple_of(r * MOE_SUB, MOE_SUB), MOE_SUB)
                g = gacc[rows, :]
                hb = (g * _sigmoid(g)) * uacc[rows, :]
                y = jnp.dot(hb.astype(BF16), wd16[...], preferred_element_type=F32)
                half = y.shape[1] // 2
                ybuf[rows, :] = _pack_bf16_pair(y[:, :half], y[:, half:])
                return 0

            lax.fori_loop(0, n_sub, down, 0)
            out_copies(i, lambda c: c.start())

            @pl.when(i == n_used - 1)
            def _():
                out_copies(i, lambda c: c.wait())


def _experts(step_expert, step_row, step_valid, n_used, xs, w_gate, w_up, w_down, n_assign):
    xs_rows = xs.shape[0]
    n_blocks = step_expert.shape[0]
    d, f = w_gate.shape[-2:]
    tk = d // MOE_KS
    assert (xs_rows - n_assign) % MOE_SBLK == 0 and xs.shape[1] == d // 2

    def blk(i, nu):
        return jnp.minimum(i, nu[0] - 1)

    def kk(i, k, nu):
        return jnp.where(i < nu[0], k, MOE_KS - 1)

    grid_spec = pltpu.PrefetchScalarGridSpec(
        num_scalar_prefetch=4,
        grid=(n_blocks, MOE_KS),
        in_specs=[
            pl.BlockSpec((pl.Element(MOE_SBLK), pl.Element(tk // 2)),
                         lambda i, k, be, rs, nv, nu: (pl.multiple_of(rs[blk(i, nu)], MOE_ALIGN),
                                                       kk(i, k, nu) * (tk // 2))),
            pl.BlockSpec((None, tk, f), lambda i, k, be, rs, nv, nu: (be[blk(i, nu)], kk(i, k, nu), 0)),
            pl.BlockSpec((None, tk, f), lambda i, k, be, rs, nv, nu: (be[blk(i, nu)], kk(i, k, nu), 0)),
            pl.BlockSpec((None, f, d), lambda i, k, be, rs, nv, nu: (be[blk(i, nu)], 0, 0)),
        ],
        out_specs=pl.BlockSpec(memory_space=pl.ANY),
        scratch_shapes=[pltpu.VMEM((MOE_SBLK, f), F32), pltpu.VMEM((MOE_SBLK, f), F32),
                        pltpu.VMEM((tk, f), BF16), pltpu.VMEM((tk, f), BF16),
                        pltpu.VMEM((f, d), BF16), pltpu.VMEM((MOE_SBLK, d // 2), U32),
                        pltpu.SemaphoreType.DMA(())],
    )
    return pl.pallas_call(
        functools.partial(_expert_kernel, n_assign=n_assign),
        grid_spec=grid_spec,
        out_shape=jax.ShapeDtypeStruct((xs_rows, d // 2), U32),
        compiler_params=_cparams(("arbitrary", "arbitrary")),
        name="moe_experts",
    )(step_expert, step_row, step_valid, n_used, xs, w_gate, w_up, w_down)


def _combine_kernel(dest_ref, dest_next_ref, w_ref, h_ref, g_ref, ys_ref, o_ref, ybuf, sems, *,
                    final_norm):
    tb = h_ref.shape[0]
    i = pl.program_id(0)
    slot = i % 2

    def gathers(rows_ref, s, fn):
        def body(g, _):
            r0 = pl.multiple_of(g * SUBLANES, SUBLANES)
            for q in range(SUBLANES):
                for j in range(2):
                    fn(pltpu.make_async_copy(ys_ref.at[pl.ds(rows_ref[0, 0, 2 * (r0 + q) + j], 1)],
                                             ybuf.at[s, j, pl.ds(r0 + q, 1)], sems.at[s]))
            return 0
        lax.fori_loop(0, tb // SUBLANES, body, 0)

    @pl.when(i == 0)
    def _():
        gathers(dest_ref, 0, lambda c: c.start())

    @pl.when(i + 1 < pl.num_programs(0))
    def _():
        gathers(dest_next_ref, 1 - slot, lambda c: c.start())

    gathers(dest_ref, slot, lambda c: c.wait())

    w = w_ref[...]
    lo0, hi0 = _unpack_bf16_pair(ybuf[slot, 0])
    lo1, hi1 = _unpack_bf16_pair(ybuf[slot, 1])
    w0, w1 = w[:, 0:1], w[:, 1:2]
    h = h_ref[...] + jnp.concatenate([w0 * lo0 + w1 * lo1, w0 * hi0 + w1 * hi1], axis=1)
    if final_norm:
        h = (h * lax.rsqrt(jnp.mean(h * h, axis=-1, keepdims=True) + EPS)) * g_ref[...]
    o_ref[...] = h


def _combine(dest_blocks, wts, h, g, ys, final_norm):
    t, d = h.shape
    tb = COMBINE_TB
    n_tiles = t // tb
    return pl.pallas_call(
        functools.partial(_combine_kernel, final_norm=final_norm),
        grid=(n_tiles,),
        in_specs=[pl.BlockSpec((1, 1, 2 * tb), lambda i: (i, 0, 0), memory_space=pltpu.SMEM),
                  pl.BlockSpec((1, 1, 2 * tb), lambda i: (jnp.minimum(i + 1, n_tiles - 1), 0, 0),
                               memory_space=pltpu.SMEM),
                  pl.BlockSpec((tb, LANES), lambda i: (i, 0)),
                  pl.BlockSpec((tb, d), lambda i: (i, 0)),
                  pl.BlockSpec((1, d), lambda i: (0, 0)),
                  pl.BlockSpec(memory_space=pl.ANY)],
        out_specs=pl.BlockSpec((tb, d), lambda i: (i, 0)),
        out_shape=jax.ShapeDtypeStruct((t, d), F32),
        scratch_shapes=[pltpu.VMEM((2, 2, tb, d // 2), U32), pltpu.SemaphoreType.DMA((2,))],
        compiler_params=_cparams(("arbitrary",)),
        name="moe_combine",
    )(dest_blocks, dest_blocks, wts, h, g.reshape(1, d).astype(F32), ys)


def kernel(x, mem, norm_mix, w_in, conv_qk, b_gates, g_mlstm, w_proj_a, w_proj_b, w_out,
           norm_xattn, norm_mem, w_q_mem, w_kv_mem, w_o_mem, norm_moe,
           w_router_group, b_router_group, w_router_expert, b_router_expert,
           w_gate, w_up, w_down, norm_final):
    b, s, d = x.shape
    t = b * s
    n_mem = mem.shape[1]
    depth = w_in.shape[0]
    ml_qk_w = ML_HEADS * ML_QK
    ml_v_w = ML_HEADS * ML_V
    sb_w = SB_HEADS * SB_HD
    seg_a_w = 2 * ml_qk_w + 2 * ml_v_w
    n_gate_cols = 2 * ML_HEADS
    seg_b_w = 3 * sb_w + 2 * d
    n_assign = 2 * t
    n_blocks = n_assign // MOE_SBLK + N_EXPERTS
    xs_rows = n_assign + N_EXPERTS * MOE_ALIGN + MOE_SBLK

    h = x.reshape(t, d)
    mem2 = mem.reshape(b * n_mem, d)
    for l in range(depth):
        xn = _rmsnorm(h, norm_mix[l], BF16)
        w_in_t = jnp.swapaxes(w_in[l], 0, 1)
        seg_a = _matmul(xn, w_in_t, seg_a_w, BF16, transposed=True, name="in_proj_a")
        gates = _matmul(xn, w_in_t, LANES, F32, tn=LANES, col_start=seg_a_w, transposed=True,
                        name="in_proj_gates")
        seg_b = _matmul(xn, w_in_t, seg_b_w, BF16, col_start=seg_a_w + n_gate_cols, transposed=True,
                        name="in_proj_b")
        gates_t = gates[:, :n_gate_cols].reshape(b, s, 2, ML_HEADS).transpose(0, 2, 3, 1)
        gates_t = gates_t.reshape(b, 2, ML_HEADS, s // ML_CHUNK, ML_CHUNK)
        hm = _mlstm(seg_a.reshape(b, s, seg_a_w), gates_t, conv_qk[l], b_gates[l], g_mlstm[l])
        hs = _stick_breaking(seg_b.reshape(b, s, seg_b_w))
        y = _merge(hm.reshape(t, ml_v_w), hs.reshape(t, sb_w), w_proj_a[l], w_proj_b[l],
                   seg_b, 3 * sb_w, 3 * sb_w + d, d)
        h = _matmul(y, w_out[l], d, F32, res=h, name="out_proj")
        hn = _rmsnorm(h, norm_xattn[l], BF16)
        q = _matmul(hn, w_q_mem[l], d, BF16, name="xattn_q")
        memn = _rmsnorm(mem2, norm_mem[l], BF16)
        kv = _matmul(memn, w_kv_mem[l], 2 * d, BF16, name="xattn_kv")
        o = _xattn(q.reshape(b, s, d), kv.reshape(b, n_mem, 2 * d))
        h = _matmul(o.reshape(t, d), w_o_mem[l], d, F32, res=h, name="xattn_o")
        w_r = jnp.pad(jnp.concatenate([w_router_group[l], w_router_expert[l]], axis=1),
                      ((0, 0), (0, LANES - N_GROUPS - N_EXPERTS)))
        b_r = jnp.pad(jnp.concatenate([b_router_group[l], b_router_expert[l]]),
                      (0, LANES - N_GROUPS - N_EXPERTS)).reshape(1, LANES).astype(F32)
        hn3, ids, wts = _router(h, norm_moe[l], w_r, b_r)
        dest, steps, misc = _moe_meta(ids, n_blocks)
        xs = _dispatch(dest[:, :2].reshape(t // DISPATCH_TB, 1, 2 * DISPATCH_TB), misc, hn3, xs_rows)
        ys = _experts(steps[:, 0], steps[:, 1], steps[:, 2], misc[1, :1], xs,
                      w_gate[l], w_up[l], w_down[l], n_assign)
        h = _combine(dest[:, :2].reshape(t // COMBINE_TB, 1, 2 * COMBINE_TB), wts, h,
                     norm_final, ys, final_norm=(l == depth - 1))
    return h.reshape(b, s, d)
```

```python
import functools
import math

import jax
import jax.numpy as jnp
from jax import lax
from jax.experimental import pallas as pl
from jax.experimental.pallas import tpu as pltpu

F32 = jnp.float32
BF16 = jnp.bfloat16
I32 = jnp.int32
U32 = jnp.uint32

EPS = 1e-6
ML_HEADS = 8
ML_QK = 128
ML_V = 256
ML_CHUNK = 128
CONV_W = 4
ML_HPS = 2
SB_HEADS = 16
SB_HD = 128
SB_BLOCK = 128
XA_HEADS = 4
N_GROUPS = 8
EXPERTS_PER_GROUP = 8
N_EXPERTS = N_GROUPS * EXPERTS_PER_GROUP
D_EXPERT = 512

LANES = 128
SUBLANES = 8
VMEM_LIMIT = 56 * 1024 * 1024

SB_TQ = 512
SB_KB = 256
MM_TM = 1024
MM_TN = 512
MM_CAST_ROWS = 512
MOE_ALIGN = 8
MOE_SBLK = 512
MOE_SUB = 128
MOE_KS = 2
ROUTE_TM = 256
META_TB = 256
DISPATCH_TB = 256
COMBINE_TB = 128


def _cparams(sem, vmem=VMEM_LIMIT):
    return pltpu.CompilerParams(dimension_semantics=sem, vmem_limit_bytes=vmem)


def _sigmoid(x):
    return 1.0 / (1.0 + jnp.exp(-x))


def _pack_bf16_pair(lo, hi):
    lo_bits = lax.bitcast_convert_type(lo.astype(BF16).astype(F32), U32)
    hi_bits = lax.bitcast_convert_type(hi.astype(BF16).astype(F32), U32)
    return (lo_bits >> 16) | (hi_bits & jnp.uint32(0xFFFF0000))


def _unpack_bf16_pair(p):
    lo = lax.bitcast_convert_type(p << 16, F32)
    hi = lax.bitcast_convert_type(p & jnp.uint32(0xFFFF0000), F32)
    return lo, hi


def _neg_softplus(x):
    return -(jnp.maximum(x, 0.0) + jnp.log1p(jnp.exp(-jnp.abs(x))))


def _rmsnorm_kernel(x_ref, g_ref, o_ref):
    x = x_ref[...].astype(F32)
    ms = jnp.mean(x * x, axis=-1, keepdims=True)
    o_ref[...] = ((x * lax.rsqrt(ms + EPS)) * g_ref[...]).astype(o_ref.dtype)


def _rmsnorm(x, g, out_dtype, tm=256):
    m, d = x.shape
    return pl.pallas_call(
        _rmsnorm_kernel,
        grid=(m // tm,),
        in_specs=[pl.BlockSpec((tm, d), lambda i: (i, 0)),
                  pl.BlockSpec((1, d), lambda i: (0, 0))],
        out_specs=pl.BlockSpec((tm, d), lambda i: (i, 0)),
        out_shape=jax.ShapeDtypeStruct((m, d), out_dtype),
        compiler_params=_cparams(("arbitrary",)),
        name="rmsnorm",
    )(x, g.reshape(1, d).astype(F32))


def _mm_kernel(*refs, has_res, transposed):
    a_ref, w_ref = refs[0], refs[1]
    r_ref = refs[2] if has_res else None
    o_ref, w16_ref = refs[-2], refs[-1]
    k, tn = w16_ref.shape

    @pl.when(pl.program_id(1) == 0)
    def _():
        if transposed:
            for k0 in range(0, k, MM_CAST_ROWS):
                ks = pl.ds(k0, MM_CAST_ROWS)
                w16_ref[ks, :] = w_ref[:, ks].T.astype(BF16)
        else:
            w16_ref[...] = w_ref[...].astype(BF16)

    acc = jnp.dot(a_ref[...], w16_ref[...], preferred_element_type=F32)
    if has_res:
        acc = r_ref[...] + acc
    o_ref[...] = acc.astype(o_ref.dtype)


def _matmul(a, w, n_cols, out_dtype, res=None, tn=MM_TN, col_start=0, transposed=False,
            name="matmul"):
    m, k = a.shape
    tm = min(MM_TM, m)
    grid = (n_cols // tn, m // tm)
    if transposed:
        assert col_start % SUBLANES == 0
        w_spec = pl.BlockSpec((pl.Element(tn), pl.Element(k)),
                              lambda j, i: (pl.multiple_of(col_start + j * tn, SUBLANES), 0))
    else:
        base_blk, rem = divmod(col_start, tn)
        assert rem == 0
        w_spec = pl.BlockSpec((k, tn), lambda j, i: (0, base_blk + j))
    in_specs = [pl.BlockSpec((tm, k), lambda j, i: (i, 0)), w_spec]
    args = [a, w]
    if res is not None:
        in_specs.append(pl.BlockSpec((tm, tn), lambda j, i: (i, j)))
        args.append(res)
    return pl.pallas_call(
        functools.partial(_mm_kernel, has_res=res is not None, transposed=transposed),
        grid=grid,
        in_specs=in_specs,
        out_specs=pl.BlockSpec((tm, tn), lambda j, i: (i, j)),
        out_shape=jax.ShapeDtypeStruct((m, n_cols), out_dtype),
        scratch_shapes=[pltpu.VMEM((k, tn), BF16)],
        compiler_params=_cparams(("arbitrary", "arbitrary")),
        name=name,
    )(*args)


def _mlstm_kernel(bg_ref, q_ref, k_ref, v_ref, o_ref, gt_ref, cwq_ref, cwk_ref, gm_ref,
                  out_ref, qpad, kpad, bc_s, li_s, *, seq):
    head0 = pl.program_id(1) * ML_HPS
    n_chunks = seq // ML_CHUNK
    L = ML_CHUNK

    qpad[0:8, :] = jnp.zeros((8, ML_HPS * ML_QK), F32)
    kpad[0:8, :] = jnp.zeros((8, ML_HPS * ML_QK), F32)
    qpad[8:, :] = q_ref[...].astype(F32)
    kpad[8:, :] = k_ref[...].astype(F32)

    lane = lax.broadcasted_iota(I32, (n_chunks, L), 1)
    for hh in range(ML_HPS):
        li_s[hh] = gt_ref[0, hh] + bg_ref[head0 + hh]
        bc = _neg_softplus(-(gt_ref[1, hh] + bg_ref[ML_HEADS + head0 + hh]))
        for s in (1, 2, 4, 8, 16, 32, 64):
            bc = bc + jnp.where(lane >= s, pltpu.roll(bc, s, axis=1), 0.0)
        bc_s[hh] = bc

    rows = lax.broadcasted_iota(I32, (L, L), 0)
    cols = lax.broadcasted_iota(I32, (L, L), 1)
    eye = rows == cols
    causal = cols <= rows
    k_scale = 1.0 / math.sqrt(ML_QK)

    def to_col(row):
        return jnp.sum(jnp.where(eye, row, 0.0), axis=1, keepdims=True)

    def conv_silu(win, cw):
        y = (cw[0:1, :] * win[5:5 + L] + cw[1:2, :] * win[6:6 + L]
             + cw[2:3, :] * win[7:7 + L] + cw[3:4, :] * win[8:8 + L])
        return y * _sigmoid(y)

    def head_chunk(hh, c, r0, carry):
        c_st, n_st, m_st = carry
        qk_cols = pl.ds(hh * ML_QK, ML_QK)
        v_cols = pl.ds(hh * ML_V, ML_V)
        qb = conv_silu(qpad[pl.ds(r0, L + 8), qk_cols], cwq_ref[:, qk_cols])
        kb = conv_silu(kpad[pl.ds(r0, L + 8), qk_cols], cwk_ref[:, qk_cols]) * k_scale
        vb = v_ref[pl.ds(r0, L), v_cols]
        bc_row = bc_s[hh, pl.ds(c, 1), :]
        li_row = li_s[hh, pl.ds(c, 1), :]
        bc_col = to_col(bc_row)

        d = jnp.where(causal, bc_col - bc_row + li_row, -jnp.inf)
        inter = bc_col + m_st
        m_t = jnp.maximum(inter, jnp.max(d, axis=1, keepdims=True))
        w_intra = jnp.exp(d - m_t)
        w_inter = jnp.exp(inter - m_t)

        qb16 = qb.astype(BF16)
        kb16 = kb.astype(BF16)
        sc = lax.dot_general(qb16, kb16, (((1,), (1,)), ((), ())),
                             preferred_element_type=F32) * w_intra
        num = (w_inter * jnp.dot(qb16, c_st.astype(BF16), preferred_element_type=F32)
               + jnp.dot(sc.astype(BF16), vb, preferred_element_type=F32))
        den = (w_inter * jnp.sum(qb * n_st, axis=1, keepdims=True)
               + jnp.sum(sc, axis=1, keepdims=True))
        hval = num / jnp.maximum(jnp.abs(den), jnp.exp(-m_t))

        hn = hval * lax.rsqrt(jnp.mean(hval * hval, axis=1, keepdims=True) + EPS)
        og = _sigmoid(o_ref[pl.ds(r0, L), v_cols].astype(F32))
        out_ref[pl.ds(r0, L), v_cols] = (og * (hn * gm_ref[:, v_cols])).astype(out_ref.dtype)

        g = bc_row[:, L - 1:L]
        ds_row = g - bc_row + li_row
        m_new = jnp.maximum(g + m_st, jnp.max(ds_row, axis=1, keepdims=True))
        w_s = jnp.exp(ds_row - m_new)
        decay = jnp.exp(g + m_st - m_new)
        kw = kb * to_col(w_s)
        c_new = decay * c_st + jnp.dot(kw.T.astype(BF16), vb, preferred_element_type=F32)
        n_new = decay * n_st + jnp.sum(kw, axis=0, keepdims=True)
        return c_new, n_new, m_new

    def chunk(c, carry):
        r0 = pl.multiple_of(c * L, L)
        return tuple(head_chunk(hh, c, r0, carry[hh]) for hh in range(ML_HPS))

    init = (jnp.zeros((ML_QK, ML_V), F32), jnp.zeros((1, ML_QK), F32),
            jnp.full((1, 1), -1e30, F32))
    lax.fori_loop(0, n_chunks, chunk, (init,) * ML_HPS)


def _mlstm(seg_a, gates_t, conv_qk, b_gates, g_mlstm):
    b, s, _ = seg_a.shape
    h = ML_HEADS
    hp = ML_HPS
    n_chunks = s // ML_CHUNK
    groups = h // hp
    qk_w, v_w = hp * ML_QK, hp * ML_V
    return pl.pallas_call(
        functools.partial(_mlstm_kernel, seq=s),
        grid=(b, groups),
        in_specs=[
            pl.BlockSpec(memory_space=pltpu.SMEM),
            pl.BlockSpec((None, s, qk_w), lambda bi, gi: (bi, 0, gi)),
            pl.BlockSpec((None, s, qk_w), lambda bi, gi: (bi, 0, groups + gi)),
            pl.BlockSpec((None, s, v_w), lambda bi, gi: (bi, 0, groups + gi)),
            pl.BlockSpec((None, s, v_w), lambda bi, gi: (bi, 0, 2 * groups + gi)),
            pl.BlockSpec((None, 2, hp, n_chunks, ML_CHUNK), lambda bi, gi: (bi, 0, gi, 0, 0)),
            pl.BlockSpec((CONV_W, qk_w), lambda bi, gi: (0, gi)),
            pl.BlockSpec((CONV_W, qk_w), lambda bi, gi: (0, groups + gi)),
            pl.BlockSpec((1, v_w), lambda bi, gi: (0, gi)),
        ],
        out_specs=pl.BlockSpec((None, s, v_w), lambda bi, gi: (bi, 0, gi)),
        out_shape=jax.ShapeDtypeStruct((b, s, h * ML_V), BF16),
        scratch_shapes=[pltpu.VMEM((s + 8, qk_w), F32), pltpu.VMEM((s + 8, qk_w), F32),
                        pltpu.VMEM((hp, n_chunks, ML_CHUNK), F32),
                        pltpu.VMEM((hp, n_chunks, ML_CHUNK), F32)],
        compiler_params=_cparams(("arbitrary", "arbitrary")),
        name="mlstm",
    )(b_gates.astype(F32), seg_a, seg_a, seg_a, seg_a, gates_t, conv_qk.astype(F32),
      conv_qk.astype(F32), g_mlstm.reshape(1, -1).astype(F32))


def _sb_kernel(q_ref, k_ref, v_ref, o_ref, acc_s, r_s, *, seq):
    L = SB_BLOCK
    tq, kb = SB_TQ, SB_KB
    scale = 1.0 / math.sqrt(SB_HD)
    rows = lax.broadcasted_iota(I32, (tq, kb), 0)
    cols = lax.broadcasted_iota(I32, (tq, kb), 1)
    ur = lax.broadcasted_iota(I32, (L, 2 * L), 0)
    uc = lax.broadcasted_iota(I32, (L, 2 * L), 1)
    u = jnp.where((uc >= L) | (ur > uc), 1.0, 0.0).astype(BF16)

    def step(qi, c0, mask, row_lo=0):
        kj = k_ref[pl.ds(c0, kb), :]
        vj = v_ref[pl.ds(c0, kb), :]
        tile_rows = pl.ds(row_lo, tq - row_lo)
        z = lax.dot_general(qi[row_lo:], kj, (((1,), (1,)), ((), ())),
                            preferred_element_type=F32) * scale
        l1m = -(jnp.maximum(z, 0.0) + jnp.log(1.0 + jnp.exp(-jnp.abs(z))))
        lm = l1m if mask is None else jnp.where(mask, l1m, 0.0)
        lm16 = lm.astype(BF16)
        cs_far = jnp.dot(lm16[:, :L], u, preferred_element_type=F32)
        cs_near = jnp.dot(lm16[:, L:], u, preferred_element_type=F32)
        r0 = r_s[tile_rows, :]
        r1 = r0 + cs_near[:, L:]
        rest = jnp.concatenate([cs_far[:, :L] + r1, cs_near[:, :L] + r0], axis=1)
        a = jnp.exp((z + l1m) + rest)
        if mask is not None:
            a = jnp.where(mask, a, 0.0)
        acc_s[tile_rows, :] += jnp.dot(a.astype(BF16), vj, preferred_element_type=F32)
        r_s[tile_rows, :] = r1 + cs_far[:, L:]

    def qtile(t, _):
        q0 = pl.multiple_of(t * tq, tq)
        qi = q_ref[pl.ds(q0, tq), :]
        acc_s[...] = jnp.zeros_like(acc_s)
        r_s[...] = jnp.zeros_like(r_s)
        for off in range(tq - kb, -1, -kb):
            step(qi, pl.multiple_of(q0 + off, kb), (cols + off < rows)[off:], row_lo=off)

        def earlier(jj, _):
            c0 = q0 - (jj + 1) * (2 * kb)
            step(qi, pl.multiple_of(c0 + kb, kb), None)
            step(qi, pl.multiple_of(c0, kb), None)
            return 0

        lax.fori_loop(0, t * (tq // (2 * kb)), earlier, 0)
        o_ref[pl.ds(q0, tq), :] = acc_s[...].astype(o_ref.dtype)
        return 0

    lax.fori_loop(0, seq // tq, qtile, 0)


def _stick_breaking(seg_b):
    b, s, _ = seg_b.shape
    h = SB_HEADS
    assert SB_KB == 2 * SB_BLOCK and SB_TQ % (2 * SB_KB) == 0 and s % SB_TQ == 0
    return pl.pallas_call(
        functools.partial(_sb_kernel, seq=s),
        grid=(b, h),
        in_specs=[pl.BlockSpec((None, s, SB_HD), lambda bi, hi: (bi, 0, hi)),
                  pl.BlockSpec((None, s, SB_HD), lambda bi, hi: (bi, 0, h + hi)),
                  pl.BlockSpec((None, s, SB_HD), lambda bi, hi: (bi, 0, 2 * h + hi))],
        out_specs=pl.BlockSpec((None, s, SB_HD), lambda bi, hi: (bi, 0, hi)),
        out_shape=jax.ShapeDtypeStruct((b, s, h * SB_HD), BF16),
        scratch_shapes=[pltpu.VMEM((SB_TQ, SB_HD), F32), pltpu.VMEM((SB_TQ, SB_BLOCK), F32)],
        compiler_params=_cparams(("arbitrary", "arbitrary")),
        name="stick_breaking",
    )(seg_b, seg_b, seg_b)


def _merge_kernel(hm_ref, hs_ref, wa_ref, wb_ref, ga_ref, gb_ref, o_ref, wa16, wb16):
    @pl.when(pl.program_id(1) == 0)
    def _():
        wa16[...] = wa_ref[...].astype(BF16)
        wb16[...] = wb_ref[...].astype(BF16)

    ya = jnp.dot(hm_ref[...], wa16[...], preferred_element_type=F32)
    yb = jnp.dot(hs_ref[...], wb16[...], preferred_element_type=F32)
    y = _sigmoid(ga_ref[...].astype(F32)) * ya + _sigmoid(gb_ref[...].astype(F32)) * yb
    o_ref[...] = y.astype(o_ref.dtype)


def _merge(hm, hs, w_a, w_b, seg_b, gate_a_col, gate_b_col, d_model):
    m, ka = hm.shape
    kb = hs.shape[1]
    tm, tn = MM_TM, MM_TN
    ga_blk = gate_a_col // tn
    gb_blk = gate_b_col // tn
    return pl.pallas_call(
        _merge_kernel,
        grid=(d_model // tn, m // tm),
        in_specs=[pl.BlockSpec((tm, ka), lambda j, i: (i, 0)),
                  pl.BlockSpec((tm, kb), lambda j, i: (i, 0)),
                  pl.BlockSpec((ka, tn), lambda j, i: (0, j)),
                  pl.BlockSpec((kb, tn), lambda j, i: (0, j)),
                  pl.BlockSpec((tm, tn), lambda j, i: (i, ga_blk + j)),
                  pl.BlockSpec((tm, tn), lambda j, i: (i, gb_blk + j))],
        out_specs=pl.BlockSpec((tm, tn), lambda j, i: (i, j)),
        out_shape=jax.ShapeDtypeStruct((m, d_model), BF16),
        scratch_shapes=[pltpu.VMEM((ka, tn), BF16), pltpu.VMEM((kb, tn), BF16)],
        compiler_params=_cparams(("arbitrary", "arbitrary")),
        name="merge",
    )(hm, hs, w_a, w_b, seg_b, seg_b)


def _xattn_kernel(q_ref, k_ref, v_ref, o_ref, *, scale):
    s = lax.dot_general(q_ref[...], k_ref[...], (((1,), (1,)), ((), ())),
                        preferred_element_type=F32) * scale
    p = jnp.exp(s - jnp.max(s, axis=1, keepdims=True))
    p = p / jnp.sum(p, axis=1, keepdims=True)
    o_ref[...] = jnp.dot(p.astype(BF16), v_ref[...], preferred_element_type=F32).astype(o_ref.dtype)


def _xattn(q, kv, tq=1024):
    b, s, d = q.shape
    n_mem = kv.shape[1]
    hd = d // XA_HEADS
    return pl.pallas_call(
        functools.partial(_xattn_kernel, scale=1.0 / math.sqrt(hd)),
        grid=(b, XA_HEADS, s // tq),
        in_specs=[pl.BlockSpec((None, tq, hd), lambda bi, hi, i: (bi, i, hi)),
                  pl.BlockSpec((None, n_mem, hd), lambda bi, hi, i: (bi, 0, hi)),
                  pl.BlockSpec((None, n_mem, hd), lambda bi, hi, i: (bi, 0, XA_HEADS + hi))],
        out_specs=pl.BlockSpec((None, tq, hd), lambda bi, hi, i: (bi, i, hi)),
        out_shape=jax.ShapeDtypeStruct((b, s, d), BF16),
        compiler_params=_cparams(("arbitrary", "arbitrary", "arbitrary")),
        name="xattn",
    )(q, kv, kv)


def _router_kernel(h_ref, g_ref, wr_ref, br_ref, hn_ref, id_ref, wt_ref, wh_ref, wl_ref):
    @pl.when(pl.program_id(0) == 0)
    def _():
        w = wr_ref[...]
        wh = w.astype(BF16)
        wh_ref[...] = wh
        wl_ref[...] = (w - wh.astype(F32)).astype(BF16)

    x = h_ref[...]
    hn = (x * lax.rsqrt(jnp.mean(x * x, axis=-1, keepdims=True) + EPS)) * g_ref[...]
    tk = hn.shape[1] // MOE_KS
    for c in range(MOE_KS):
        hn_ref[:, c * (tk // 2):(c + 1) * (tk // 2)] = _pack_bf16_pair(
            hn[:, c * tk:c * tk + tk // 2], hn[:, c * tk + tk // 2:(c + 1) * tk])
    xh = hn.astype(BF16)
    xl = (hn - xh.astype(F32)).astype(BF16)
    logits = (jnp.dot(xh, wh_ref[...], preferred_element_type=F32)
              + (jnp.dot(xh, wl_ref[...], preferred_element_type=F32)
                 + jnp.dot(xl, wh_ref[...], preferred_element_type=F32))) + br_ref[...]

    tm = logits.shape[0]
    lane = lax.broadcasted_iota(I32, (tm, LANES), 1)
    lane_f = lane.astype(F32)
    ninf = -jnp.inf

    def first_lane_of(v, vmax):
        return jnp.min(jnp.where(v == vmax, lane_f, float(LANES)), axis=1, keepdims=True)

    gl = jnp.where(lane < N_GROUPS, logits, ninf)
    gmax = jnp.max(gl, axis=1, keepdims=True)
    g_w = 1.0 / jnp.sum(jnp.exp(gl - gmax), axis=1, keepdims=True)
    g_idx = first_lane_of(gl, gmax)
    lo = float(N_GROUPS) + g_idx * float(EXPERTS_PER_GROUP)
    in_group = (lane_f >= lo) & (lane_f < lo + float(EXPERTS_PER_GROUP))
    el = jnp.where(in_group, logits, ninf)
    m1 = jnp.max(el, axis=1, keepdims=True)
    i1 = first_lane_of(el, m1)
    el2 = jnp.where(lane_f == i1, ninf, el)
    m2 = jnp.max(el2, axis=1, keepdims=True)
    i2 = first_lane_of(el2, m2)
    denom = jnp.sum(jnp.exp(el - m1), axis=1, keepdims=True)
    p1 = 1.0 / denom
    p2 = jnp.exp(m2 - m1) / denom
    psum = p1 + p2
    w1 = g_w * (p1 / psum)
    w2 = g_w * (p2 / psum)
    ids = jnp.where(lane == 0, i1 - float(N_GROUPS), jnp.where(lane == 1, i2 - float(N_GROUPS), 0.0))
    id_ref[...] = ids.astype(I32)
    wt_ref[...] = jnp.where(lane == 0, w1, jnp.where(lane == 1, w2, 0.0))


def _router(h, g, w_r, b_r):
    t, d = h.shape
    tm = ROUTE_TM
    return pl.pallas_call(
        _router_kernel,
        grid=(t // tm,),
        in_specs=[pl.BlockSpec((tm, d), lambda i: (i, 0)),
                  pl.BlockSpec((1, d), lambda i: (0, 0)),
                  pl.BlockSpec((d, LANES), lambda i: (0, 0)),
                  pl.BlockSpec((1, LANES), lambda i: (0, 0))],
        out_specs=[pl.BlockSpec((tm, d // 2), lambda i: (i, 0)),
                   pl.BlockSpec((tm, LANES), lambda i: (i, 0)),
                   pl.BlockSpec((tm, LANES), lambda i: (i, 0))],
        out_shape=[jax.ShapeDtypeStruct((t, d // 2), U32),
                   jax.ShapeDtypeStruct((t, LANES), I32),
                   jax.ShapeDtypeStruct((t, LANES), F32)],
        scratch_shapes=[pltpu.VMEM((d, LANES), BF16), pltpu.VMEM((d, LANES), BF16)],
        compiler_params=_cparams(("arbitrary",)),
        name="router",
    )(h, g.reshape(1, d).astype(F32), w_r, b_r)


def _lane_cumsum(x):
    lane = lax.broadcasted_iota(I32, x.shape, 1)
    s = 1
    while s < LANES:
        x = x + jnp.where(lane >= s, pltpu.roll(x, s, axis=1), 0.0)
        s *= 2
    return x


def _meta_kernel(ids_ref, dest_ref, blk_ref, misc_ref, rank_s, *, n_tok, n_blocks):
    tb = META_TB
    lane = lax.broadcasted_iota(I32, (tb, LANES), 1)
    lower = jnp.where(lax.broadcasted_iota(I32, (tb, tb), 0) > lax.broadcasted_iota(I32, (tb, tb), 1),
                      1.0, 0.0).astype(BF16)

    def onehots(b):
        ids = ids_ref[pl.ds(pl.multiple_of(b * tb, tb), tb), :]
        return lane == ids[:, 0:1], lane == ids[:, 1:2]

    def lanes01(v0, v1):
        return jnp.where(lane == 0, v0, jnp.where(lane == 1, v1, 0.0))

    def count(b, carry):
        o1, o2 = onehots(b)
        cnt = jnp.where(o1, 1.0, 0.0) + jnp.where(o2, 1.0, 0.0)
        before = jnp.dot(lower, cnt.astype(BF16), preferred_element_type=F32) + carry
        r1 = jnp.sum(jnp.where(o1, before, 0.0), axis=1, keepdims=True)
        r2 = jnp.sum(jnp.where(o2, before, 0.0), axis=1, keepdims=True)
        rank_s[pl.ds(pl.multiple_of(b * tb, tb), tb), :] = lanes01(r1, r2)
        return carry + jnp.sum(cnt, axis=0, keepdims=True)

    counts = lax.fori_loop(0, n_tok // tb, count, jnp.zeros((1, LANES), F32))

    cnt8 = jnp.broadcast_to(counts, (8, LANES))
    whole = jnp.floor(cnt8 * (1.0 / MOE_ALIGN)) * MOE_ALIGN
    seg = jnp.where(cnt8 > whole, whole + MOE_ALIGN, whole)
    seg_start = _lane_cumsum(seg) - seg
    nblk = jnp.floor((cnt8 + (MOE_SBLK - 1)) * (1.0 / MOE_SBLK))
    blk_end = _lane_cumsum(nblk)
    blk_start = blk_end - nblk
    row_start = seg_start[0:1, :]

    def place(b, _):
        o1, o2 = onehots(b)
        s1 = jnp.sum(jnp.where(o1, row_start, 0.0), axis=1, keepdims=True)
        s2 = jnp.sum(jnp.where(o2, row_start, 0.0), axis=1, keepdims=True)
        sl = pl.ds(pl.multiple_of(b * tb, tb), tb)
        dest_ref[sl, :] = (rank_s[sl, :] + lanes01(s1, s2)).astype(I32)
        return 0

    lax.fori_loop(0, n_tok // tb, place, 0)

    step = lax.broadcasted_iota(I32, (n_blocks, LANES), 0).astype(F32)
    elane = lax.broadcasted_iota(I32, (n_blocks, LANES), 1)
    mine = (elane < N_EXPERTS) & (blk_start[0:1, :] <= step) & (step < blk_end[0:1, :])
    done_rows = (step - blk_start[0:1, :]) * float(MOE_SBLK)

    def pick(v):
        return jnp.sum(jnp.where(mine, v, 0.0), axis=1, keepdims=True)

    s_exp = pick(elane.astype(F32))
    s_row = pick(row_start + done_rows)
    s_val = pick(jnp.minimum(counts - done_rows, float(MOE_SBLK)))
    blk_ref[...] = jnp.where(elane == 0, s_exp, jnp.where(elane == 1, s_row, jnp.where(
        elane == 2, s_val, 0.0))).astype(I32)

    lane8 = lax.broadcasted_iota(I32, (8, LANES), 1)
    sub8 = lax.broadcasted_iota(I32, (8, LANES), 0)
    pad_row = jnp.where((cnt8 > whole) & (lane8 < N_EXPERTS), seg_start + whole, -1.0)
    n_used = jnp.broadcast_to(blk_end[:, N_EXPERTS - 1:N_EXPERTS], (8, LANES))
    misc_ref[...] = jnp.where(sub8 == 0, pad_row, jnp.where(sub8 == 1, n_used, 0.0)).astype(I32)


def _moe_meta(ids, n_blocks):
    t = ids.shape[0]
    assert MOE_ALIGN & (MOE_ALIGN - 1) == 0 and MOE_SBLK & (MOE_SBLK - 1) == 0
    return pl.pallas_call(
        functools.partial(_meta_kernel, n_tok=t, n_blocks=n_blocks),
        out_shape=[jax.ShapeDtypeStruct((t, LANES), I32),
                   jax.ShapeDtypeStruct((n_blocks, LANES), I32),
                   jax.ShapeDtypeStruct((8, LANES), I32)],
        scratch_shapes=[pltpu.VMEM((t, LANES), F32)],
        compiler_params=pltpu.CompilerParams(vmem_limit_bytes=VMEM_LIMIT),
        name="moe_meta",
    )(ids)


def _dispatch_kernel(dest_ref, misc_ref, x_ref, xs_ref, zbuf, sem, *, n_assign):
    tb = x_ref.shape[0]
    zrows = zbuf.shape[0]

    @pl.when(pl.program_id(0) == 0)
    def _():
        zbuf[...] = jnp.zeros_like(zbuf)

        def tail_copy(j):
            return pltpu.make_async_copy(zbuf, xs_ref.at[pl.ds(n_assign + j * zrows, zrows)], sem)

        n_tail = (xs_ref.shape[0] - n_assign) // zrows
        for j in range(n_tail):
            tail_copy(j).start()
        for j in range(n_tail):
            tail_copy(j).wait()

        def pad_copy(row):
            return pltpu.make_async_copy(zbuf.at[pl.ds(0, MOE_ALIGN)],
                                         xs_ref.at[pl.ds(row, MOE_ALIGN)], sem)

        def pads(fn):
            def body(e, _):
                row = misc_ref[0, e]

                @pl.when(row >= 0)
                def _():
                    fn(pad_copy(pl.multiple_of(row, MOE_ALIGN)))
                return 0
            lax.fori_loop(0, N_EXPERTS, body, 0)

        pads(lambda c: c.start())
        pads(lambda c: c.wait())

    def row_copy(r, d):
        return pltpu.make_async_copy(x_ref.at[pl.ds(r, 1)], xs_ref.at[pl.ds(d, 1)], sem)

    def scatters(fn):
        def body(g, _):
            r0 = pl.multiple_of(g * SUBLANES, SUBLANES)
            for q in range(SUBLANES):
                for j in range(2):
                    fn(row_copy(r0 + q, dest_ref[0, 0, 2 * (r0 + q) + j]))
            return 0
        lax.fori_loop(0, tb // SUBLANES, body, 0)

    scatters(lambda c: c.start())
    scatters(lambda c: c.wait())


def _dispatch(dest_blocks, misc, hn, xs_rows):
    t, d = hn.shape
    tb = DISPATCH_TB
    n_assign = 2 * t
    assert (xs_rows - n_assign) % MOE_SUB == 0
    return pl.pallas_call(
        functools.partial(_dispatch_kernel, n_assign=n_assign),
        grid=(t // tb,),
        in_specs=[pl.BlockSpec((1, 1, 2 * tb), lambda i: (i, 0, 0), memory_space=pltpu.SMEM),
                  pl.BlockSpec(memory_space=pltpu.SMEM),
                  pl.BlockSpec((tb, d), lambda i: (i, 0))],
        out_specs=pl.BlockSpec(memory_space=pl.ANY),
        out_shape=jax.ShapeDtypeStruct((xs_rows, d), hn.dtype),
        scratch_shapes=[pltpu.VMEM((MOE_SUB, d), hn.dtype), pltpu.SemaphoreType.DMA(())],
        compiler_params=_cparams(("arbitrary",)),
        name="moe_dispatch",
    )(dest_blocks, misc, hn)


def _expert_kernel(be_ref, rs_ref, nv_ref, nu_ref, xs_ref, wg_ref, wu_ref, wd_ref, ys_ref,
                   gacc, uacc, wg16, wu16, wd16, ybuf, sem, *, n_assign):
    del be_ref
    i = pl.program_id(0)
    k = pl.program_id(1)
    n_used = nu_ref[0]
    sub_shift = MOE_SUB.bit_length() - 1

    def out_copies(j, fn):
        row0 = rs_ref[j]
        rows = ((nv_ref[j] + (MOE_ALIGN - 1)) // MOE_ALIGN) * MOE_ALIGN
        n_full = rows >> sub_shift
        rem = rows & (MOE_SUB - 1)

        def piece_copy(off, size):
            src_off = off if isinstance(off, int) else pl.multiple_of(off, MOE_ALIGN)
            return pltpu.make_async_copy(
                ybuf.at[pl.ds(src_off, size)],
                ys_ref.at[pl.ds(pl.multiple_of(row0 + off, MOE_ALIGN), size)], sem)

        for s in range(MOE_SBLK // MOE_SUB):
            @pl.when(s < n_full)
            def _():
                fn(piece_copy(s * MOE_SUB, MOE_SUB))
        piece = MOE_SUB // 2
        while piece >= MOE_ALIGN:
            off = n_full * MOE_SUB + (rem & (MOE_SUB - 2 * piece))

            @pl.when((rem & piece) != 0)
            def _():
                fn(piece_copy(off, piece))
            piece //= 2

    @pl.when((i == 0) & (k == 0))
    def _():
        ybuf[...] = jnp.zeros_like(ybuf)

        def tail_copy(j):
            return pltpu.make_async_copy(ybuf, ys_ref.at[pl.ds(n_assign + j * MOE_SBLK, MOE_SBLK)], sem)

        n_tail = (ys_ref.shape[0] - n_assign) // MOE_SBLK
        for j in range(n_tail):
            tail_copy(j).start()
        for j in range(n_tail):
            tail_copy(j).wait()

    @pl.when(i < n_used)
    def _():
        n_sub = (nv_ref[i] + (MOE_SUB - 1)) >> sub_shift
        wg16[...] = wg_ref[...].astype(BF16)
        wu16[...] = wu_ref[...].astype(BF16)

        @pl.when(k == 0)
        def _():
            gacc[...] = jnp.zeros_like(gacc)
            uacc[...] = jnp.zeros_like(uacc)

        def up(r, _):
            rows = pl.ds(pl.multiple_of(r * MOE_SUB, MOE_SUB), MOE_SUB)
            x = jnp.concatenate(_unpack_bf16_pair(xs_ref[rows, :]), axis=1).astype(BF16)
            gacc[rows, :] += jnp.dot(x, wg16[...], preferred_element_type=F32)
            uacc[rows, :] += jnp.dot(x, wu16[...], preferred_element_type=F32)
            return 0

        lax.fori_loop(0, n_sub, up, 0)

        @pl.when(k == MOE_KS - 1)
        def _():
            wd16[...] = wd_ref[...].astype(BF16)

            @pl.when(i > 0)
            def _():
                out_copies(i - 1, lambda c: c.wait())

            def down(r, _):
                rows = pl.ds(pl.multiple_of(r * MOE_SUB, MOE_SUB), MOE_SUB)
                g = gacc[rows, :]
                hb = (g * _sigmoid(g)) * uacc[rows, :]
                y = jnp.dot(hb.astype(BF16), wd16[...], preferred_element_type=F32)
                half = y.shape[1] // 2
                ybuf[rows, :] = _pack_bf16_pair(y[:, :half], y[:, half:])
                return 0

            lax.fori_loop(0, n_sub, down, 0)
            out_copies(i, lambda c: c.start())

            @pl.when(i == n_used - 1)
            def _():
                out_copies(i, lambda c: c.wait())


def _experts(step_expert, step_row, step_valid, n_used, xs, w_gate, w_up, w_down, n_assign):
    xs_rows = xs.shape[0]
    n_blocks = step_expert.shape[0]
    d, f = w_gate.shape[-2:]
    tk = d // MOE_KS
    assert (xs_rows - n_assign) % MOE_SBLK == 0 and xs.shape[1] == d // 2

    def blk(i, nu):
        return jnp.minimum(i, nu[0] - 1)

    def kk(i, k, nu):
        return jnp.where(i < nu[0], k, MOE_KS - 1)

    grid_spec = pltpu.PrefetchScalarGridSpec(
        num_scalar_prefetch=4,
        grid=(n_blocks, MOE_KS),
        in_specs=[
            pl.BlockSpec((pl.Element(MOE_SBLK), pl.Element(tk // 2)),
                         lambda i, k, be, rs, nv, nu: (pl.multiple_of(rs[blk(i, nu)], MOE_ALIGN),
                                                       kk(i, k, nu) * (tk // 2))),
            pl.BlockSpec((None, tk, f), lambda i, k, be, rs, nv, nu: (be[blk(i, nu)], kk(i, k, nu), 0)),
            pl.BlockSpec((None, tk, f), lambda i, k, be, rs, nv, nu: (be[blk(i, nu)], kk(i, k, nu), 0)),
            pl.BlockSpec((None, f, d), lambda i, k, be, rs, nv, nu: (be[blk(i, nu)], 0, 0)),
        ],
        out_specs=pl.BlockSpec(memory_space=pl.ANY),
        scratch_shapes=[pltpu.VMEM((MOE_SBLK, f), F32), pltpu.VMEM((MOE_SBLK, f), F32),
                        pltpu.VMEM((tk, f), BF16), pltpu.VMEM((tk, f), BF16),
                        pltpu.VMEM((f, d), BF16), pltpu.VMEM((MOE_SBLK, d // 2), U32),
                        pltpu.SemaphoreType.DMA(())],
    )
    return pl.pallas_call(
        functools.partial(_expert_kernel, n_assign=n_assign),
        grid_spec=grid_spec,
        out_shape=jax.ShapeDtypeStruct((xs_rows, d // 2), U32),
        compiler_params=_cparams(("arbitrary", "arbitrary")),
        name="moe_experts",
    )(step_expert, step_row, step_valid, n_used, xs, w_gate, w_up, w_down)


def _combine_kernel(dest_ref, dest_next_ref, w_ref, h_ref, g_ref, ys_ref, o_ref, ybuf, sems, *,
                    final_norm):
    tb = h_ref.shape[0]
    i = pl.program_id(0)
    slot = i % 2

    def gathers(rows_ref, s, fn):
        def body(g, _):
            r0 = pl.multiple_of(g * SUBLANES, SUBLANES)
            for q in range(SUBLANES):
                for j in range(2):
                    fn(pltpu.make_async_copy(ys_ref.at[pl.ds(rows_ref[0, 0, 2 * (r0 + q) + j], 1)],
                                             ybuf.at[s, j, pl.ds(r0 + q, 1)], sems.at[s]))
            return 0
        lax.fori_loop(0, tb // SUBLANES, body, 0)

    @pl.when(i == 0)
    def _():
        gathers(dest_ref, 0, lambda c: c.start())

    @pl.when(i + 1 < pl.num_programs(0))
    def _():
        gathers(dest_next_ref, 1 - slot, lambda c: c.start())

    gathers(dest_ref, slot, lambda c: c.wait())

    w = w_ref[...]
    lo0, hi0 = _unpack_bf16_pair(ybuf[slot, 0])
    lo1, hi1 = _unpack_bf16_pair(ybuf[slot, 1])
    w0, w1 = w[:, 0:1], w[:, 1:2]
    h = h_ref[...] + jnp.concatenate([w0 * lo0 + w1 * lo1, w0 * hi0 + w1 * hi1], axis=1)
    if final_norm:
        h = (h * lax.rsqrt(jnp.mean(h * h, axis=-1, keepdims=True) + EPS)) * g_ref[...]
    o_ref[...] = h


def _combine(dest_blocks, wts, h, g, ys, final_norm):
    t, d = h.shape
    tb = COMBINE_TB
    n_tiles = t // tb
    return pl.pallas_call(
        functools.partial(_combine_kernel, final_norm=final_norm),
        grid=(n_tiles,),
        in_specs=[pl.BlockSpec((1, 1, 2 * tb), lambda i: (i, 0, 0), memory_space=pltpu.SMEM),
                  pl.BlockSpec((1, 1, 2 * tb), lambda i: (jnp.minimum(i + 1, n_tiles - 1), 0, 0),
                               memory_space=pltpu.SMEM),
                  pl.BlockSpec((tb, LANES), lambda i: (i, 0)),
                  pl.BlockSpec((tb, d), lambda i: (i, 0)),
                  pl.BlockSpec((1, d), lambda i: (0, 0)),
                  pl.BlockSpec(memory_space=pl.ANY)],
        out_specs=pl.BlockSpec((tb, d), lambda i: (i, 0)),
        out_shape=jax.ShapeDtypeStruct((t, d), F32),
        scratch_shapes=[pltpu.VMEM((2, 2, tb, d // 2), U32), pltpu.SemaphoreType.DMA((2,))],
        compiler_params=_cparams(("arbitrary",)),
        name="moe_combine",
    )(dest_blocks, dest_blocks, wts, h, g.reshape(1, d).astype(F32), ys)


def kernel(x, mem, norm_mix, w_in, conv_qk, b_gates, g_mlstm, w_proj_a, w_proj_b, w_out,
           norm_xattn, norm_mem, w_q_mem, w_kv_mem, w_o_mem, norm_moe,
           w_router_group, b_router_group, w_router_expert, b_router_expert,
           w_gate, w_up, w_down, norm_final):
    b, s, d = x.shape
    t = b * s
    n_mem = mem.shape[1]
    depth = w_in.shape[0]
    ml_qk_w = ML_HEADS * ML_QK
    ml_v_w = ML_HEADS * ML_V
    sb_w = SB_HEADS * SB_HD
    seg_a_w = 2 * ml_qk_w + 2 * ml_v_w
    n_gate_cols = 2 * ML_HEADS
    seg_b_w = 3 * sb_w + 2 * d
    n_assign = 2 * t
    n_blocks = n_assign // MOE_SBLK + N_EXPERTS
    xs_rows = n_assign + N_EXPERTS * MOE_ALIGN + MOE_SBLK

    h = x.reshape(t, d)
    mem2 = mem.reshape(b * n_mem, d)
    for l in range(depth):
        xn = _rmsnorm(h, norm_mix[l], BF16)
        w_in_t = jnp.swapaxes(w_in[l], 0, 1)
        seg_a = _matmul(xn, w_in_t, seg_a_w, BF16, transposed=True, name="in_proj_a")
        gates = _matmul(xn, w_in_t, LANES, F32, tn=LANES, col_start=seg_a_w, transposed=True,
                        name="in_proj_gates")
        seg_b = _matmul(xn, w_in_t, seg_b_w, BF16, col_start=seg_a_w + n_gate_cols, transposed=True,
                        name="in_proj_b")
        gates_t = gates[:, :n_gate_cols].reshape(b, s, 2, ML_HEADS).transpose(0, 2, 3, 1)
        gates_t = gates_t.reshape(b, 2, ML_HEADS, s // ML_CHUNK, ML_CHUNK)
        hm = _mlstm(seg_a.reshape(b, s, seg_a_w), gates_t, conv_qk[l], b_gates[l], g_mlstm[l])
        hs = _stick_breaking(seg_b.reshape(b, s, seg_b_w))
        y = _merge(hm.reshape(t, ml_v_w), hs.reshape(t, sb_w), w_proj_a[l], w_proj_b[l],
                   seg_b, 3 * sb_w, 3 * sb_w + d, d)
        h = _matmul(y, w_out[l], d, F32, res=h, name="out_proj")
        hn = _rmsnorm(h, norm_xattn[l], BF16)
        q = _matmul(hn, w_q_mem[l], d, BF16, name="xattn_q")
        memn = _rmsnorm(mem2, norm_mem[l], BF16)
        kv = _matmul(memn, w_kv_mem[l], 2 * d, BF16, name="xattn_kv")
        o = _xattn(q.reshape(b, s, d), kv.reshape(b, n_mem, 2 * d))
        h = _matmul(o.reshape(t, d), w_o_mem[l], d, F32, res=h, name="xattn_o")
        w_r = jnp.pad(jnp.concatenate([w_router_group[l], w_router_expert[l]], axis=1),
                      ((0, 0), (0, LANES - N_GROUPS - N_EXPERTS)))
        b_r = jnp.pad(jnp.concatenate([b_router_group[l], b_router_expert[l]]),
                      (0, LANES - N_GROUPS - N_EXPERTS)).reshape(1, LANES).astype(F32)
        hn3, ids, wts = _router(h, norm_moe[l], w_r, b_r)
        dest, steps, misc = _moe_meta(ids, n_blocks)
        xs = _dispatch(dest[:, :2].reshape(t // DISPATCH_TB, 1, 2 * DISPATCH_TB), misc, hn3, xs_rows)
        ys = _experts(steps[:, 0], steps[:, 1], steps[:, 2], misc[1, :1], xs,
                      w_gate[l], w_up[l], w_down[l], n_assign)
        h = _combine(dest[:, :2].reshape(t // COMBINE_TB, 1, 2 * COMBINE_TB), wts, h,
                     norm_final, ys, final_norm=(l == depth - 1))
    return h.reshape(b, s, d)
```

```python
import functools
import math

import jax
import jax.numpy as jnp
from jax import lax
from jax.experimental import pallas as pl
from jax.experimental.pallas import tpu as pltpu

F32 = jnp.float32
BF16 = jnp.bfloat16
I32 = jnp.int32
U32 = jnp.uint32

EPS = 1e-6
ML_HEADS = 8
ML_QK = 128
ML_V = 256
ML_CHUNK = 128
CONV_W = 4
ML_HPS = 2
SB_HEADS = 16
SB_HD = 128
SB_BLOCK = 128
XA_HEADS = 4
N_GROUPS = 8
EXPERTS_PER_GROUP = 8
N_EXPERTS = N_GROUPS * EXPERTS_PER_GROUP
D_EXPERT = 512

LANES = 128
SUBLANES = 8
VMEM_LIMIT = 56 * 1024 * 1024

SB_TQ = 512
SB_KB = 256
MM_TM = 1024
MM_TN = 512
MM_TM_WIDE = 512
MM_TN_WIDE = 1024
MM_CAST_ROWS = 512
MOE_ALIGN = 8
MOE_SBLK = 512
MOE_SUB = 128
MOE_KS = 2
ROUTE_TM = 256
META_TB = 256
DISPATCH_TB = 256
COMBINE_TB = 128


def _cparams(sem, vmem=VMEM_LIMIT):
    return pltpu.CompilerParams(dimension_semantics=sem, vmem_limit_bytes=vmem)


def _sigmoid(x):
    return 1.0 / (1.0 + jnp.exp(-x))


def _pack_bf16_pair(lo, hi):
    lo_bits = lax.bitcast_convert_type(lo.astype(BF16).astype(F32), U32)
    hi_bits = lax.bitcast_convert_type(hi.astype(BF16).astype(F32), U32)
    return (lo_bits >> 16) | (hi_bits & jnp.uint32(0xFFFF0000))


def _unpack_bf16_pair(p):
    lo = lax.bitcast_convert_type(p << 16, F32)
    hi = lax.bitcast_convert_type(p & jnp.uint32(0xFFFF0000), F32)
    return lo, hi


def _neg_softplus(x):
    return -(jnp.maximum(x, 0.0) + jnp.log1p(jnp.exp(-jnp.abs(x))))


def _rmsnorm_kernel(x_ref, g_ref, o_ref):
    x = x_ref[...].astype(F32)
    ms = jnp.mean(x * x, axis=-1, keepdims=True)
    o_ref[...] = ((x * lax.rsqrt(ms + EPS)) * g_ref[...]).astype(o_ref.dtype)


def _rmsnorm(x, g, out_dtype, tm=256):
    m, d = x.shape
    return pl.pallas_call(
        _rmsnorm_kernel,
        grid=(m // tm,),
        in_specs=[pl.BlockSpec((tm, d), lambda i: (i, 0)),
                  pl.BlockSpec((1, d), lambda i: (0, 0))],
        out_specs=pl.BlockSpec((tm, d), lambda i: (i, 0)),
        out_shape=jax.ShapeDtypeStruct((m, d), out_dtype),
        compiler_params=_cparams(("arbitrary",)),
        name="rmsnorm",
    )(x, g.reshape(1, d).astype(F32))


def _mm_kernel(*refs, has_res, transposed):
    a_ref, w_ref = refs[0], refs[1]
    r_ref = refs[2] if has_res else None
    o_ref, w16_ref = refs[-2], refs[-1]
    k, tn = w16_ref.shape

    @pl.when(pl.program_id(1) == 0)
    def _():
        if transposed:
            for k0 in range(0, k, MM_CAST_ROWS):
                ks = pl.ds(k0, MM_CAST_ROWS)
                w16_ref[ks, :] = w_ref[:, ks].T.astype(BF16)
        else:
            w16_ref[...] = w_ref[...].astype(BF16)

    acc = jnp.dot(a_ref[...], w16_ref[...], preferred_element_type=F32)
    if has_res:
        acc = r_ref[...] + acc
    o_ref[...] = acc.astype(o_ref.dtype)


def _mm_tiles(m, n_cols, out_dtype, has_res):
    if not has_res and out_dtype == BF16 and n_cols % MM_TN_WIDE == 0 and m % MM_TM_WIDE == 0:
        return MM_TM_WIDE, MM_TN_WIDE
    return min(MM_TM, m), MM_TN


def _matmul(a, w, n_cols, out_dtype, res=None, tn=None, col_start=0, transposed=False,
            name="matmul"):
    m, k = a.shape
    if tn is None:
        tm, tn = _mm_tiles(m, n_cols, out_dtype, res is not None)
    else:
        tm = min(MM_TM, m)
    grid = (n_cols // tn, m // tm)
    if transposed:
        assert col_start % SUBLANES == 0
        w_spec = pl.BlockSpec((pl.Element(tn), pl.Element(k)),
                              lambda j, i: (pl.multiple_of(col_start + j * tn, SUBLANES), 0))
    else:
        base_blk, rem = divmod(col_start, tn)
        assert rem == 0
        w_spec = pl.BlockSpec((k, tn), lambda j, i: (0, base_blk + j))
    in_specs = [pl.BlockSpec((tm, k), lambda j, i: (i, 0)), w_spec]
    args = [a, w]
    if res is not None:
        in_specs.append(pl.BlockSpec((tm, tn), lambda j, i: (i, j)))
        args.append(res)
    return pl.pallas_call(
        functools.partial(_mm_kernel, has_res=res is not None, transposed=transposed),
        grid=grid,
        in_specs=in_specs,
        out_specs=pl.BlockSpec((tm, tn), lambda j, i: (i, j)),
        out_shape=jax.ShapeDtypeStruct((m, n_cols), out_dtype),
        scratch_shapes=[pltpu.VMEM((k, tn), BF16)],
        compiler_params=_cparams(("arbitrary", "arbitrary")),
        name=name,
    )(*args)


def _mlstm_kernel(bg_ref, q_ref, k_ref, v_ref, o_ref, gt_ref, cwq_ref, cwk_ref, gm_ref,
                  out_ref, qpad, kpad, bc_s, li_s, *, seq):
    head0 = pl.program_id(1) * ML_HPS
    n_chunks = seq // ML_CHUNK
    L = ML_CHUNK

    qpad[0:8, :] = jnp.zeros((8, ML_HPS * ML_QK), F32)
    kpad[0:8, :] = jnp.zeros((8, ML_HPS * ML_QK), F32)
    qpad[8:, :] = q_ref[...].astype(F32)
    kpad[8:, :] = k_ref[...].astype(F32)

    lane = lax.broadcasted_iota(I32, (n_chunks, L), 1)
    for hh in range(ML_HPS):
        li_s[hh] = gt_ref[0, hh] + bg_ref[head0 + hh]
        bc = _neg_softplus(-(gt_ref[1, hh] + bg_ref[ML_HEADS + head0 + hh]))
        for s in (1, 2, 4, 8, 16, 32, 64):
            bc = bc + jnp.where(lane >= s, pltpu.roll(bc, s, axis=1), 0.0)
        bc_s[hh] = bc

    rows = lax.broadcasted_iota(I32, (L, L), 0)
    cols = lax.broadcasted_iota(I32, (L, L), 1)
    eye = rows == cols
    causal = cols <= rows
    k_scale = 1.0 / math.sqrt(ML_QK)

    def to_col(row):
        return jnp.sum(jnp.where(eye, row, 0.0), axis=1, keepdims=True)

    def conv_silu(win, cw):
        y = (cw[0:1, :] * win[5:5 + L] + cw[1:2, :] * win[6:6 + L]
             + cw[2:3, :] * win[7:7 + L] + cw[3:4, :] * win[8:8 + L])
        return y * _sigmoid(y)

    def head_chunk(hh, c, r0, carry):
        c_st, n_st, m_st = carry
        qk_cols = pl.ds(hh * ML_QK, ML_QK)
        v_cols = pl.ds(hh * ML_V, ML_V)
        qb = conv_silu(qpad[pl.ds(r0, L + 8), qk_cols], cwq_ref[:, qk_cols])
        kb = conv_silu(kpad[pl.ds(r0, L + 8), qk_cols], cwk_ref[:, qk_cols]) * k_scale
        vb = v_ref[pl.ds(r0, L), v_cols]
        bc_row = bc_s[hh, pl.ds(c, 1), :]
        li_row = li_s[hh, pl.ds(c, 1), :]
        bc_col = to_col(bc_row)

        d = jnp.where(causal, bc_col - bc_row + li_row, -jnp.inf)
        inter = bc_col + m_st
        m_t = jnp.maximum(inter, jnp.max(d, axis=1, keepdims=True))
        w_intra = jnp.exp(d - m_t)
        w_inter = jnp.exp(inter - m_t)

        qb16 = qb.astype(BF16)
        kb16 = kb.astype(BF16)
        sc = lax.dot_general(qb16, kb16, (((1,), (1,)), ((), ())),
                             preferred_element_type=F32) * w_intra
        num = (w_inter * jnp.dot(qb16, c_st.astype(BF16), preferred_element_type=F32)
               + jnp.dot(sc.astype(BF16), vb, preferred_element_type=F32))
        den = (w_inter * jnp.sum(qb * n_st, axis=1, keepdims=True)
               + jnp.sum(sc, axis=1, keepdims=True))
        hval = num / jnp.maximum(jnp.abs(den), jnp.exp(-m_t))

        hn = hval * lax.rsqrt(jnp.mean(hval * hval, axis=1, keepdims=True) + EPS)
        og = _sigmoid(o_ref[pl.ds(r0, L), v_cols].astype(F32))
        out_ref[pl.ds(r0, L), v_cols] = (og * (hn * gm_ref[:, v_cols])).astype(out_ref.dtype)

        g = bc_row[:, L - 1:L]
        ds_row = g - bc_row + li_row
        m_new = jnp.maximum(g + m_st, jnp.max(ds_row, axis=1, keepdims=True))
        w_s = jnp.exp(ds_row - m_new)
        decay = jnp.exp(g + m_st - m_new)
        kw = kb * to_col(w_s)
        c_new = decay * c_st + jnp.dot(kw.T.astype(BF16), vb, preferred_element_type=F32)
        n_new = decay * n_st + jnp.sum(kw, axis=0, keepdims=True)
        return c_new, n_new, m_new

    def chunk(c, carry):
        r0 = pl.multiple_of(c * L, L)
        return tuple(head_chunk(hh, c, r0, carry[hh]) for hh in range(ML_HPS))

    init = (jnp.zeros((ML_QK, ML_V), F32), jnp.zeros((1, ML_QK), F32),
            jnp.full((1, 1), -1e30, F32))
    lax.fori_loop(0, n_chunks, chunk, (init,) * ML_HPS)


def _mlstm(seg_a, gates_t, conv_qk, b_gates, g_mlstm):
    b, s, _ = seg_a.shape
    h = ML_HEADS
    hp = ML_HPS
    n_chunks = s // ML_CHUNK
    groups = h // hp
    qk_w, v_w = hp * ML_QK, hp * ML_V
    return pl.pallas_call(
        functools.partial(_mlstm_kernel, seq=s),
        grid=(b, groups),
        in_specs=[
            pl.BlockSpec(memory_space=pltpu.SMEM),
            pl.BlockSpec((None, s, qk_w), lambda bi, gi: (bi, 0, gi)),
            pl.BlockSpec((None, s, qk_w), lambda bi, gi: (bi, 0, groups + gi)),
            pl.BlockSpec((None, s, v_w), lambda bi, gi: (bi, 0, groups + gi)),
            pl.BlockSpec((None, s, v_w), lambda bi, gi: (bi, 0, 2 * groups + gi)),
            pl.BlockSpec((None, 2, hp, n_chunks, ML_CHUNK), lambda bi, gi: (bi, 0, gi, 0, 0)),
            pl.BlockSpec((CONV_W, qk_w), lambda bi, gi: (0, gi)),
            pl.BlockSpec((CONV_W, qk_w), lambda bi, gi: (0, groups + gi)),
            pl.BlockSpec((1, v_w), lambda bi, gi: (0, gi)),
        ],
        out_specs=pl.BlockSpec((None, s, v_w), lambda bi, gi: (bi, 0, gi)),
        out_shape=jax.ShapeDtypeStruct((b, s, h * ML_V), BF16),
        scratch_shapes=[pltpu.VMEM((s + 8, qk_w), F32), pltpu.VMEM((s + 8, qk_w), F32),
                        pltpu.VMEM((hp, n_chunks, ML_CHUNK), F32),
                        pltpu.VMEM((hp, n_chunks, ML_CHUNK), F32)],
        compiler_params=_cparams(("arbitrary", "arbitrary")),
        name="mlstm",
    )(b_gates.astype(F32), seg_a, seg_a, seg_a, seg_a, gates_t, conv_qk.astype(F32),
      conv_qk.astype(F32), g_mlstm.reshape(1, -1).astype(F32))


def _sb_kernel(q_ref, k_ref, v_ref, o_ref, acc_s, r_s, *, seq):
    L = SB_BLOCK
    tq, kb = SB_TQ, SB_KB
    scale = 1.0 / math.sqrt(SB_HD)
    rows = lax.broadcasted_iota(I32, (tq, kb), 0)
    cols = lax.broadcasted_iota(I32, (tq, kb), 1)
    ur = lax.broadcasted_iota(I32, (L, 2 * L), 0)
    uc = lax.broadcasted_iota(I32, (L, 2 * L), 1)
    u = jnp.where((uc >= L) | (ur > uc), 1.0, 0.0).astype(BF16)

    def step(qi, c0, mask, row_lo=0):
        kj = k_ref[pl.ds(c0, kb), :]
        vj = v_ref[pl.ds(c0, kb), :]
        tile_rows = pl.ds(row_lo, tq - row_lo)
        z = lax.dot_general(qi[row_lo:], kj, (((1,), (1,)), ((), ())),
                            preferred_element_type=F32) * scale
        l1m = -(jnp.maximum(z, 0.0) + jnp.log(1.0 + jnp.exp(-jnp.abs(z))))
        lm = l1m if mask is None else jnp.where(mask, l1m, 0.0)
        lm16 = lm.astype(BF16)
        cs_far = jnp.dot(lm16[:, :L], u, preferred_element_type=F32)
        cs_near = jnp.dot(lm16[:, L:], u, preferred_element_type=F32)
        r0 = r_s[tile_rows, :]
        r1 = r0 + cs_near[:, L:]
        rest = jnp.concatenate([cs_far[:, :L] + r1, cs_near[:, :L] + r0], axis=1)
        a = jnp.exp((z + l1m) + rest)
        if mask is not None:
            a = jnp.where(mask, a, 0.0)
        acc_s[tile_rows, :] += jnp.dot(a.astype(BF16), vj, preferred_element_type=F32)
        r_s[tile_rows, :] = r1 + cs_far[:, L:]

    def qtile(t, _):
        q0 = pl.multiple_of(t * tq, tq)
        qi = q_ref[pl.ds(q0, tq), :]
        acc_s[...] = jnp.zeros_like(acc_s)
        r_s[...] = jnp.zeros_like(r_s)
        for off in range(tq - kb, -1, -kb):
            step(qi, pl.multiple_of(q0 + off, kb), (cols + off < rows)[off:], row_lo=off)

        def earlier(jj, _):
            c0 = q0 - (jj + 1) * (2 * kb)
            step(qi, pl.multiple_of(c0 + kb, kb), None)
            step(qi, pl.multiple_of(c0, kb), None)
            return 0

        lax.fori_loop(0, t * (tq // (2 * kb)), earlier, 0)
        o_ref[pl.ds(q0, tq), :] = acc_s[...].astype(o_ref.dtype)
        return 0

    lax.fori_loop(0, seq // tq, qtile, 0)


def _stick_breaking(seg_b):
    b, s, _ = seg_b.shape
    h = SB_HEADS
    assert SB_KB == 2 * SB_BLOCK and SB_TQ % (2 * SB_KB) == 0 and s % SB_TQ == 0
    return pl.pallas_call(
        functools.partial(_sb_kernel, seq=s),
        grid=(b, h),
        in_specs=[pl.BlockSpec((None, s, SB_HD), lambda bi, hi: (bi, 0, hi)),
                  pl.BlockSpec((None, s, SB_HD), lambda bi, hi: (bi, 0, h + hi)),
                  pl.BlockSpec((None, s, SB_HD), lambda bi, hi: (bi, 0, 2 * h + hi))],
        out_specs=pl.BlockSpec((None, s, SB_HD), lambda bi, hi: (bi, 0, hi)),
        out_shape=jax.ShapeDtypeStruct((b, s, h * SB_HD), BF16),
        scratch_shapes=[pltpu.VMEM((SB_TQ, SB_HD), F32), pltpu.VMEM((SB_TQ, SB_BLOCK), F32)],
        compiler_params=_cparams(("arbitrary", "arbitrary")),
        name="stick_breaking",
    )(seg_b, seg_b, seg_b)


def _merge_kernel(hm_ref, hs_ref, wa_ref, wb_ref, ga_ref, gb_ref, o_ref, wa16, wb16):
    @pl.when(pl.program_id(1) == 0)
    def _():
        wa16[...] = wa_ref[...].astype(BF16)
        wb16[...] = wb_ref[...].astype(BF16)

    ya = jnp.dot(hm_ref[...], wa16[...], preferred_element_type=F32)
    yb = jnp.dot(hs_ref[...], wb16[...], preferred_element_type=F32)
    y = _sigmoid(ga_ref[...].astype(F32)) * ya + _sigmoid(gb_ref[...].astype(F32)) * yb
    o_ref[...] = y.astype(o_ref.dtype)


def _merge(hm, hs, w_a, w_b, seg_b, gate_a_col, gate_b_col, d_model):
    m, ka = hm.shape
    kb = hs.shape[1]
    tm, tn = MM_TM, MM_TN
    ga_blk = gate_a_col // tn
    gb_blk = gate_b_col // tn
    return pl.pallas_call(
        _merge_kernel,
        grid=(d_model // tn, m // tm),
        in_specs=[pl.BlockSpec((tm, ka), lambda j, i: (i, 0)),
                  pl.BlockSpec((tm, kb), lambda j, i: (i, 0)),
                  pl.BlockSpec((ka, tn), lambda j, i: (0, j)),
                  pl.BlockSpec((kb, tn), lambda j, i: (0, j)),
                  pl.BlockSpec((tm, tn), lambda j, i: (i, ga_blk + j)),
                  pl.BlockSpec((tm, tn), lambda j, i: (i, gb_blk + j))],
        out_specs=pl.BlockSpec((tm, tn), lambda j, i: (i, j)),
        out_shape=jax.ShapeDtypeStruct((m, d_model), BF16),
        scratch_shapes=[pltpu.VMEM((ka, tn), BF16), pltpu.VMEM((kb, tn), BF16)],
        compiler_params=_cparams(("arbitrary", "arbitrary")),
        name="merge",
    )(hm, hs, w_a, w_b, seg_b, seg_b)


def _xattn_kernel(q_ref, k_ref, v_ref, o_ref, *, scale):
    s = lax.dot_general(q_ref[...], k_ref[...], (((1,), (1,)), ((), ())),
                        preferred_element_type=F32) * scale
    p = jnp.exp(s - jnp.max(s, axis=1, keepdims=True))
    p = p / jnp.sum(p, axis=1, keepdims=True)
    o_ref[...] = jnp.dot(p.astype(BF16), v_ref[...], preferred_element_type=F32).astype(o_ref.dtype)


def _xattn(q, kv, tq=1024):
    b, s, d = q.shape
    n_mem = kv.shape[1]
    hd = d // XA_HEADS
    return pl.pallas_call(
        functools.partial(_xattn_kernel, scale=1.0 / math.sqrt(hd)),
        grid=(b, XA_HEADS, s // tq),
        in_specs=[pl.BlockSpec((None, tq, hd), lambda bi, hi, i: (bi, i, hi)),
                  pl.BlockSpec((None, n_mem, hd), lambda bi, hi, i: (bi, 0, hi)),
                  pl.BlockSpec((None, n_mem, hd), lambda bi, hi, i: (bi, 0, XA_HEADS + hi))],
        out_specs=pl.BlockSpec((None, tq, hd), lambda bi, hi, i: (bi, i, hi)),
        out_shape=jax.ShapeDtypeStruct((b, s, d), BF16),
        compiler_params=_cparams(("arbitrary", "arbitrary", "arbitrary")),
        name="xattn",
    )(q, kv, kv)


def _router_kernel(h_ref, g_ref, wr_ref, br_ref, hn_ref, id_ref, wt_ref, wh_ref, wl_ref):
    @pl.when(pl.program_id(0) == 0)
    def _():
        w = wr_ref[...]
        wh = w.astype(BF16)
        wh_ref[...] = wh
        wl_ref[...] = (w - wh.astype(F32)).astype(BF16)

    x = h_ref[...]
    hn = (x * lax.rsqrt(jnp.mean(x * x, axis=-1, keepdims=True) + EPS)) * g_ref[...]
    tk = hn.shape[1] // MOE_KS
    for c in range(MOE_KS):
        hn_ref[:, c * (tk // 2):(c + 1) * (tk // 2)] = _pack_bf16_pair(
            hn[:, c * tk:c * tk + tk // 2], hn[:, c * tk + tk // 2:(c + 1) * tk])
    xh = hn.astype(BF16)
    xl = (hn - xh.astype(F32)).astype(BF16)
    logits = (jnp.dot(xh, wh_ref[...], preferred_element_type=F32)
              + (jnp.dot(xh, wl_ref[...], preferred_element_type=F32)
                 + jnp.dot(xl, wh_ref[...], preferred_element_type=F32))) + br_ref[...]

    tm = logits.shape[0]
    lane = lax.broadcasted_iota(I32, (tm, LANES), 1)
    lane_f = lane.astype(F32)
    ninf = -jnp.inf

    def first_lane_of(v, vmax):
        return jnp.min(jnp.where(v == vmax, lane_f, float(LANES)), axis=1, keepdims=True)

    gl = jnp.where(lane < N_GROUPS, logits, ninf)
    gmax = jnp.max(gl, axis=1, keepdims=True)
    g_w = 1.0 / jnp.sum(jnp.exp(gl - gmax), axis=1, keepdims=True)
    g_idx = first_lane_of(gl, gmax)
    lo = float(N_GROUPS) + g_idx * float(EXPERTS_PER_GROUP)
    in_group = (lane_f >= lo) & (lane_f < lo + float(EXPERTS_PER_GROUP))
    el = jnp.where(in_group, logits, ninf)
    m1 = jnp.max(el, axis=1, keepdims=True)
    i1 = first_lane_of(el, m1)
    el2 = jnp.where(lane_f == i1, ninf, el)
    m2 = jnp.max(el2, axis=1, keepdims=True)
    i2 = first_lane_of(el2, m2)
    denom = jnp.sum(jnp.exp(el - m1), axis=1, keepdims=True)
    p1 = 1.0 / denom
    p2 = jnp.exp(m2 - m1) / denom
    psum = p1 + p2
    w1 = g_w * (p1 / psum)
    w2 = g_w * (p2 / psum)
    ids = jnp.where(lane == 0, i1 - float(N_GROUPS), jnp.where(lane == 1, i2 - float(N_GROUPS), 0.0))
    id_ref[...] = ids.astype(I32)
    wt_ref[...] = jnp.where(lane == 0, w1, jnp.where(lane == 1, w2, 0.0))


def _router(h, g, w_r, b_r):
    t, d = h.shape
    tm = ROUTE_TM
    return pl.pallas_call(
        _router_kernel,
        grid=(t // tm,),
        in_specs=[pl.BlockSpec((tm, d), lambda i: (i, 0)),
                  pl.BlockSpec((1, d), lambda i: (0, 0)),
                  pl.BlockSpec((d, LANES), lambda i: (0, 0)),
                  pl.BlockSpec((1, LANES), lambda i: (0, 0))],
        out_specs=[pl.BlockSpec((tm, d // 2), lambda i: (i, 0)),
                   pl.BlockSpec((tm, LANES), lambda i: (i, 0)),
                   pl.BlockSpec((tm, LANES), lambda i: (i, 0))],
        out_shape=[jax.ShapeDtypeStruct((t, d // 2), U32),
                   jax.ShapeDtypeStruct((t, LANES), I32),
                   jax.ShapeDtypeStruct((t, LANES), F32)],
        scratch_shapes=[pltpu.VMEM((d, LANES), BF16), pltpu.VMEM((d, LANES), BF16)],
        compiler_params=_cparams(("arbitrary",)),
        name="router",
    )(h, g.reshape(1, d).astype(F32), w_r, b_r)


def _lane_cumsum(x):
    lane = lax.broadcasted_iota(I32, x.shape, 1)
    s = 1
    while s < LANES:
        x = x + jnp.where(lane >= s, pltpu.roll(x, s, axis=1), 0.0)
        s *= 2
    return x


def _meta_kernel(ids_ref, dest_ref, blk_ref, misc_ref, rank_s, *, n_tok, n_blocks):
    tb = META_TB
    lane = lax.broadcasted_iota(I32, (tb, LANES), 1)
    lower = jnp.where(lax.broadcasted_iota(I32, (tb, tb), 0) > lax.broadcasted_iota(I32, (tb, tb), 1),
                      1.0, 0.0).astype(BF16)

    def onehots(b):
        ids = ids_ref[pl.ds(pl.multiple_of(b * tb, tb), tb), :]
        return lane == ids[:, 0:1], lane == ids[:, 1:2]

    def lanes01(v0, v1):
        return jnp.where(lane == 0, v0, jnp.where(lane == 1, v1, 0.0))

    def count(b, carry):
        o1, o2 = onehots(b)
        cnt = jnp.where(o1, 1.0, 0.0) + jnp.where(o2, 1.0, 0.0)
        before = jnp.dot(lower, cnt.astype(BF16), preferred_element_type=F32) + carry
        r1 = jnp.sum(jnp.where(o1, before, 0.0), axis=1, keepdims=True)
        r2 = jnp.sum(jnp.where(o2, before, 0.0), axis=1, keepdims=True)
        rank_s[pl.ds(pl.multiple_of(b * tb, tb), tb), :] = lanes01(r1, r2)
        return carry + jnp.sum(cnt, axis=0, keepdims=True)

    counts = lax.fori_loop(0, n_tok // tb, count, jnp.zeros((1, LANES), F32))

    cnt8 = jnp.broadcast_to(counts, (8, LANES))
    whole = jnp.floor(cnt8 * (1.0 / MOE_ALIGN)) * MOE_ALIGN
    seg = jnp.where(cnt8 > whole, whole + MOE_ALIGN, whole)
    seg_start = _lane_cumsum(seg) - seg
    nblk = jnp.floor((cnt8 + (MOE_SBLK - 1)) * (1.0 / MOE_SBLK))
    blk_end = _lane_cumsum(nblk)
    blk_start = blk_end - nblk
    row_start = seg_start[0:1, :]

    def place(b, _):
        o1, o2 = onehots(b)
        s1 = jnp.sum(jnp.where(o1, row_start, 0.0), axis=1, keepdims=True)
        s2 = jnp.sum(jnp.where(o2, row_start, 0.0), axis=1, keepdims=True)
        sl = pl.ds(pl.multiple_of(b * tb, tb), tb)
        dest_ref[sl, :] = (rank_s[sl, :] + lanes01(s1, s2)).astype(I32)
        return 0

    lax.fori_loop(0, n_tok // tb, place, 0)

    step = lax.broadcasted_iota(I32, (n_blocks, LANES), 0).astype(F32)
    elane = lax.broadcasted_iota(I32, (n_blocks, LANES), 1)
    mine = (elane < N_EXPERTS) & (blk_start[0:1, :] <= step) & (step < blk_end[0:1, :])
    done_rows = (step - blk_start[0:1, :]) * float(MOE_SBLK)

    def pick(v):
        return jnp.sum(jnp.where(mine, v, 0.0), axis=1, keepdims=True)

    s_exp = pick(elane.astype(F32))
    s_row = pick(row_start + done_rows)
    s_val = pick(jnp.minimum(counts - done_rows, float(MOE_SBLK)))
    blk_ref[...] = jnp.where(elane == 0, s_exp, jnp.where(elane == 1, s_row, jnp.where(
        elane == 2, s_val, 0.0))).astype(I32)

    lane8 = lax.broadcasted_iota(I32, (8, LANES), 1)
    sub8 = lax.broadcasted_iota(I32, (8, LANES), 0)
    pad_row = jnp.where((cnt8 > whole) & (lane8 < N_EXPERTS), seg_start + whole, -1.0)
    n_used = jnp.broadcast_to(blk_end[:, N_EXPERTS - 1:N_EXPERTS], (8, LANES))
    misc_ref[...] = jnp.where(sub8 == 0, pad_row, jnp.where(sub8 == 1, n_used, 0.0)).astype(I32)


def _moe_meta(ids, n_blocks):
    t = ids.shape[0]
    assert MOE_ALIGN & (MOE_ALIGN - 1) == 0 and MOE_SBLK & (MOE_SBLK - 1) == 0
    return pl.pallas_call(
        functools.partial(_meta_kernel, n_tok=t, n_blocks=n_blocks),
        out_shape=[jax.ShapeDtypeStruct((t, LANES), I32),
                   jax.ShapeDtypeStruct((n_blocks, LANES), I32),
                   jax.ShapeDtypeStruct((8, LANES), I32)],
        scratch_shapes=[pltpu.VMEM((t, LANES), F32)],
        compiler_params=pltpu.CompilerParams(vmem_limit_bytes=VMEM_LIMIT),
        name="moe_meta",
    )(ids)


def _dispatch_kernel(dest_ref, misc_ref, x_ref, xs_ref, zbuf, sem, *, n_assign):
    tb = x_ref.shape[0]
    zrows = zbuf.shape[0]

    @pl.when(pl.program_id(0) == 0)
    def _():
        zbuf[...] = jnp.zeros_like(zbuf)

        def tail_copy(j):
            return pltpu.make_async_copy(zbuf, xs_ref.at[pl.ds(n_assign + j * zrows, zrows)], sem)

        n_tail = (xs_ref.shape[0] - n_assign) // zrows
        for j in range(n_tail):
            tail_copy(j).start()
        for j in range(n_tail):
            tail_copy(j).wait()

        def pad_copy(row):
            return pltpu.make_async_copy(zbuf.at[pl.ds(0, MOE_ALIGN)],
                                         xs_ref.at[pl.ds(row, MOE_ALIGN)], sem)

        def pads(fn):
            def body(e, _):
                row = misc_ref[0, e]

                @pl.when(row >= 0)
                def _():
                    fn(pad_copy(pl.multiple_of(row, MOE_ALIGN)))
                return 0
            lax.fori_loop(0, N_EXPERTS, body, 0)

        pads(lambda c: c.start())
        pads(lambda c: c.wait())

    def row_copy(r, d):
        return pltpu.make_async_copy(x_ref.at[pl.ds(r, 1)], xs_ref.at[pl.ds(d, 1)], sem)

    def scatters(fn):
        def body(g, _):
            r0 = pl.multiple_of(g * SUBLANES, SUBLANES)
            for q in range(SUBLANES):
                for j in range(2):
                    fn(row_copy(r0 + q, dest_ref[0, 0, 2 * (r0 + q) + j]))
            return 0
        lax.fori_loop(0, tb // SUBLANES, body, 0)

    scatters(lambda c: c.start())
    scatters(lambda c: c.wait())


def _dispatch(dest_blocks, misc, hn, xs_rows):
    t, d = hn.shape
    tb = DISPATCH_TB
    n_assign = 2 * t
    assert (xs_rows - n_assign) % MOE_SUB == 0
    return pl.pallas_call(
        functools.partial(_dispatch_kernel, n_assign=n_assign),
        grid=(t // tb,),
        in_specs=[pl.BlockSpec((1, 1, 2 * tb), lambda i: (i, 0, 0), memory_space=pltpu.SMEM),
                  pl.BlockSpec(memory_space=pltpu.SMEM),
                  pl.BlockSpec((tb, d), lambda i: (i, 0))],
        out_specs=pl.BlockSpec(memory_space=pl.ANY),
        out_shape=jax.ShapeDtypeStruct((xs_rows, d), hn.dtype),
        scratch_shapes=[pltpu.VMEM((MOE_SUB, d), hn.dtype), pltpu.SemaphoreType.DMA(())],
        compiler_params=_cparams(("arbitrary",)),
        name="moe_dispatch",
    )(dest_blocks, misc, hn)


def _expert_kernel(be_ref, rs_ref, nv_ref, nu_ref, xs_ref, wg_ref, wu_ref, wd_ref, ys_ref,
                   gacc, uacc, wg16, wu16, wd16, ybuf, sem, *, n_assign):
    del be_ref
    i = pl.program_id(0)
    k = pl.program_id(1)
    n_used = nu_ref[0]
    sub_shift = MOE_SUB.bit_length() - 1

    def out_copies(j, fn):
        row0 = rs_ref[j]
        rows = ((nv_ref[j] + (MOE_ALIGN - 1)) // MOE_ALIGN) * MOE_ALIGN
        n_full = rows >> sub_shift
        rem = rows & (MOE_SUB - 1)

        def piece_copy(off, size):
            src_off = off if isinstance(off, int) else pl.multiple_of(off, MOE_ALIGN)
            return pltpu.make_async_copy(
                ybuf.at[pl.ds(src_off, size)],
                ys_ref.at[pl.ds(pl.multiple_of(row0 + off, MOE_ALIGN), size)], sem)

        for s in range(MOE_SBLK // MOE_SUB):
            @pl.when(s < n_full)
            def _():
                fn(piece_copy(s * MOE_SUB, MOE_SUB))
        piece = MOE_SUB // 2
        while piece >= MOE_ALIGN:
            off = n_full * MOE_SUB + (rem & (MOE_SUB - 2 * piece))

            @pl.when((rem & piece) != 0)
            def _():
                fn(piece_copy(off, piece))
            piece //= 2

    @pl.when((i == 0) & (k == 0))
    def _():
        ybuf[...] = jnp.zeros_like(ybuf)

        def tail_copy(j):
            return pltpu.make_async_copy(ybuf, ys_ref.at[pl.ds(n_assign + j * MOE_SBLK, MOE_SBLK)], sem)

        n_tail = (ys_ref.shape[0] - n_assign) // MOE_SBLK
        for j in range(n_tail):
            tail_copy(j).start()
        for j in range(n_tail):
            tail_copy(j).wait()

    @pl.when(i < n_used)
    def _():
        n_sub = (nv_ref[i] + (MOE_SUB - 1)) >> sub_shift
        wg16[...] = wg_ref[...].astype(BF16)
        wu16[...] = wu_ref[...].astype(BF16)

        @pl.when(k == 0)
        def _():
            gacc[...] = jnp.zeros_like(gacc)
            uacc[...] = jnp.zeros_like(uacc)

        def up(r, _):
            rows = pl.ds(pl.multiple_of(r * MOE_SUB, MOE_SUB), MOE_SUB)
            x = jnp.concatenate(_unpack_bf16_pair(xs_ref[rows, :]), axis=1).astype(BF16)
            gacc[rows, :] += jnp.dot(x, wg16[...], preferred_element_type=F32)
            uacc[rows, :] += jnp.dot(x, wu16[...], preferred_element_type=F32)
            return 0

        lax.fori_loop(0, n_sub, up, 0)

        @pl.when(k == MOE_KS - 1)
        def _():
            wd16[...] = wd_ref[...].astype(BF16)

            @pl.when(i > 0)
            def _():
                out_copies(i - 1, lambda c: c.wait())

            def down(r, _):
                rows = pl.ds(pl.multiple_of(r * MOE_SUB, MOE_SUB), MOE_SUB)
                g = gacc[rows, :]
                hb = (g * _sigmoid(g)) * uacc[rows, :]
                y = jnp.dot(hb.astype(BF16), wd16[...], preferred_element_type=F32)
                half = y.shape[1] // 2
                ybuf[rows, :] = _pack_bf16_pair(y[:, :half], y[:, half:])
                return 0

            lax.fori_loop(0, n_sub, down, 0)
            out_copies(i, lambda c: c.start())

            @pl.when(i == n_used - 1)
            def _():
                out_copies(i, lambda c: c.wait())


def _experts(step_expert, step_row, step_valid, n_used, xs, w_gate, w_up, w_down, n_assign):
    xs_rows = xs.shape[0]
    n_blocks = step_expert.shape[0]
    d, f = w_gate.shape[-2:]
    tk = d // MOE_KS
    assert (xs_rows - n_assign) % MOE_SBLK == 0 and xs.shape[1] == d // 2

    def blk(i, nu):
        return jnp.minimum(i, nu[0] - 1)

    def kk(i, k, nu):
        return jnp.where(i < nu[0], k, MOE_KS - 1)

    grid_spec = pltpu.PrefetchScalarGridSpec(
        num_scalar_prefetch=4,
        grid=(n_blocks, MOE_KS),
        in_specs=[
            pl.BlockSpec((pl.Element(MOE_SBLK), pl.Element(tk // 2)),
                         lambda i, k, be, rs, nv, nu: (pl.multiple_of(rs[blk(i, nu)], MOE_ALIGN),
                                                       kk(i, k, nu) * (tk // 2))),
            pl.BlockSpec((None, tk, f), lambda i, k, be, rs, nv, nu: (be[blk(i, nu)], kk(i, k, nu), 0)),
            pl.BlockSpec((None, tk, f), lambda i, k, be, rs, nv, nu: (be[blk(i, nu)], kk(i, k, nu), 0)),
            pl.BlockSpec((None, f, d), lambda i, k, be, rs, nv, nu: (be[blk(i, nu)], 0, 0)),
        ],
        out_specs=pl.BlockSpec(memory_space=pl.ANY),
        scratch_shapes=[pltpu.VMEM((MOE_SBLK, f), F32), pltpu.VMEM((MOE_SBLK, f), F32),
                        pltpu.VMEM((tk, f), BF16), pltpu.VMEM((tk, f), BF16),
                        pltpu.VMEM((f, d), BF16), pltpu.VMEM((MOE_SBLK, d // 2), U32),
                        pltpu.SemaphoreType.DMA(())],
    )
    return pl.pallas_call(
        functools.partial(_expert_kernel, n_assign=n_assign),
        grid_spec=grid_spec,
        out_shape=jax.ShapeDtypeStruct((xs_rows, d // 2), U32),
        compiler_params=_cparams(("arbitrary", "arbitrary")),
        name="moe_experts",
    )(step_expert, step_row, step_valid, n_used, xs, w_gate, w_up, w_down)


def _combine_kernel(dest_ref, dest_next_ref, w_ref, h_ref, g_ref, ys_ref, o_ref, ybuf, sems, *,
                    final_norm):
    tb = h_ref.shape[0]
    i = pl.program_id(0)
    slot = i % 2

    def gathers(rows_ref, s, fn):
        def body(g, _):
            r0 = pl.multiple_of(g * SUBLANES, SUBLANES)
            for q in range(SUBLANES):
                for j in range(2):
                    fn(pltpu.make_async_copy(ys_ref.at[pl.ds(rows_ref[0, 0, 2 * (r0 + q) + j], 1)],
                                             ybuf.at[s, j, pl.ds(r0 + q, 1)], sems.at[s]))
            return 0
        lax.fori_loop(0, tb // SUBLANES, body, 0)

    @pl.when(i == 0)
    def _():
        gathers(dest_ref, 0, lambda c: c.start())

    @pl.when(i + 1 < pl.num_programs(0))
    def _():
        gathers(dest_next_ref, 1 - slot, lambda c: c.start())

    gathers(dest_ref, slot, lambda c: c.wait())

    w = w_ref[...]
    lo0, hi0 = _unpack_bf16_pair(ybuf[slot, 0])
    lo1, hi1 = _unpack_bf16_pair(ybuf[slot, 1])
    w0, w1 = w[:, 0:1], w[:, 1:2]
    h = h_ref[...] + jnp.concatenate([w0 * lo0 + w1 * lo1, w0 * hi0 + w1 * hi1], axis=1)
    if final_norm:
        h = (h * lax.rsqrt(jnp.mean(h * h, axis=-1, keepdims=True) + EPS)) * g_ref[...]
    o_ref[...] = h


def _combine(dest_blocks, wts, h, g, ys, final_norm):
    t, d = h.shape
    tb = COMBINE_TB
    n_tiles = t // tb
    return pl.pallas_call(
        functools.partial(_combine_kernel, final_norm=final_norm),
        grid=(n_tiles,),
        in_specs=[pl.BlockSpec((1, 1, 2 * tb), lambda i: (i, 0, 0), memory_space=pltpu.SMEM),
                  pl.BlockSpec((1, 1, 2 * tb), lambda i: (jnp.minimum(i + 1, n_tiles - 1), 0, 0),
                               memory_space=pltpu.SMEM),
                  pl.BlockSpec((tb, LANES), lambda i: (i, 0)),
                  pl.BlockSpec((tb, d), lambda i: (i, 0)),
                  pl.BlockSpec((1, d), lambda i: (0, 0)),
                  pl.BlockSpec(memory_space=pl.ANY)],
        out_specs=pl.BlockSpec((tb, d), lambda i: (i, 0)),
        out_shape=jax.ShapeDtypeStruct((t, d), F32),
        scratch_shapes=[pltpu.VMEM((2, 2, tb, d // 2), U32), pltpu.SemaphoreType.DMA((2,))],
        compiler_params=_cparams(("arbitrary",)),
        name="moe_combine",
    )(dest_blocks, dest_blocks, wts, h, g.reshape(1, d).astype(F32), ys)


def kernel(x, mem, norm_mix, w_in, conv_qk, b_gates, g_mlstm, w_proj_a, w_proj_b, w_out,
           norm_xattn, norm_mem, w_q_mem, w_kv_mem, w_o_mem, norm_moe,
           w_router_group, b_router_group, w_router_expert, b_router_expert,
           w_gate, w_up, w_down, norm_final):
    b, s, d = x.shape
    t = b * s
    n_mem = mem.shape[1]
    depth = w_in.shape[0]
    ml_qk_w = ML_HEADS * ML_QK
    ml_v_w = ML_HEADS * ML_V
    sb_w = SB_HEADS * SB_HD
    seg_a_w = 2 * ml_qk_w + 2 * ml_v_w
    n_gate_cols = 2 * ML_HEADS
    seg_b_w = 3 * sb_w + 2 * d
    n_assign = 2 * t
    n_blocks = n_assign // MOE_SBLK + N_EXPERTS
    xs_rows = n_assign + N_EXPERTS * MOE_ALIGN + MOE_SBLK

    h = x.reshape(t, d)
    mem2 = mem.reshape(b * n_mem, d)
    for l in range(depth):
        xn = _rmsnorm(h, norm_mix[l], BF16)
        w_in_t = jnp.swapaxes(w_in[l], 0, 1)
        seg_a = _matmul(xn, w_in_t, seg_a_w, BF16, transposed=True, name="in_proj_a")
        gates = _matmul(xn, w_in_t, LANES, F32, tn=LANES, col_start=seg_a_w, transposed=True,
                        name="in_proj_gates")
        seg_b = _matmul(xn, w_in_t, seg_b_w, BF16, col_start=seg_a_w + n_gate_cols, transposed=True,
                        name="in_proj_b")
        gates_t = gates[:, :n_gate_cols].reshape(b, s, 2, ML_HEADS).transpose(0, 2, 3, 1)
        gates_t = gates_t.reshape(b, 2, ML_HEADS, s // ML_CHUNK, ML_CHUNK)
        hm = _mlstm(seg_a.reshape(b, s, seg_a_w), gates_t, conv_qk[l], b_gates[l], g_mlstm[l])
        hs = _stick_breaking(seg_b.reshape(b, s, seg_b_w))
        y = _merge(hm.reshape(t, ml_v_w), hs.reshape(t, sb_w), w_proj_a[l], w_proj_b[l],
                   seg_b, 3 * sb_w, 3 * sb_w + d, d)
        h = _matmul(y, w_out[l], d, F32, res=h, name="out_proj")
        hn = _rmsnorm(h, norm_xattn[l], BF16)
        q = _matmul(hn, w_q_mem[l], d, BF16, name="xattn_q")
        memn = _rmsnorm(mem2, norm_mem[l], BF16)
        kv = _matmul(memn, w_kv_mem[l], 2 * d, BF16, name="xattn_kv")
        o = _xattn(q.reshape(b, s, d), kv.reshape(b, n_mem, 2 * d))
        h = _matmul(o.reshape(t, d), w_o_mem[l], d, F32, res=h, name="xattn_o")
        w_r = jnp.pad(jnp.concatenate([w_router_group[l], w_router_expert[l]], axis=1),
                      ((0, 0), (0, LANES - N_GROUPS - N_EXPERTS)))
        b_r = jnp.pad(jnp.concatenate([b_router_group[l], b_router_expert[l]]),
                      (0, LANES - N_GROUPS - N_EXPERTS)).reshape(1, LANES).astype(F32)
        hn3, ids, wts = _router(h, norm_moe[l], w_r, b_r)
        dest, steps, misc = _moe_meta(ids, n_blocks)
        xs = _dispatch(dest[:, :2].reshape(t // DISPATCH_TB, 1, 2 * DISPATCH_TB), misc, hn3, xs_rows)
        ys = _experts(steps[:, 0], steps[:, 1], steps[:, 2], misc[1, :1], xs,
                      w_gate[l], w_up[l], w_down[l], n_assign)
        h = _combine(dest[:, :2].reshape(t // COMBINE_TB, 1, 2 * COMBINE_TB), wts, h,
                     norm_final, ys, final_norm=(l == depth - 1))
    return h.reshape(b, s, d)
```

```python
import functools
import math

import jax
import jax.numpy as jnp
from jax import lax
from jax.experimental import pallas as pl
from jax.experimental.pallas import tpu as pltpu

F32 = jnp.float32
BF16 = jnp.bfloat16
I32 = jnp.int32
U32 = jnp.uint32

EPS = 1e-6
ML_HEADS = 8
ML_QK = 128
ML_V = 256
ML_CHUNK = 128
CONV_W = 4
ML_HPS = 2
SB_HEADS = 16
SB_HD = 128
SB_BLOCK = 128
XA_HEADS = 4
N_GROUPS = 8
EXPERTS_PER_GROUP = 8
N_EXPERTS = N_GROUPS * EXPERTS_PER_GROUP
D_EXPERT = 512

LANES = 128
SUBLANES = 8
VMEM_LIMIT = 56 * 1024 * 1024

SB_TQ = 512
SB_KB = 256
MM_TM = 1024
MM_TN = 512
MM_CAST_ROWS = 512
MOE_ALIGN = 8
MOE_SBLK = 512
MOE_SUB = 128
MOE_KS = 2
ROUTE_TM = 512
META_TB = 256
DISPATCH_TB = 512
COMBINE_TB = 256


def _cparams(sem, vmem=VMEM_LIMIT):
    return pltpu.CompilerParams(dimension_semantics=sem, vmem_limit_bytes=vmem)


def _sigmoid(x):
    return 1.0 / (1.0 + jnp.exp(-x))


def _pack_bf16_pair(lo, hi):
    lo_bits = lax.bitcast_convert_type(lo.astype(BF16).astype(F32), U32)
    hi_bits = lax.bitcast_convert_type(hi.astype(BF16).astype(F32), U32)
    return (lo_bits >> 16) | (hi_bits & jnp.uint32(0xFFFF0000))


def _unpack_bf16_pair(p):
    lo = lax.bitcast_convert_type(p << 16, F32)
    hi = lax.bitcast_convert_type(p & jnp.uint32(0xFFFF0000), F32)
    return lo, hi


def _neg_softplus(x):
    return -(jnp.maximum(x, 0.0) + jnp.log1p(jnp.exp(-jnp.abs(x))))


def _rmsnorm_kernel(x_ref, g_ref, o_ref):
    x = x_ref[...].astype(F32)
    ms = jnp.mean(x * x, axis=-1, keepdims=True)
    o_ref[...] = ((x * lax.rsqrt(ms + EPS)) * g_ref[...]).astype(o_ref.dtype)


def _rmsnorm(x, g, out_dtype, tm=512):
    m, d = x.shape
    return pl.pallas_call(
        _rmsnorm_kernel,
        grid=(m // tm,),
        in_specs=[pl.BlockSpec((tm, d), lambda i: (i, 0)),
                  pl.BlockSpec((1, d), lambda i: (0, 0))],
        out_specs=pl.BlockSpec((tm, d), lambda i: (i, 0)),
        out_shape=jax.ShapeDtypeStruct((m, d), out_dtype),
        compiler_params=_cparams(("arbitrary",)),
        name="rmsnorm",
    )(x, g.reshape(1, d).astype(F32))


def _mm_kernel(*refs, has_res, transposed):
    a_ref, w_ref = refs[0], refs[1]
    r_ref = refs[2] if has_res else None
    o_ref, w16_ref = refs[-2], refs[-1]
    k, tn = w16_ref.shape

    @pl.when(pl.program_id(1) == 0)
    def _():
        if transposed:
            for k0 in range(0, k, MM_CAST_ROWS):
                ks = pl.ds(k0, MM_CAST_ROWS)
                w16_ref[ks, :] = w_ref[:, ks].T.astype(BF16)
        else:
            w16_ref[...] = w_ref[...].astype(BF16)

    acc = jnp.dot(a_ref[...], w16_ref[...], preferred_element_type=F32)
    if has_res:
        acc = r_ref[...] + acc
    o_ref[...] = acc.astype(o_ref.dtype)


def _matmul(a, w, n_cols, out_dtype, res=None, tn=MM_TN, col_start=0, transposed=False,
            name="matmul"):
    m, k = a.shape
    tm = min(MM_TM, m)
    grid = (n_cols // tn, m // tm)
    if transposed:
        assert col_start % SUBLANES == 0
        w_spec = pl.BlockSpec((pl.Element(tn), pl.Element(k)),
                              lambda j, i: (pl.multiple_of(col_start + j * tn, SUBLANES), 0))
    else:
        base_blk, rem = divmod(col_start, tn)
        assert rem == 0
        w_spec = pl.BlockSpec((k, tn), lambda j, i: (0, base_blk + j))
    in_specs = [pl.BlockSpec((tm, k), lambda j, i: (i, 0)), w_spec]
    args = [a, w]
    if res is not None:
        in_specs.append(pl.BlockSpec((tm, tn), lambda j, i: (i, j)))
        args.append(res)
    return pl.pallas_call(
        functools.partial(_mm_kernel, has_res=res is not None, transposed=transposed),
        grid=grid,
        in_specs=in_specs,
        out_specs=pl.BlockSpec((tm, tn), lambda j, i: (i, j)),
        out_shape=jax.ShapeDtypeStruct((m, n_cols), out_dtype),
        scratch_shapes=[pltpu.VMEM((k, tn), BF16)],
        compiler_params=_cparams(("arbitrary", "arbitrary")),
        name=name,
    )(*args)


def _mlstm_kernel(bg_ref, q_ref, k_ref, v_ref, o_ref, gt_ref, cwq_ref, cwk_ref, gm_ref,
                  out_ref, qpad, kpad, bc_s, li_s, *, seq):
    head0 = pl.program_id(1) * ML_HPS
    n_chunks = seq // ML_CHUNK
    L = ML_CHUNK

    qpad[0:8, :] = jnp.zeros((8, ML_HPS * ML_QK), F32)
    kpad[0:8, :] = jnp.zeros((8, ML_HPS * ML_QK), F32)
    qpad[8:, :] = q_ref[...].astype(F32)
    kpad[8:, :] = k_ref[...].astype(F32)

    lane = lax.broadcasted_iota(I32, (n_chunks, L), 1)
    for hh in range(ML_HPS):
        li_s[hh] = gt_ref[0, hh] + bg_ref[head0 + hh]
        bc = _neg_softplus(-(gt_ref[1, hh] + bg_ref[ML_HEADS + head0 + hh]))
        for s in (1, 2, 4, 8, 16, 32, 64):
            bc = bc + jnp.where(lane >= s, pltpu.roll(bc, s, axis=1), 0.0)
        bc_s[hh] = bc

    rows = lax.broadcasted_iota(I32, (L, L), 0)
    cols = lax.broadcasted_iota(I32, (L, L), 1)
    eye = rows == cols
    causal = cols <= rows
    k_scale = 1.0 / math.sqrt(ML_QK)

    def to_col(row):
        return jnp.sum(jnp.where(eye, row, 0.0), axis=1, keepdims=True)

    def conv_silu(win, cw):
        y = (cw[0:1, :] * win[5:5 + L] + cw[1:2, :] * win[6:6 + L]
             + cw[2:3, :] * win[7:7 + L] + cw[3:4, :] * win[8:8 + L])
        return y * _sigmoid(y)

    def head_chunk(hh, c, r0, carry):
        c_st, n_st, m_st = carry
        qk_cols = pl.ds(hh * ML_QK, ML_QK)
        v_cols = pl.ds(hh * ML_V, ML_V)
        qb = conv_silu(qpad[pl.ds(r0, L + 8), qk_cols], cwq_ref[:, qk_cols])
        kb = conv_silu(kpad[pl.ds(r0, L + 8), qk_cols], cwk_ref[:, qk_cols]) * k_scale
        vb = v_ref[pl.ds(r0, L), v_cols]
        bc_row = bc_s[hh, pl.ds(c, 1), :]
        li_row = li_s[hh, pl.ds(c, 1), :]
        bc_col = to_col(bc_row)

        d = jnp.where(causal, bc_col - bc_row + li_row, -jnp.inf)
        inter = bc_col + m_st
        m_t = jnp.maximum(inter, jnp.max(d, axis=1, keepdims=True))
        w_intra = jnp.exp(d - m_t)
        w_inter = jnp.exp(inter - m_t)

        qb16 = qb.astype(BF16)
        kb16 = kb.astype(BF16)
        sc = lax.dot_general(qb16, kb16, (((1,), (1,)), ((), ())),
                             preferred_element_type=F32) * w_intra
        num = (w_inter * jnp.dot(qb16, c_st.astype(BF16), preferred_element_type=F32)
               + jnp.dot(sc.astype(BF16), vb, preferred_element_type=F32))
        den = (w_inter * jnp.sum(qb * n_st, axis=1, keepdims=True)
               + jnp.sum(sc, axis=1, keepdims=True))
        hval = num / jnp.maximum(jnp.abs(den), jnp.exp(-m_t))

        hn = hval * lax.rsqrt(jnp.mean(hval * hval, axis=1, keepdims=True) + EPS)
        og = _sigmoid(o_ref[pl.ds(r0, L), v_cols].astype(F32))
        out_ref[pl.ds(r0, L), v_cols] = (og * (hn * gm_ref[:, v_cols])).astype(out_ref.dtype)

        g = bc_row[:, L - 1:L]
        ds_row = g - bc_row + li_row
        m_new = jnp.maximum(g + m_st, jnp.max(ds_row, axis=1, keepdims=True))
        w_s = jnp.exp(ds_row - m_new)
        decay = jnp.exp(g + m_st - m_new)
        kw = kb * to_col(w_s)
        c_new = decay * c_st + jnp.dot(kw.T.astype(BF16), vb, preferred_element_type=F32)
        n_new = decay * n_st + jnp.sum(kw, axis=0, keepdims=True)
        return c_new, n_new, m_new

    def chunk(c, carry):
        r0 = pl.multiple_of(c * L, L)
        return tuple(head_chunk(hh, c, r0, carry[hh]) for hh in range(ML_HPS))

    init = (jnp.zeros((ML_QK, ML_V), F32), jnp.zeros((1, ML_QK), F32),
            jnp.full((1, 1), -1e30, F32))
    lax.fori_loop(0, n_chunks, chunk, (init,) * ML_HPS)


def _mlstm(seg_a, gates_t, conv_qk, b_gates, g_mlstm):
    b, s, _ = seg_a.shape
    h = ML_HEADS
    hp = ML_HPS
    n_chunks = s // ML_CHUNK
    groups = h // hp
    qk_w, v_w = hp * ML_QK, hp * ML_V
    return pl.pallas_call(
        functools.partial(_mlstm_kernel, seq=s),
        grid=(b, groups),
        in_specs=[
            pl.BlockSpec(memory_space=pltpu.SMEM),
            pl.BlockSpec((None, s, qk_w), lambda bi, gi: (bi, 0, gi)),
            pl.BlockSpec((None, s, qk_w), lambda bi, gi: (bi, 0, groups + gi)),
            pl.BlockSpec((None, s, v_w), lambda bi, gi: (bi, 0, groups + gi)),
            pl.BlockSpec((None, s, v_w), lambda bi, gi: (bi, 0, 2 * groups + gi)),
            pl.BlockSpec((None, 2, hp, n_chunks, ML_CHUNK), lambda bi, gi: (bi, 0, gi, 0, 0)),
            pl.BlockSpec((CONV_W, qk_w), lambda bi, gi: (0, gi)),
            pl.BlockSpec((CONV_W, qk_w), lambda bi, gi: (0, groups + gi)),
            pl.BlockSpec((1, v_w), lambda bi, gi: (0, gi)),
        ],
        out_specs=pl.BlockSpec((None, s, v_w), lambda bi, gi: (bi, 0, gi)),
        out_shape=jax.ShapeDtypeStruct((b, s, h * ML_V), BF16),
        scratch_shapes=[pltpu.VMEM((s + 8, qk_w), F32), pltpu.VMEM((s + 8, qk_w), F32),
                        pltpu.VMEM((hp, n_chunks, ML_CHUNK), F32),
                        pltpu.VMEM((hp, n_chunks, ML_CHUNK), F32)],
        compiler_params=_cparams(("arbitrary", "arbitrary")),
        name="mlstm",
    )(b_gates.astype(F32), seg_a, seg_a, seg_a, seg_a, gates_t, conv_qk.astype(F32),
      conv_qk.astype(F32), g_mlstm.reshape(1, -1).astype(F32))


def _sb_kernel(q_ref, k_ref, v_ref, o_ref, acc_s, r_s, *, seq):
    L = SB_BLOCK
    tq, kb = SB_TQ, SB_KB
    scale = 1.0 / math.sqrt(SB_HD)
    rows = lax.broadcasted_iota(I32, (tq, kb), 0)
    cols = lax.broadcasted_iota(I32, (tq, kb), 1)
    ur = lax.broadcasted_iota(I32, (L, 2 * L), 0)
    uc = lax.broadcasted_iota(I32, (L, 2 * L), 1)
    u = jnp.where((uc >= L) | (ur > uc), 1.0, 0.0).astype(BF16)

    def step(qi, c0, mask, row_lo=0):
        kj = k_ref[pl.ds(c0, kb), :]
        vj = v_ref[pl.ds(c0, kb), :]
        tile_rows = pl.ds(row_lo, tq - row_lo)
        z = lax.dot_general(qi[row_lo:], kj, (((1,), (1,)), ((), ())),
                            preferred_element_type=F32) * scale
        l1m = -(jnp.maximum(z, 0.0) + jnp.log(1.0 + jnp.exp(-jnp.abs(z))))
        lm = l1m if mask is None else jnp.where(mask, l1m, 0.0)
        lm16 = lm.astype(BF16)
        cs_far = jnp.dot(lm16[:, :L], u, preferred_element_type=F32)
        cs_near = jnp.dot(lm16[:, L:], u, preferred_element_type=F32)
        r0 = r_s[tile_rows, :]
        r1 = r0 + cs_near[:, L:]
        rest = jnp.concatenate([cs_far[:, :L] + r1, cs_near[:, :L] + r0], axis=1)
        a = jnp.exp((z + l1m) + rest)
        if mask is not None:
            a = jnp.where(mask, a, 0.0)
        acc_s[tile_rows, :] += jnp.dot(a.astype(BF16), vj, preferred_element_type=F32)
        r_s[tile_rows, :] = r1 + cs_far[:, L:]

    def qtile(t, _):
        q0 = pl.multiple_of(t * tq, tq)
        qi = q_ref[pl.ds(q0, tq), :]
        acc_s[...] = jnp.zeros_like(acc_s)
        r_s[...] = jnp.zeros_like(r_s)
        for off in range(tq - kb, -1, -kb):
            step(qi, pl.multiple_of(q0 + off, kb), (cols + off < rows)[off:], row_lo=off)

        def earlier(jj, _):
            c0 = q0 - (jj + 1) * (2 * kb)
            step(qi, pl.multiple_of(c0 + kb, kb), None)
            step(qi, pl.multiple_of(c0, kb), None)
            return 0

        lax.fori_loop(0, t * (tq // (2 * kb)), earlier, 0)
        o_ref[pl.ds(q0, tq), :] = acc_s[...].astype(o_ref.dtype)
        return 0

    lax.fori_loop(0, seq // tq, qtile, 0)


def _stick_breaking(seg_b):
    b, s, _ = seg_b.shape
    h = SB_HEADS
    assert SB_KB == 2 * SB_BLOCK and SB_TQ % (2 * SB_KB) == 0 and s % SB_TQ == 0
    return pl.pallas_call(
        functools.partial(_sb_kernel, seq=s),
        grid=(b, h),
        in_specs=[pl.BlockSpec((None, s, SB_HD), lambda bi, hi: (bi, 0, hi)),
                  pl.BlockSpec((None, s, SB_HD), lambda bi, hi: (bi, 0, h + hi)),
                  pl.BlockSpec((None, s, SB_HD), lambda bi, hi: (bi, 0, 2 * h + hi))],
        out_specs=pl.BlockSpec((None, s, SB_HD), lambda bi, hi: (bi, 0, hi)),
        out_shape=jax.ShapeDtypeStruct((b, s, h * SB_HD), BF16),
        scratch_shapes=[pltpu.VMEM((SB_TQ, SB_HD), F32), pltpu.VMEM((SB_TQ, SB_BLOCK), F32)],
        compiler_params=_cparams(("arbitrary", "arbitrary")),
        name="stick_breaking",
    )(seg_b, seg_b, seg_b)


def _merge_kernel(hm_ref, hs_ref, wa_ref, wb_ref, ga_ref, gb_ref, o_ref, wa16, wb16):
    @pl.when(pl.program_id(1) == 0)
    def _():
        wa16[...] = wa_ref[...].astype(BF16)
        wb16[...] = wb_ref[...].astype(BF16)

    ya = jnp.dot(hm_ref[...], wa16[...], preferred_element_type=F32)
    yb = jnp.dot(hs_ref[...], wb16[...], preferred_element_type=F32)
    y = _sigmoid(ga_ref[...].astype(F32)) * ya + _sigmoid(gb_ref[...].astype(F32)) * yb
    o_ref[...] = y.astype(o_ref.dtype)


def _merge(hm, hs, w_a, w_b, seg_b, gate_a_col, gate_b_col, d_model):
    m, ka = hm.shape
    kb = hs.shape[1]
    tm, tn = MM_TM, MM_TN
    ga_blk = gate_a_col // tn
    gb_blk = gate_b_col // tn
    return pl.pallas_call(
        _merge_kernel,
        grid=(d_model // tn, m // tm),
        in_specs=[pl.BlockSpec((tm, ka), lambda j, i: (i, 0)),
                  pl.BlockSpec((tm, kb), lambda j, i: (i, 0)),
                  pl.BlockSpec((ka, tn), lambda j, i: (0, j)),
                  pl.BlockSpec((kb, tn), lambda j, i: (0, j)),
                  pl.BlockSpec((tm, tn), lambda j, i: (i, ga_blk + j)),
                  pl.BlockSpec((tm, tn), lambda j, i: (i, gb_blk + j))],
        out_specs=pl.BlockSpec((tm, tn), lambda j, i: (i, j)),
        out_shape=jax.ShapeDtypeStruct((m, d_model), BF16),
        scratch_shapes=[pltpu.VMEM((ka, tn), BF16), pltpu.VMEM((kb, tn), BF16)],
        compiler_params=_cparams(("arbitrary", "arbitrary")),
        name="merge",
    )(hm, hs, w_a, w_b, seg_b, seg_b)


def _xattn_kernel(q_ref, k_ref, v_ref, o_ref, *, scale):
    s = lax.dot_general(q_ref[...], k_ref[...], (((1,), (1,)), ((), ())),
                        preferred_element_type=F32) * scale
    p = jnp.exp(s - jnp.max(s, axis=1, keepdims=True))
    p = p / jnp.sum(p, axis=1, keepdims=True)
    o_ref[...] = jnp.dot(p.astype(BF16), v_ref[...], preferred_element_type=F32).astype(o_ref.dtype)


def _xattn(q, kv, tq=2048):
    b, s, d = q.shape
    n_mem = kv.shape[1]
    hd = d // XA_HEADS
    return pl.pallas_call(
        functools.partial(_xattn_kernel, scale=1.0 / math.sqrt(hd)),
        grid=(b, XA_HEADS, s // tq),
        in_specs=[pl.BlockSpec((None, tq, hd), lambda bi, hi, i: (bi, i, hi)),
                  pl.BlockSpec((None, n_mem, hd), lambda bi, hi, i: (bi, 0, hi)),
                  pl.BlockSpec((None, n_mem, hd), lambda bi, hi, i: (bi, 0, XA_HEADS + hi))],
        out_specs=pl.BlockSpec((None, tq, hd), lambda bi, hi, i: (bi, i, hi)),
        out_shape=jax.ShapeDtypeStruct((b, s, d), BF16),
        compiler_params=_cparams(("arbitrary", "arbitrary", "arbitrary")),
        name="xattn",
    )(q, kv, kv)


def _router_kernel(h_ref, g_ref, wr_ref, br_ref, hn_ref, id_ref, wt_ref, wh_ref, wl_ref):
    @pl.when(pl.program_id(0) == 0)
    def _():
        w = wr_ref[...]
        wh = w.astype(BF16)
        wh_ref[...] = wh
        wl_ref[...] = (w - wh.astype(F32)).astype(BF16)

    x = h_ref[...]
    hn = (x * lax.rsqrt(jnp.mean(x * x, axis=-1, keepdims=True) + EPS)) * g_ref[...]
    tk = hn.shape[1] // MOE_KS
    for c in range(MOE_KS):
        hn_ref[:, c * (tk // 2):(c + 1) * (tk // 2)] = _pack_bf16_pair(
            hn[:, c * tk:c * tk + tk // 2], hn[:, c * tk + tk // 2:(c + 1) * tk])
    xh = hn.astype(BF16)
    xl = (hn - xh.astype(F32)).astype(BF16)
    logits = (jnp.dot(xh, wh_ref[...], preferred_element_type=F32)
              + (jnp.dot(xh, wl_ref[...], preferred_element_type=F32)
                 + jnp.dot(xl, wh_ref[...], preferred_element_type=F32))) + br_ref[...]

    tm = logits.shape[0]
    lane = lax.broadcasted_iota(I32, (tm, LANES), 1)
    lane_f = lane.astype(F32)
    ninf = -jnp.inf

    def first_lane_of(v, vmax):
        return jnp.min(jnp.where(v == vmax, lane_f, float(LANES)), axis=1, keepdims=True)

    gl = jnp.where(lane < N_GROUPS, logits, ninf)
    gmax = jnp.max(gl, axis=1, keepdims=True)
    g_w = 1.0 / jnp.sum(jnp.exp(gl - gmax), axis=1, keepdims=True)
    g_idx = first_lane_of(gl, gmax)
    lo = float(N_GROUPS) + g_idx * float(EXPERTS_PER_GROUP)
    in_group = (lane_f >= lo) & (lane_f < lo + float(EXPERTS_PER_GROUP))
    el = jnp.where(in_group, logits, ninf)
    m1 = jnp.max(el, axis=1, keepdims=True)
    i1 = first_lane_of(el, m1)
    el2 = jnp.where(lane_f == i1, ninf, el)
    m2 = jnp.max(el2, axis=1, keepdims=True)
    i2 = first_lane_of(el2, m2)
    denom = jnp.sum(jnp.exp(el - m1), axis=1, keepdims=True)
    p1 = 1.0 / denom
    p2 = jnp.exp(m2 - m1) / denom
    psum = p1 + p2
    w1 = g_w * (p1 / psum)
    w2 = g_w * (p2 / psum)
    ids = jnp.where(lane == 0, i1 - float(N_GROUPS), jnp.where(lane == 1, i2 - float(N_GROUPS), 0.0))
    id_ref[...] = ids.astype(I32)
    wt_ref[...] = jnp.where(lane == 0, w1, jnp.where(lane == 1, w2, 0.0))


def _router(h, g, w_r, b_r):
    t, d = h.shape
    tm = ROUTE_TM
    return pl.pallas_call(
        _router_kernel,
        grid=(t // tm,),
        in_specs=[pl.BlockSpec((tm, d), lambda i: (i, 0)),
                  pl.BlockSpec((1, d), lambda i: (0, 0)),
                  pl.BlockSpec((d, LANES), lambda i: (0, 0)),
                  pl.BlockSpec((1, LANES), lambda i: (0, 0))],
        out_specs=[pl.BlockSpec((tm, d // 2), lambda i: (i, 0)),
                   pl.BlockSpec((tm, LANES), lambda i: (i, 0)),
                   pl.BlockSpec((tm, LANES), lambda i: (i, 0))],
        out_shape=[jax.ShapeDtypeStruct((t, d // 2), U32),
                   jax.ShapeDtypeStruct((t, LANES), I32),
                   jax.ShapeDtypeStruct((t, LANES), F32)],
        scratch_shapes=[pltpu.VMEM((d, LANES), BF16), pltpu.VMEM((d, LANES), BF16)],
        compiler_params=_cparams(("arbitrary",)),
        name="router",
    )(h, g.reshape(1, d).astype(F32), w_r, b_r)


def _lane_cumsum(x):
    lane = lax.broadcasted_iota(I32, x.shape, 1)
    s = 1
    while s < LANES:
        x = x + jnp.where(lane >= s, pltpu.roll(x, s, axis=1), 0.0)
        s *= 2
    return x


def _meta_kernel(ids_ref, dest_ref, blk_ref, misc_ref, rank_s, *, n_tok, n_blocks):
    tb = META_TB
    lane = lax.broadcasted_iota(I32, (tb, LANES), 1)
    lower = jnp.where(lax.broadcasted_iota(I32, (tb, tb), 0) > lax.broadcasted_iota(I32, (tb, tb), 1),
                      1.0, 0.0).astype(BF16)

    def onehots(b):
        ids = ids_ref[pl.ds(pl.multiple_of(b * tb, tb), tb), :]
        return lane == ids[:, 0:1], lane == ids[:, 1:2]

    def lanes01(v0, v1):
        return jnp.where(lane == 0, v0, jnp.where(lane == 1, v1, 0.0))

    def count(b, carry):
        o1, o2 = onehots(b)
        cnt = jnp.where(o1, 1.0, 0.0) + jnp.where(o2, 1.0, 0.0)
        before = jnp.dot(lower, cnt.astype(BF16), preferred_element_type=F32) + carry
        r1 = jnp.sum(jnp.where(o1, before, 0.0), axis=1, keepdims=True)
        r2 = jnp.sum(jnp.where(o2, before, 0.0), axis=1, keepdims=True)
        rank_s[pl.ds(pl.multiple_of(b * tb, tb), tb), :] = lanes01(r1, r2)
        return carry + jnp.sum(cnt, axis=0, keepdims=True)

    counts = lax.fori_loop(0, n_tok // tb, count, jnp.zeros((1, LANES), F32))

    cnt8 = jnp.broadcast_to(counts, (8, LANES))
    whole = jnp.floor(cnt8 * (1.0 / MOE_ALIGN)) * MOE_ALIGN
    seg = jnp.where(cnt8 > whole, whole + MOE_ALIGN, whole)
    seg_start = _lane_cumsum(seg) - seg
    nblk = jnp.floor((cnt8 + (MOE_SBLK - 1)) * (1.0 / MOE_SBLK))
    blk_end = _lane_cumsum(nblk)
    blk_start = blk_end - nblk
    row_start = seg_start[0:1, :]

    def place(b, _):
        o1, o2 = onehots(b)
        s1 = jnp.sum(jnp.where(o1, row_start, 0.0), axis=1, keepdims=True)
        s2 = jnp.sum(jnp.where(o2, row_start, 0.0), axis=1, keepdims=True)
        sl = pl.ds(pl.multiple_of(b * tb, tb), tb)
        dest_ref[sl, :] = (rank_s[sl, :] + lanes01(s1, s2)).astype(I32)
        return 0

    lax.fori_loop(0, n_tok // tb, place, 0)

    step = lax.broadcasted_iota(I32, (n_blocks, LANES), 0).astype(F32)
    elane = lax.broadcasted_iota(I32, (n_blocks, LANES), 1)
    mine = (elane < N_EXPERTS) & (blk_start[0:1, :] <= step) & (step < blk_end[0:1, :])
    done_rows = (step - blk_start[0:1, :]) * float(MOE_SBLK)

    def pick(v):
        return jnp.sum(jnp.where(mine, v, 0.0), axis=1, keepdims=True)

    s_exp = pick(elane.astype(F32))
    s_row = pick(row_start + done_rows)
    s_val = pick(jnp.minimum(counts - done_rows, float(MOE_SBLK)))
    blk_ref[...] = jnp.where(elane == 0, s_exp, jnp.where(elane == 1, s_row, jnp.where(
        elane == 2, s_val, 0.0))).astype(I32)

    lane8 = lax.broadcasted_iota(I32, (8, LANES), 1)
    sub8 = lax.broadcasted_iota(I32, (8, LANES), 0)
    pad_row = jnp.where((cnt8 > whole) & (lane8 < N_EXPERTS), seg_start + whole, -1.0)
    n_used = jnp.broadcast_to(blk_end[:, N_EXPERTS - 1:N_EXPERTS], (8, LANES))
    misc_ref[...] = jnp.where(sub8 == 0, pad_row, jnp.where(sub8 == 1, n_used, 0.0)).astype(I32)


def _moe_meta(ids, n_blocks):
    t = ids.shape[0]
    assert MOE_ALIGN & (MOE_ALIGN - 1) == 0 and MOE_SBLK & (MOE_SBLK - 1) == 0
    return pl.pallas_call(
        functools.partial(_meta_kernel, n_tok=t, n_blocks=n_blocks),
        out_shape=[jax.ShapeDtypeStruct((t, LANES), I32),
                   jax.ShapeDtypeStruct((n_blocks, LANES), I32),
                   jax.ShapeDtypeStruct((8, LANES), I32)],
        scratch_shapes=[pltpu.VMEM((t, LANES), F32)],
        compiler_params=pltpu.CompilerParams(vmem_limit_bytes=VMEM_LIMIT),
        name="moe_meta",
    )(ids)


def _dispatch_kernel(dest_ref, misc_ref, x_ref, xs_ref, zbuf, sem, *, n_assign):
    tb = x_ref.shape[0]
    zrows = zbuf.shape[0]

    @pl.when(pl.program_id(0) == 0)
    def _():
        zbuf[...] = jnp.zeros_like(zbuf)

        def tail_copy(j):
            return pltpu.make_async_copy(zbuf, xs_ref.at[pl.ds(n_assign + j * zrows, zrows)], sem)

        n_tail = (xs_ref.shape[0] - n_assign) // zrows
        for j in range(n_tail):
            tail_copy(j).start()
        for j in range(n_tail):
            tail_copy(j).wait()

        def pad_copy(row):
            return pltpu.make_async_copy(zbuf.at[pl.ds(0, MOE_ALIGN)],
                                         xs_ref.at[pl.ds(row, MOE_ALIGN)], sem)

        def pads(fn):
            def body(e, _):
                row = misc_ref[0, e]

                @pl.when(row >= 0)
                def _():
                    fn(pad_copy(pl.multiple_of(row, MOE_ALIGN)))
                return 0
            lax.fori_loop(0, N_EXPERTS, body, 0)

        pads(lambda c: c.start())
        pads(lambda c: c.wait())

    def row_copy(r, d):
        return pltpu.make_async_copy(x_ref.at[pl.ds(r, 1)], xs_ref.at[pl.ds(d, 1)], sem)

    def scatters(fn):
        def body(g, _):
            r0 = pl.multiple_of(g * SUBLANES, SUBLANES)
            for q in range(SUBLANES):
                for j in range(2):
                    fn(row_copy(r0 + q, dest_ref[0, 0, 2 * (r0 + q) + j]))
            return 0
        lax.fori_loop(0, tb // SUBLANES, body, 0)

    scatters(lambda c: c.start())
    scatters(lambda c: c.wait())


def _dispatch(dest_blocks, misc, hn, xs_rows):
    t, d = hn.shape
    tb = DISPATCH_TB
    n_assign = 2 * t
    assert (xs_rows - n_assign) % MOE_SUB == 0
    return pl.pallas_call(
        functools.partial(_dispatch_kernel, n_assign=n_assign),
        grid=(t // tb,),
        in_specs=[pl.BlockSpec((1, 1, 2 * tb), lambda i: (i, 0, 0), memory_space=pltpu.SMEM),
                  pl.BlockSpec(memory_space=pltpu.SMEM),
                  pl.BlockSpec((tb, d), lambda i: (i, 0))],
        out_specs=pl.BlockSpec(memory_space=pl.ANY),
        out_shape=jax.ShapeDtypeStruct((xs_rows, d), hn.dtype),
        scratch_shapes=[pltpu.VMEM((MOE_SUB, d), hn.dtype), pltpu.SemaphoreType.DMA(())],
        compiler_params=_cparams(("arbitrary",)),
        name="moe_dispatch",
    )(dest_blocks, misc, hn)


def _expert_kernel(be_ref, rs_ref, nv_ref, nu_ref, xs_ref, wg_ref, wu_ref, wd_ref, ys_ref,
                   gacc, uacc, wg16, wu16, wd16, ybuf, sem, *, n_assign):
    del be_ref
    i = pl.program_id(0)
    k = pl.program_id(1)
    n_used = nu_ref[0]
    sub_shift = MOE_SUB.bit_length() - 1

    def out_copies(j, fn):
        row0 = rs_ref[j]
        rows = ((nv_ref[j] + (MOE_ALIGN - 1)) // MOE_ALIGN) * MOE_ALIGN
        n_full = rows >> sub_shift
        rem = rows & (MOE_SUB - 1)

        def piece_copy(off, size):
            src_off = off if isinstance(off, int) else pl.multiple_of(off, MOE_ALIGN)
            return pltpu.make_async_copy(
                ybuf.at[pl.ds(src_off, size)],
                ys_ref.at[pl.ds(pl.multiple_of(row0 + off, MOE_ALIGN), size)], sem)

        for s in range(MOE_SBLK // MOE_SUB):
            @pl.when(s < n_full)
            def _():
                fn(piece_copy(s * MOE_SUB, MOE_SUB))
        piece = MOE_SUB // 2
        while piece >= MOE_ALIGN:
            off = n_full * MOE_SUB + (rem & (MOE_SUB - 2 * piece))

            @pl.when((rem & piece) != 0)
            def _():
                fn(piece_copy(off, piece))
            piece //= 2

    @pl.when((i == 0) & (k == 0))
    def _():
        ybuf[...] = jnp.zeros_like(ybuf)

        def tail_copy(j):
            return pltpu.make_async_copy(ybuf, ys_ref.at[pl.ds(n_assign + j * MOE_SBLK, MOE_SBLK)], sem)

        n_tail = (ys_ref.shape[0] - n_assign) // MOE_SBLK
        for j in range(n_tail):
            tail_copy(j).start()
        for j in range(n_tail):
            tail_copy(j).wait()

    @pl.when(i < n_used)
    def _():
        n_sub = (nv_ref[i] + (MOE_SUB - 1)) >> sub_shift
        wg16[...] = wg_ref[...].astype(BF16)
        wu16[...] = wu_ref[...].astype(BF16)

        @pl.when(k == 0)
        def _():
            gacc[...] = jnp.zeros_like(gacc)
            uacc[...] = jnp.zeros_like(uacc)

        def up(r, _):
            rows = pl.ds(pl.multiple_of(r * MOE_SUB, MOE_SUB), MOE_SUB)
            x = jnp.concatenate(_unpack_bf16_pair(xs_ref[rows, :]), axis=1).astype(BF16)
            gacc[rows, :] += jnp.dot(x, wg16[...], preferred_element_type=F32)
            uacc[rows, :] += jnp.dot(x, wu16[...], preferred_element_type=F32)
            return 0

        lax.fori_loop(0, n_sub, up, 0)

        @pl.when(k == MOE_KS - 1)
        def _():
            wd16[...] = wd_ref[...].astype(BF16)

            @pl.when(i > 0)
            def _():
                out_copies(i - 1, lambda c: c.wait())

            def down(r, _):
                rows = pl.ds(pl.multiple_of(r * MOE_SUB, MOE_SUB), MOE_SUB)
                g = gacc[rows, :]
                hb = (g * _sigmoid(g)) * uacc[rows, :]
                y = jnp.dot(hb.astype(BF16), wd16[...], preferred_element_type=F32)
                half = y.shape[1] // 2
                ybuf[rows, :] = _pack_bf16_pair(y[:, :half], y[:, half:])
                return 0

            lax.fori_loop(0, n_sub, down, 0)
            out_copies(i, lambda c: c.start())

            @pl.when(i == n_used - 1)
            def _():
                out_copies(i, lambda c: c.wait())


def _experts(step_expert, step_row, step_valid, n_used, xs, w_gate, w_up, w_down, n_assign):
    xs_rows = xs.shape[0]
    n_blocks = step_expert.shape[0]
    d, f = w_gate.shape[-2:]
    tk = d // MOE_KS
    assert (xs_rows - n_assign) % MOE_SBLK == 0 and xs.shape[1] == d // 2

    def blk(i, nu):
        return jnp.minimum(i, nu[0] - 1)

    def kk(i, k, nu):
        return jnp.where(i < nu[0], k, MOE_KS - 1)

    grid_spec = pltpu.PrefetchScalarGridSpec(
        num_scalar_prefetch=4,
        grid=(n_blocks, MOE_KS),
        in_specs=[
            pl.BlockSpec((pl.Element(MOE_SBLK), pl.Element(tk // 2)),
                         lambda i, k, be, rs, nv, nu: (pl.multiple_of(rs[blk(i, nu)], MOE_ALIGN),
                                                       kk(i, k, nu) * (tk // 2))),
            pl.BlockSpec((None, tk, f), lambda i, k, be, rs, nv, nu: (be[blk(i, nu)], kk(i, k, nu), 0)),
            pl.BlockSpec((None, tk, f), lambda i, k, be, rs, nv, nu: (be[blk(i, nu)], kk(i, k, nu), 0)),
            pl.BlockSpec((None, f, d), lambda i, k, be, rs, nv, nu: (be[blk(i, nu)], 0, 0)),
        ],
        out_specs=pl.BlockSpec(memory_space=pl.ANY),
        scratch_shapes=[pltpu.VMEM((MOE_SBLK, f), F32), pltpu.VMEM((MOE_SBLK, f), F32),
                        pltpu.VMEM((tk, f), BF16), pltpu.VMEM((tk, f), BF16),
                        pltpu.VMEM((f, d), BF16), pltpu.VMEM((MOE_SBLK, d // 2), U32),
                        pltpu.SemaphoreType.DMA(())],
    )
    return pl.pallas_call(
        functools.partial(_expert_kernel, n_assign=n_assign),
        grid_spec=grid_spec,
        out_shape=jax.ShapeDtypeStruct((xs_rows, d // 2), U32),
        compiler_params=_cparams(("arbitrary", "arbitrary")),
        name="moe_experts",
    )(step_expert, step_row, step_valid, n_used, xs, w_gate, w_up, w_down)


def _combine_kernel(dest_ref, dest_next_ref, w_ref, h_ref, g_ref, ys_ref, o_ref, ybuf, sems, *,
                    final_norm):
    tb = h_ref.shape[0]
    i = pl.program_id(0)
    slot = i % 2

    def gathers(rows_ref, s, fn):
        def body(g, _):
            r0 = pl.multiple_of(g * SUBLANES, SUBLANES)
            for q in range(SUBLANES):
                for j in range(2):
                    fn(pltpu.make_async_copy(ys_ref.at[pl.ds(rows_ref[0, 0, 2 * (r0 + q) + j], 1)],
                                             ybuf.at[s, j, pl.ds(r0 + q, 1)], sems.at[s]))
            return 0
        lax.fori_loop(0, tb // SUBLANES, body, 0)

    @pl.when(i == 0)
    def _():
        gathers(dest_ref, 0, lambda c: c.start())

    @pl.when(i + 1 < pl.num_programs(0))
    def _():
        gathers(dest_next_ref, 1 - slot, lambda c: c.start())

    gathers(dest_ref, slot, lambda c: c.wait())

    w = w_ref[...]
    lo0, hi0 = _unpack_bf16_pair(ybuf[slot, 0])
    lo1, hi1 = _unpack_bf16_pair(ybuf[slot, 1])
    w0, w1 = w[:, 0:1], w[:, 1:2]
    h = h_ref[...] + jnp.concatenate([w0 * lo0 + w1 * lo1, w0 * hi0 + w1 * hi1], axis=1)
    if final_norm:
        h = (h * lax.rsqrt(jnp.mean(h * h, axis=-1, keepdims=True) + EPS)) * g_ref[...]
    o_ref[...] = h


def _combine(dest_blocks, wts, h, g, ys, final_norm):
    t, d = h.shape
    tb = COMBINE_TB
    n_tiles = t // tb
    return pl.pallas_call(
        functools.partial(_combine_kernel, final_norm=final_norm),
        grid=(n_tiles,),
        in_specs=[pl.BlockSpec((1, 1, 2 * tb), lambda i: (i, 0, 0), memory_space=pltpu.SMEM),
                  pl.BlockSpec((1, 1, 2 * tb), lambda i: (jnp.minimum(i + 1, n_tiles - 1), 0, 0),
                               memory_space=pltpu.SMEM),
                  pl.BlockSpec((tb, LANES), lambda i: (i, 0)),
                  pl.BlockSpec((tb, d), lambda i: (i, 0)),
                  pl.BlockSpec((1, d), lambda i: (0, 0)),
                  pl.BlockSpec(memory_space=pl.ANY)],
        out_specs=pl.BlockSpec((tb, d), lambda i: (i, 0)),
        out_shape=jax.ShapeDtypeStruct((t, d), F32),
        scratch_shapes=[pltpu.VMEM((2, 2, tb, d // 2), U32), pltpu.SemaphoreType.DMA((2,))],
        compiler_params=_cparams(("arbitrary",)),
        name="moe_combine",
    )(dest_blocks, dest_blocks, wts, h, g.reshape(1, d).astype(F32), ys)


def kernel(x, mem, norm_mix, w_in, conv_qk, b_gates, g_mlstm, w_proj_a, w_proj_b, w_out,
           norm_xattn, norm_mem, w_q_mem, w_kv_mem, w_o_mem, norm_moe,
           w_router_group, b_router_group, w_router_expert, b_router_expert,
           w_gate, w_up, w_down, norm_final):
    b, s, d = x.shape
    t = b * s
    n_mem = mem.shape[1]
    depth = w_in.shape[0]
    ml_qk_w = ML_HEADS * ML_QK
    ml_v_w = ML_HEADS * ML_V
    sb_w = SB_HEADS * SB_HD
    seg_a_w = 2 * ml_qk_w + 2 * ml_v_w
    n_gate_cols = 2 * ML_HEADS
    seg_b_w = 3 * sb_w + 2 * d
    n_assign = 2 * t
    n_blocks = n_assign // MOE_SBLK + N_EXPERTS
    xs_rows = n_assign + N_EXPERTS * MOE_ALIGN + MOE_SBLK

    h = x.reshape(t, d)
    mem2 = mem.reshape(b * n_mem, d)
    for l in range(depth):
        xn = _rmsnorm(h, norm_mix[l], BF16)
        w_in_t = jnp.swapaxes(w_in[l], 0, 1)
        seg_a = _matmul(xn, w_in_t, seg_a_w, BF16, transposed=True, name="in_proj_a")
        gates = _matmul(xn, w_in_t, LANES, F32, tn=LANES, col_start=seg_a_w, transposed=True,
                        name="in_proj_gates")
        seg_b = _matmul(xn, w_in_t, seg_b_w, BF16, col_start=seg_a_w + n_gate_cols, transposed=True,
                        name="in_proj_b")
        gates_t = gates[:, :n_gate_cols].reshape(b, s, 2, ML_HEADS).transpose(0, 2, 3, 1)
        gates_t = gates_t.reshape(b, 2, ML_HEADS, s // ML_CHUNK, ML_CHUNK)
        hm = _mlstm(seg_a.reshape(b, s, seg_a_w), gates_t, conv_qk[l], b_gates[l], g_mlstm[l])
        hs = _stick_breaking(seg_b.reshape(b, s, seg_b_w))
        y = _merge(hm.reshape(t, ml_v_w), hs.reshape(t, sb_w), w_proj_a[l], w_proj_b[l],
                   seg_b, 3 * sb_w, 3 * sb_w + d, d)
        h = _matmul(y, w_out[l], d, F32, res=h, name="out_proj")
        hn = _rmsnorm(h, norm_xattn[l], BF16)
        q = _matmul(hn, w_q_mem[l], d, BF16, name="xattn_q")
        memn = _rmsnorm(mem2, norm_mem[l], BF16)
        kv = _matmul(memn, w_kv_mem[l], 2 * d, BF16, name="xattn_kv")
        o = _xattn(q.reshape(b, s, d), kv.reshape(b, n_mem, 2 * d))
        h = _matmul(o.reshape(t, d), w_o_mem[l], d, F32, res=h, name="xattn_o")
        w_r = jnp.pad(jnp.concatenate([w_router_group[l], w_router_expert[l]], axis=1),
                      ((0, 0), (0, LANES - N_GROUPS - N_EXPERTS)))
        b_r = jnp.pad(jnp.concatenate([b_router_group[l], b_router_expert[l]]),
                      (0, LANES - N_GROUPS - N_EXPERTS)).reshape(1, LANES).astype(F32)
        hn3, ids, wts = _router(h, norm_moe[l], w_r, b_r)
        dest, steps, misc = _moe_meta(ids, n_blocks)
        xs = _dispatch(dest[:, :2].reshape(t // DISPATCH_TB, 1, 2 * DISPATCH_TB), misc, hn3, xs_rows)
        ys = _experts(steps[:, 0], steps[:, 1], steps[:, 2], misc[1, :1], xs,
                      w_gate[l], w_up[l], w_down[l], n_assign)
        h = _combine(dest[:, :2].reshape(t // COMBINE_TB, 1, 2 * COMBINE_TB), wts, h,
                     norm_final, ys, final_norm=(l == depth - 1))
    return h.reshape(b, s, d)
```

```python
import functools
import math

import jax
import jax.numpy as jnp
from jax import lax
from jax.experimental import pallas as pl
from jax.experimental.pallas import tpu as pltpu

F32 = jnp.float32
BF16 = jnp.bfloat16
I32 = jnp.int32
U32 = jnp.uint32

EPS = 1e-6
ML_HEADS = 8
ML_QK = 128
ML_V = 256
ML_CHUNK = 128
CONV_W = 4
ML_HPS = 2
SB_HEADS = 16
SB_HD = 128
SB_BLOCK = 128
XA_HEADS = 4
N_GROUPS = 8
EXPERTS_PER_GROUP = 8
N_EXPERTS = N_GROUPS * EXPERTS_PER_GROUP
D_EXPERT = 512

LANES = 128
SUBLANES = 8
VMEM_LIMIT = 56 * 1024 * 1024

SB_TQ = 1024
SB_KB = 256
MM_TM = 1024
MM_TN = 512
MM_CAST_ROWS = 512
MOE_ALIGN = 8
MOE_SBLK = 512
MOE_SUB = 128
MOE_KS = 2
ROUTE_TM = 512
META_TB = 256
DISPATCH_TB = 512
COMBINE_TB = 256


def _cparams(sem, vmem=VMEM_LIMIT):
    return pltpu.CompilerParams(dimension_semantics=sem, vmem_limit_bytes=vmem)


def _sigmoid(x):
    return 1.0 / (1.0 + jnp.exp(-x))


def _pack_bf16_pair(lo, hi):
    lo_bits = lax.bitcast_convert_type(lo.astype(BF16).astype(F32), U32)
    hi_bits = lax.bitcast_convert_type(hi.astype(BF16).astype(F32), U32)
    return (lo_bits >> 16) | (hi_bits & jnp.uint32(0xFFFF0000))


def _unpack_bf16_pair(p):
    lo = lax.bitcast_convert_type(p << 16, F32)
    hi = lax.bitcast_convert_type(p & jnp.uint32(0xFFFF0000), F32)
    return lo, hi


def _neg_softplus(x):
    return -(jnp.maximum(x, 0.0) + jnp.log1p(jnp.exp(-jnp.abs(x))))


def _rmsnorm_kernel(x_ref, g_ref, o_ref):
    x = x_ref[...].astype(F32)
    ms = jnp.mean(x * x, axis=-1, keepdims=True)
    o_ref[...] = ((x * lax.rsqrt(ms + EPS)) * g_ref[...]).astype(o_ref.dtype)


def _rmsnorm(x, g, out_dtype, tm=512):
    m, d = x.shape
    return pl.pallas_call(
        _rmsnorm_kernel,
        grid=(m // tm,),
        in_specs=[pl.BlockSpec((tm, d), lambda i: (i, 0)),
                  pl.BlockSpec((1, d), lambda i: (0, 0))],
        out_specs=pl.BlockSpec((tm, d), lambda i: (i, 0)),
        out_shape=jax.ShapeDtypeStruct((m, d), out_dtype),
        compiler_params=_cparams(("arbitrary",)),
        name="rmsnorm",
    )(x, g.reshape(1, d).astype(F32))


def _mm_kernel(*refs, has_res, transposed):
    a_ref, w_ref = refs[0], refs[1]
    r_ref = refs[2] if has_res else None
    o_ref, w16_ref = refs[-2], refs[-1]
    k, tn = w16_ref.shape

    @pl.when(pl.program_id(1) == 0)
    def _():
        if transposed:
            for k0 in range(0, k, MM_CAST_ROWS):
                ks = pl.ds(k0, MM_CAST_ROWS)
                w16_ref[ks, :] = w_ref[:, ks].T.astype(BF16)
        else:
            w16_ref[...] = w_ref[...].astype(BF16)

    acc = jnp.dot(a_ref[...], w16_ref[...], preferred_element_type=F32)
    if has_res:
        acc = r_ref[...] + acc
    o_ref[...] = acc.astype(o_ref.dtype)


def _matmul(a, w, n_cols, out_dtype, res=None, tn=MM_TN, col_start=0, transposed=False,
            name="matmul"):
    m, k = a.shape
    tm = min(MM_TM, m)
    grid = (n_cols // tn, m // tm)
    if transposed:
        assert col_start % SUBLANES == 0
        w_spec = pl.BlockSpec((pl.Element(tn), pl.Element(k)),
                              lambda j, i: (pl.multiple_of(col_start + j * tn, SUBLANES), 0))
    else:
        base_blk, rem = divmod(col_start, tn)
        assert rem == 0
        w_spec = pl.BlockSpec((k, tn), lambda j, i: (0, base_blk + j))
    in_specs = [pl.BlockSpec((tm, k), lambda j, i: (i, 0)), w_spec]
    args = [a, w]
    if res is not None:
        in_specs.append(pl.BlockSpec((tm, tn), lambda j, i: (i, j)))
        args.append(res)
    return pl.pallas_call(
        functools.partial(_mm_kernel, has_res=res is not None, transposed=transposed),
        grid=grid,
        in_specs=in_specs,
        out_specs=pl.BlockSpec((tm, tn), lambda j, i: (i, j)),
        out_shape=jax.ShapeDtypeStruct((m, n_cols), out_dtype),
        scratch_shapes=[pltpu.VMEM((k, tn), BF16)],
        compiler_params=_cparams(("arbitrary", "arbitrary")),
        name=name,
    )(*args)


def _mlstm_kernel(bg_ref, q_ref, k_ref, v_ref, o_ref, gt_ref, cwq_ref, cwk_ref, gm_ref,
                  out_ref, qpad, kpad, bc_s, li_s, *, seq):
    head0 = pl.program_id(1) * ML_HPS
    n_chunks = seq // ML_CHUNK
    L = ML_CHUNK

    qpad[0:8, :] = jnp.zeros((8, ML_HPS * ML_QK), F32)
    kpad[0:8, :] = jnp.zeros((8, ML_HPS * ML_QK), F32)
    qpad[8:, :] = q_ref[...].astype(F32)
    kpad[8:, :] = k_ref[...].astype(F32)

    lane = lax.broadcasted_iota(I32, (n_chunks, L), 1)
    for hh in range(ML_HPS):
        li_s[hh] = gt_ref[0, hh] + bg_ref[head0 + hh]
        bc = _neg_softplus(-(gt_ref[1, hh] + bg_ref[ML_HEADS + head0 + hh]))
        for s in (1, 2, 4, 8, 16, 32, 64):
            bc = bc + jnp.where(lane >= s, pltpu.roll(bc, s, axis=1), 0.0)
        bc_s[hh] = bc

    rows = lax.broadcasted_iota(I32, (L, L), 0)
    cols = lax.broadcasted_iota(I32, (L, L), 1)
    eye = rows == cols
    causal = cols <= rows
    k_scale = 1.0 / math.sqrt(ML_QK)

    def to_col(row):
        return jnp.sum(jnp.where(eye, row, 0.0), axis=1, keepdims=True)

    def conv_silu(win, cw):
        y = (cw[0:1, :] * win[5:5 + L] + cw[1:2, :] * win[6:6 + L]
             + cw[2:3, :] * win[7:7 + L] + cw[3:4, :] * win[8:8 + L])
        return y * _sigmoid(y)

    def head_chunk(hh, c, r0, carry):
        c_st, n_st, m_st = carry
        qk_cols = pl.ds(hh * ML_QK, ML_QK)
        v_cols = pl.ds(hh * ML_V, ML_V)
        qb = conv_silu(qpad[pl.ds(r0, L + 8), qk_cols], cwq_ref[:, qk_cols])
        kb = conv_silu(kpad[pl.ds(r0, L + 8), qk_cols], cwk_ref[:, qk_cols]) * k_scale
        vb = v_ref[pl.ds(r0, L), v_cols]
        bc_row = bc_s[hh, pl.ds(c, 1), :]
        li_row = li_s[hh, pl.ds(c, 1), :]
        bc_col = to_col(bc_row)

        d = jnp.where(causal, bc_col - bc_row + li_row, -jnp.inf)
        inter = bc_col + m_st
        m_t = jnp.maximum(inter, jnp.max(d, axis=1, keepdims=True))
        w_intra = jnp.exp(d - m_t)
        w_inter = jnp.exp(inter - m_t)

        qb16 = qb.astype(BF16)
        kb16 = kb.astype(BF16)
        sc = lax.dot_general(qb16, kb16, (((1,), (1,)), ((), ())),
                             preferred_element_type=F32) * w_intra
        num = (w_inter * jnp.dot(qb16, c_st.astype(BF16), preferred_element_type=F32)
               + jnp.dot(sc.astype(BF16), vb, preferred_element_type=F32))
        den = (w_inter * jnp.sum(qb * n_st, axis=1, keepdims=True)
               + jnp.sum(sc, axis=1, keepdims=True))
        hval = num / jnp.maximum(jnp.abs(den), jnp.exp(-m_t))

        hn = hval * lax.rsqrt(jnp.mean(hval * hval, axis=1, keepdims=True) + EPS)
        og = _sigmoid(o_ref[pl.ds(r0, L), v_cols].astype(F32))
        out_ref[pl.ds(r0, L), v_cols] = (og * (hn * gm_ref[:, v_cols])).astype(out_ref.dtype)

        g = bc_row[:, L - 1:L]
        ds_row = g - bc_row + li_row
        m_new = jnp.maximum(g + m_st, jnp.max(ds_row, axis=1, keepdims=True))
        w_s = jnp.exp(ds_row - m_new)
        decay = jnp.exp(g + m_st - m_new)
        kw = kb * to_col(w_s)
        c_new = decay * c_st + jnp.dot(kw.T.astype(BF16), vb, preferred_element_type=F32)
        n_new = decay * n_st + jnp.sum(kw, axis=0, keepdims=True)
        return c_new, n_new, m_new

    def chunk(c, carry):
        r0 = pl.multiple_of(c * L, L)
        return tuple(head_chunk(hh, c, r0, carry[hh]) for hh in range(ML_HPS))

    init = (jnp.zeros((ML_QK, ML_V), F32), jnp.zeros((1, ML_QK), F32),
            jnp.full((1, 1), -1e30, F32))
    lax.fori_loop(0, n_chunks, chunk, (init,) * ML_HPS)


def _mlstm(seg_a, gates_t, conv_qk, b_gates, g_mlstm):
    b, s, _ = seg_a.shape
    h = ML_HEADS
    hp = ML_HPS
    n_chunks = s // ML_CHUNK
    groups = h // hp
    qk_w, v_w = hp * ML_QK, hp * ML_V
    return pl.pallas_call(
        functools.partial(_mlstm_kernel, seq=s),
        grid=(b, groups),
        in_specs=[
            pl.BlockSpec(memory_space=pltpu.SMEM),
            pl.BlockSpec((None, s, qk_w), lambda bi, gi: (bi, 0, gi)),
            pl.BlockSpec((None, s, qk_w), lambda bi, gi: (bi, 0, groups + gi)),
            pl.BlockSpec((None, s, v_w), lambda bi, gi: (bi, 0, groups + gi)),
            pl.BlockSpec((None, s, v_w), lambda bi, gi: (bi, 0, 2 * groups + gi)),
            pl.BlockSpec((None, 2, hp, n_chunks, ML_CHUNK), lambda bi, gi: (bi, 0, gi, 0, 0)),
            pl.BlockSpec((CONV_W, qk_w), lambda bi, gi: (0, gi)),
            pl.BlockSpec((CONV_W, qk_w), lambda bi, gi: (0, groups + gi)),
            pl.BlockSpec((1, v_w), lambda bi, gi: (0, gi)),
        ],
        out_specs=pl.BlockSpec((None, s, v_w), lambda bi, gi: (bi, 0, gi)),
        out_shape=jax.ShapeDtypeStruct((b, s, h * ML_V), BF16),
        scratch_shapes=[pltpu.VMEM((s + 8, qk_w), F32), pltpu.VMEM((s + 8, qk_w), F32),
                        pltpu.VMEM((hp, n_chunks, ML_CHUNK), F32),
                        pltpu.VMEM((hp, n_chunks, ML_CHUNK), F32)],
        compiler_params=_cparams(("arbitrary", "arbitrary")),
        name="mlstm",
    )(b_gates.astype(F32), seg_a, seg_a, seg_a, seg_a, gates_t, conv_qk.astype(F32),
      conv_qk.astype(F32), g_mlstm.reshape(1, -1).astype(F32))


def _sb_kernel(q_ref, k_ref, v_ref, o_ref, acc_s, r_s, *, seq):
    L = SB_BLOCK
    tq, kb = SB_TQ, SB_KB
    scale = 1.0 / math.sqrt(SB_HD)
    rows = lax.broadcasted_iota(I32, (tq, kb), 0)
    cols = lax.broadcasted_iota(I32, (tq, kb), 1)
    ur = lax.broadcasted_iota(I32, (L, 2 * L), 0)
    uc = lax.broadcasted_iota(I32, (L, 2 * L), 1)
    u = jnp.where((uc >= L) | (ur > uc), 1.0, 0.0).astype(BF16)

    def step(qi, c0, mask, row_lo=0):
        kj = k_ref[pl.ds(c0, kb), :]
        vj = v_ref[pl.ds(c0, kb), :]
        tile_rows = pl.ds(row_lo, tq - row_lo)
        z = lax.dot_general(qi[row_lo:], kj, (((1,), (1,)), ((), ())),
                            preferred_element_type=F32) * scale
        l1m = -(jnp.maximum(z, 0.0) + jnp.log(1.0 + jnp.exp(-jnp.abs(z))))
        lm = l1m if mask is None else jnp.where(mask, l1m, 0.0)
        lm16 = lm.astype(BF16)
        cs_far = jnp.dot(lm16[:, :L], u, preferred_element_type=F32)
        cs_near = jnp.dot(lm16[:, L:], u, preferred_element_type=F32)
        r0 = r_s[tile_rows, :]
        r1 = r0 + cs_near[:, L:]
        rest = jnp.concatenate([cs_far[:, :L] + r1, cs_near[:, :L] + r0], axis=1)
        a = jnp.exp((z + l1m) + rest)
        if mask is not None:
            a = jnp.where(mask, a, 0.0)
        acc_s[tile_rows, :] += jnp.dot(a.astype(BF16), vj, preferred_element_type=F32)
        r_s[tile_rows, :] = r1 + cs_far[:, L:]

    def qtile(t, _):
        q0 = pl.multiple_of(t * tq, tq)
        qi = q_ref[pl.ds(q0, tq), :]
        acc_s[...] = jnp.zeros_like(acc_s)
        r_s[...] = jnp.zeros_like(r_s)
        for off in range(tq - kb, -1, -kb):
            step(qi, pl.multiple_of(q0 + off, kb), (cols + off < rows)[off:], row_lo=off)

        def earlier(jj, _):
            c0 = q0 - (jj + 1) * (2 * kb)
            step(qi, pl.multiple_of(c0 + kb, kb), None)
            step(qi, pl.multiple_of(c0, kb), None)
            return 0

        lax.fori_loop(0, t * (tq // (2 * kb)), earlier, 0)
        o_ref[pl.ds(q0, tq), :] = acc_s[...].astype(o_ref.dtype)
        return 0

    lax.fori_loop(0, seq // tq, qtile, 0)


def _stick_breaking(seg_b):
    b, s, _ = seg_b.shape
    h = SB_HEADS
    assert SB_KB == 2 * SB_BLOCK and SB_TQ % (2 * SB_KB) == 0 and s % SB_TQ == 0
    return pl.pallas_call(
        functools.partial(_sb_kernel, seq=s),
        grid=(b, h),
        in_specs=[pl.BlockSpec((None, s, SB_HD), lambda bi, hi: (bi, 0, hi)),
                  pl.BlockSpec((None, s, SB_HD), lambda bi, hi: (bi, 0, h + hi)),
                  pl.BlockSpec((None, s, SB_HD), lambda bi, hi: (bi, 0, 2 * h + hi))],
        out_specs=pl.BlockSpec((None, s, SB_HD), lambda bi, hi: (bi, 0, hi)),
        out_shape=jax.ShapeDtypeStruct((b, s, h * SB_HD), BF16),
        scratch_shapes=[pltpu.VMEM((SB_TQ, SB_HD), F32), pltpu.VMEM((SB_TQ, SB_BLOCK), F32)],
        compiler_params=_cparams(("arbitrary", "arbitrary")),
        name="stick_breaking",
    )(seg_b, seg_b, seg_b)


def _merge_kernel(hm_ref, hs_ref, wa_ref, wb_ref, ga_ref, gb_ref, o_ref, wa16, wb16):
    @pl.when(pl.program_id(1) == 0)
    def _():
        wa16[...] = wa_ref[...].astype(BF16)
        wb16[...] = wb_ref[...].astype(BF16)

    ya = jnp.dot(hm_ref[...], wa16[...], preferred_element_type=F32)
    yb = jnp.dot(hs_ref[...], wb16[...], preferred_element_type=F32)
    y = _sigmoid(ga_ref[...].astype(F32)) * ya + _sigmoid(gb_ref[...].astype(F32)) * yb
    o_ref[...] = y.astype(o_ref.dtype)


def _merge(hm, hs, w_a, w_b, seg_b, gate_a_col, gate_b_col, d_model):
    m, ka = hm.shape
    kb = hs.shape[1]
    tm, tn = MM_TM, MM_TN
    ga_blk = gate_a_col // tn
    gb_blk = gate_b_col // tn
    return pl.pallas_call(
        _merge_kernel,
        grid=(d_model // tn, m // tm),
        in_specs=[pl.BlockSpec((tm, ka), lambda j, i: (i, 0)),
                  pl.BlockSpec((tm, kb), lambda j, i: (i, 0)),
                  pl.BlockSpec((ka, tn), lambda j, i: (0, j)),
                  pl.BlockSpec((kb, tn), lambda j, i: (0, j)),
                  pl.BlockSpec((tm, tn), lambda j, i: (i, ga_blk + j)),
                  pl.BlockSpec((tm, tn), lambda j, i: (i, gb_blk + j))],
        out_specs=pl.BlockSpec((tm, tn), lambda j, i: (i, j)),
        out_shape=jax.ShapeDtypeStruct((m, d_model), BF16),
        scratch_shapes=[pltpu.VMEM((ka, tn), BF16), pltpu.VMEM((kb, tn), BF16)],
        compiler_params=_cparams(("arbitrary", "arbitrary")),
        name="merge",
    )(hm, hs, w_a, w_b, seg_b, seg_b)


def _xattn_kernel(q_ref, k_ref, v_ref, o_ref, *, scale):
    s = lax.dot_general(q_ref[...], k_ref[...], (((1,), (1,)), ((), ())),
                        preferred_element_type=F32) * scale
    p = jnp.exp(s - jnp.max(s, axis=1, keepdims=True))
    p = p / jnp.sum(p, axis=1, keepdims=True)
    o_ref[...] = jnp.dot(p.astype(BF16), v_ref[...], preferred_element_type=F32).astype(o_ref.dtype)


def _xattn(q, kv, tq=2048):
    b, s, d = q.shape
    n_mem = kv.shape[1]
    hd = d // XA_HEADS
    return pl.pallas_call(
        functools.partial(_xattn_kernel, scale=1.0 / math.sqrt(hd)),
        grid=(b, XA_HEADS, s // tq),
        in_specs=[pl.BlockSpec((None, tq, hd), lambda bi, hi, i: (bi, i, hi)),
                  pl.BlockSpec((None, n_mem, hd), lambda bi, hi, i: (bi, 0, hi)),
                  pl.BlockSpec((None, n_mem, hd), lambda bi, hi, i: (bi, 0, XA_HEADS + hi))],
        out_specs=pl.BlockSpec((None, tq, hd), lambda bi, hi, i: (bi, i, hi)),
        out_shape=jax.ShapeDtypeStruct((b, s, d), BF16),
        compiler_params=_cparams(("arbitrary", "arbitrary", "arbitrary")),
        name="xattn",
    )(q, kv, kv)


def _router_kernel(h_ref, g_ref, wr_ref, br_ref, hn_ref, id_ref, wt_ref, wh_ref, wl_ref):
    @pl.when(pl.program_id(0) == 0)
    def _():
        w = wr_ref[...]
        wh = w.astype(BF16)
        wh_ref[...] = wh
        wl_ref[...] = (w - wh.astype(F32)).astype(BF16)

    x = h_ref[...]
    hn = (x * lax.rsqrt(jnp.mean(x * x, axis=-1, keepdims=True) + EPS)) * g_ref[...]
    tk = hn.shape[1] // MOE_KS
    for c in range(MOE_KS):
        hn_ref[:, c * (tk // 2):(c + 1) * (tk // 2)] = _pack_bf16_pair(
            hn[:, c * tk:c * tk + tk // 2], hn[:, c * tk + tk // 2:(c + 1) * tk])
    xh = hn.astype(BF16)
    xl = (hn - xh.astype(F32)).astype(BF16)
    logits = (jnp.dot(xh, wh_ref[...], preferred_element_type=F32)
              + (jnp.dot(xh, wl_ref[...], preferred_element_type=F32)
                 + jnp.dot(xl, wh_ref[...], preferred_element_type=F32))) + br_ref[...]

    tm = logits.shape[0]
    lane = lax.broadcasted_iota(I32, (tm, LANES), 1)
    lane_f = lane.astype(F32)
    ninf = -jnp.inf

    def first_lane_of(v, vmax):
        return jnp.min(jnp.where(v == vmax, lane_f, float(LANES)), axis=1, keepdims=True)

    gl = jnp.where(lane < N_GROUPS, logits, ninf)
    gmax = jnp.max(gl, axis=1, keepdims=True)
    g_w = 1.0 / jnp.sum(jnp.exp(gl - gmax), axis=1, keepdims=True)
    g_idx = first_lane_of(gl, gmax)
    lo = float(N_GROUPS) + g_idx * float(EXPERTS_PER_GROUP)
    in_group = (lane_f >= lo) & (lane_f < lo + float(EXPERTS_PER_GROUP))
    el = jnp.where(in_group, logits, ninf)
    m1 = jnp.max(el, axis=1, keepdims=True)
    i1 = first_lane_of(el, m1)
    el2 = jnp.where(lane_f == i1, ninf, el)
    m2 = jnp.max(el2, axis=1, keepdims=True)
    i2 = first_lane_of(el2, m2)
    denom = jnp.sum(jnp.exp(el - m1), axis=1, keepdims=True)
    p1 = 1.0 / denom
    p2 = jnp.exp(m2 - m1) / denom
    psum = p1 + p2
    w1 = g_w * (p1 / psum)
    w2 = g_w * (p2 / psum)
    ids = jnp.where(lane == 0, i1 - float(N_GROUPS), jnp.where(lane == 1, i2 - float(N_GROUPS), 0.0))
    id_ref[...] = ids.astype(I32)
    wt_ref[...] = jnp.where(lane == 0, w1, jnp.where(lane == 1, w2, 0.0))


def _router(h, g, w_r, b_r):
    t, d = h.shape
    tm = ROUTE_TM
    return pl.pallas_call(
        _router_kernel,
        grid=(t // tm,),
        in_specs=[pl.BlockSpec((tm, d), lambda i: (i, 0)),
                  pl.BlockSpec((1, d), lambda i: (0, 0)),
                  pl.BlockSpec((d, LANES), lambda i: (0, 0)),
                  pl.BlockSpec((1, LANES), lambda i: (0, 0))],
        out_specs=[pl.BlockSpec((tm, d // 2), lambda i: (i, 0)),
                   pl.BlockSpec((tm, LANES), lambda i: (i, 0)),
                   pl.BlockSpec((tm, LANES), lambda i: (i, 0))],
        out_shape=[jax.ShapeDtypeStruct((t, d // 2), U32),
                   jax.ShapeDtypeStruct((t, LANES), I32),
                   jax.ShapeDtypeStruct((t, LANES), F32)],
        scratch_shapes=[pltpu.VMEM((d, LANES), BF16), pltpu.VMEM((d, LANES), BF16)],
        compiler_params=_cparams(("arbitrary",)),
        name="router",
    )(h, g.reshape(1, d).astype(F32), w_r, b_r)


def _lane_cumsum(x):
    lane = lax.broadcasted_iota(I32, x.shape, 1)
    s = 1
    while s < LANES:
        x = x + jnp.where(lane >= s, pltpu.roll(x, s, axis=1), 0.0)
        s *= 2
    return x


def _meta_kernel(ids_ref, dest_ref, blk_ref, misc_ref, rank_s, *, n_tok, n_blocks):
    tb = META_TB
    lane = lax.broadcasted_iota(I32, (tb, LANES), 1)
    lower = jnp.where(lax.broadcasted_iota(I32, (tb, tb), 0) > lax.broadcasted_iota(I32, (tb, tb), 1),
                      1.0, 0.0).astype(BF16)

    def onehots(b):
        ids = ids_ref[pl.ds(pl.multiple_of(b * tb, tb), tb), :]
        return lane == ids[:, 0:1], lane == ids[:, 1:2]

    def lanes01(v0, v1):
        return jnp.where(lane == 0, v0, jnp.where(lane == 1, v1, 0.0))

    def count(b, carry):
        o1, o2 = onehots(b)
        cnt = jnp.where(o1, 1.0, 0.0) + jnp.where(o2, 1.0, 0.0)
        before = jnp.dot(lower, cnt.astype(BF16), preferred_element_type=F32) + carry
        r1 = jnp.sum(jnp.where(o1, before, 0.0), axis=1, keepdims=True)
        r2 = jnp.sum(jnp.where(o2, before, 0.0), axis=1, keepdims=True)
        rank_s[pl.ds(pl.multiple_of(b * tb, tb), tb), :] = lanes01(r1, r2)
        return carry + jnp.sum(cnt, axis=0, keepdims=True)

    counts = lax.fori_loop(0, n_tok // tb, count, jnp.zeros((1, LANES), F32))

    cnt8 = jnp.broadcast_to(counts, (8, LANES))
    whole = jnp.floor(cnt8 * (1.0 / MOE_ALIGN)) * MOE_ALIGN
    seg = jnp.where(cnt8 > whole, whole + MOE_ALIGN, whole)
    seg_start = _lane_cumsum(seg) - seg
    nblk = jnp.floor((cnt8 + (MOE_SBLK - 1)) * (1.0 / MOE_SBLK))
    blk_end = _lane_cumsum(nblk)
    blk_start = blk_end - nblk
    row_start = seg_start[0:1, :]

    def place(b, _):
        o1, o2 = onehots(b)
        s1 = jnp.sum(jnp.where(o1, row_start, 0.0), axis=1, keepdims=True)
        s2 = jnp.sum(jnp.where(o2, row_start, 0.0), axis=1, keepdims=True)
        sl = pl.ds(pl.multiple_of(b * tb, tb), tb)
        dest_ref[sl, :] = (rank_s[sl, :] + lanes01(s1, s2)).astype(I32)
        return 0

    lax.fori_loop(0, n_tok // tb, place, 0)

    step = lax.broadcasted_iota(I32, (n_blocks, LANES), 0).astype(F32)
    elane = lax.broadcasted_iota(I32, (n_blocks, LANES), 1)
    mine = (elane < N_EXPERTS) & (blk_start[0:1, :] <= step) & (step < blk_end[0:1, :])
    done_rows = (step - blk_start[0:1, :]) * float(MOE_SBLK)

    def pick(v):
        return jnp.sum(jnp.where(mine, v, 0.0), axis=1, keepdims=True)

    s_exp = pick(elane.astype(F32))
    s_row = pick(row_start + done_rows)
    s_val = pick(jnp.minimum(counts - done_rows, float(MOE_SBLK)))
    blk_ref[...] = jnp.where(elane == 0, s_exp, jnp.where(elane == 1, s_row, jnp.where(
        elane == 2, s_val, 0.0))).astype(I32)

    lane8 = lax.broadcasted_iota(I32, (8, LANES), 1)
    sub8 = lax.broadcasted_iota(I32, (8, LANES), 0)
    pad_row = jnp.where((cnt8 > whole) & (lane8 < N_EXPERTS), seg_start + whole, -1.0)
    n_used = jnp.broadcast_to(blk_end[:, N_EXPERTS - 1:N_EXPERTS], (8, LANES))
    misc_ref[...] = jnp.where(sub8 == 0, pad_row, jnp.where(sub8 == 1, n_used, 0.0)).astype(I32)


def _moe_meta(ids, n_blocks):
    t = ids.shape[0]
    assert MOE_ALIGN & (MOE_ALIGN - 1) == 0 and MOE_SBLK & (MOE_SBLK - 1) == 0
    return pl.pallas_call(
        functools.partial(_meta_kernel, n_tok=t, n_blocks=n_blocks),
        out_shape=[jax.ShapeDtypeStruct((t, LANES), I32),
                   jax.ShapeDtypeStruct((n_blocks, LANES), I32),
                   jax.ShapeDtypeStruct((8, LANES), I32)],
        scratch_shapes=[pltpu.VMEM((t, LANES), F32)],
        compiler_params=pltpu.CompilerParams(vmem_limit_bytes=VMEM_LIMIT),
        name="moe_meta",
    )(ids)


def _dispatch_kernel(dest_ref, misc_ref, x_ref, xs_ref, zbuf, sem, *, n_assign):
    tb = x_ref.shape[0]
    zrows = zbuf.shape[0]

    @pl.when(pl.program_id(0) == 0)
    def _():
        zbuf[...] = jnp.zeros_like(zbuf)

        def tail_copy(j):
            return pltpu.make_async_copy(zbuf, xs_ref.at[pl.ds(n_assign + j * zrows, zrows)], sem)

        n_tail = (xs_ref.shape[0] - n_assign) // zrows
        for j in range(n_tail):
            tail_copy(j).start()
        for j in range(n_tail):
            tail_copy(j).wait()

        def pad_copy(row):
            return pltpu.make_async_copy(zbuf.at[pl.ds(0, MOE_ALIGN)],
                                         xs_ref.at[pl.ds(row, MOE_ALIGN)], sem)

        def pads(fn):
            def body(e, _):
                row = misc_ref[0, e]

                @pl.when(row >= 0)
                def _():
                    fn(pad_copy(pl.multiple_of(row, MOE_ALIGN)))
                return 0
            lax.fori_loop(0, N_EXPERTS, body, 0)

        pads(lambda c: c.start())
        pads(lambda c: c.wait())

    def row_copy(r, d):
        return pltpu.make_async_copy(x_ref.at[pl.ds(r, 1)], xs_ref.at[pl.ds(d, 1)], sem)

    def scatters(fn):
        def body(g, _):
            r0 = pl.multiple_of(g * SUBLANES, SUBLANES)
            for q in range(SUBLANES):
                for j in range(2):
                    fn(row_copy(r0 + q, dest_ref[0, 0, 2 * (r0 + q) + j]))
            return 0
        lax.fori_loop(0, tb // SUBLANES, body, 0)

    scatters(lambda c: c.start())
    scatters(lambda c: c.wait())


def _dispatch(dest_blocks, misc, hn, xs_rows):
    t, d = hn.shape
    tb = DISPATCH_TB
    n_assign = 2 * t
    assert (xs_rows - n_assign) % MOE_SUB == 0
    return pl.pallas_call(
        functools.partial(_dispatch_kernel, n_assign=n_assign),
        grid=(t // tb,),
        in_specs=[pl.BlockSpec((1, 1, 2 * tb), lambda i: (i, 0, 0), memory_space=pltpu.SMEM),
                  pl.BlockSpec(memory_space=pltpu.SMEM),
                  pl.BlockSpec((tb, d), lambda i: (i, 0))],
        out_specs=pl.BlockSpec(memory_space=pl.ANY),
        out_shape=jax.ShapeDtypeStruct((xs_rows, d), hn.dtype),
        scratch_shapes=[pltpu.VMEM((MOE_SUB, d), hn.dtype), pltpu.SemaphoreType.DMA(())],
        compiler_params=_cparams(("arbitrary",)),
        name="moe_dispatch",
    )(dest_blocks, misc, hn)


def _expert_kernel(be_ref, rs_ref, nv_ref, nu_ref, xs_ref, wg_ref, wu_ref, wd_ref, ys_ref,
                   gacc, uacc, wg16, wu16, wd16, ybuf, sem, *, n_assign):
    del be_ref
    i = pl.program_id(0)
    k = pl.program_id(1)
    n_used = nu_ref[0]
    sub_shift = MOE_SUB.bit_length() - 1

    def out_copies(j, fn):
        row0 = rs_ref[j]
        rows = ((nv_ref[j] + (MOE_ALIGN - 1)) // MOE_ALIGN) * MOE_ALIGN
        n_full = rows >> sub_shift
        rem = rows & (MOE_SUB - 1)

        def piece_copy(off, size):
            src_off = off if isinstance(off, int) else pl.multiple_of(off, MOE_ALIGN)
            return pltpu.make_async_copy(
                ybuf.at[pl.ds(src_off, size)],
                ys_ref.at[pl.ds(pl.multiple_of(row0 + off, MOE_ALIGN), size)], sem)

        for s in range(MOE_SBLK // MOE_SUB):
            @pl.when(s < n_full)
            def _():
                fn(piece_copy(s * MOE_SUB, MOE_SUB))
        piece = MOE_SUB // 2
        while piece >= MOE_ALIGN:
            off = n_full * MOE_SUB + (rem & (MOE_SUB - 2 * piece))

            @pl.when((rem & piece) != 0)
            def _():
                fn(piece_copy(off, piece))
            piece //= 2

    @pl.when((i == 0) & (k == 0))
    def _():
        ybuf[...] = jnp.zeros_like(ybuf)

        def tail_copy(j):
            return pltpu.make_async_copy(ybuf, ys_ref.at[pl.ds(n_assign + j * MOE_SBLK, MOE_SBLK)], sem)

        n_tail = (ys_ref.shape[0] - n_assign) // MOE_SBLK
        for j in range(n_tail):
            tail_copy(j).start()
        for j in range(n_tail):
            tail_copy(j).wait()

    @pl.when(i < n_used)
    def _():
        n_sub = (nv_ref[i] + (MOE_SUB - 1)) >> sub_shift
        wg16[...] = wg_ref[...].astype(BF16)
        wu16[...] = wu_ref[...].astype(BF16)

        @pl.when(k == 0)
        def _():
            gacc[...] = jnp.zeros_like(gacc)
            uacc[...] = jnp.zeros_like(uacc)

        def up(r, _):
            rows = pl.ds(pl.multiple_of(r * MOE_SUB, MOE_SUB), MOE_SUB)
            x = jnp.concatenate(_unpack_bf16_pair(xs_ref[rows, :]), axis=1).astype(BF16)
            gacc[rows, :] += jnp.dot(x, wg16[...], preferred_element_type=F32)
            uacc[rows, :] += jnp.dot(x, wu16[...], preferred_element_type=F32)
            return 0

        lax.fori_loop(0, n_sub, up, 0)

        @pl.when(k == MOE_KS - 1)
        def _():
            wd16[...] = wd_ref[...].astype(BF16)

            @pl.when(i > 0)
            def _():
                out_copies(i - 1, lambda c: c.wait())

            def down(r, _):
                rows = pl.ds(pl.multiple_of(r * MOE_SUB, MOE_SUB), MOE_SUB)
                g = gacc[rows, :]
                hb = (g * _sigmoid(g)) * uacc[rows, :]
                y = jnp.dot(hb.astype(BF16), wd16[...], preferred_element_type=F32)
                half = y.shape[1] // 2
                ybuf[rows, :] = _pack_bf16_pair(y[:, :half], y[:, half:])
                return 0

            lax.fori_loop(0, n_sub, down, 0)
            out_copies(i, lambda c: c.start())

            @pl.when(i == n_used - 1)
            def _():
                out_copies(i, lambda c: c.wait())


def _experts(step_expert, step_row, step_valid, n_used, xs, w_gate, w_up, w_down, n_assign):
    xs_rows = xs.shape[0]
    n_blocks = step_expert.shape[0]
    d, f = w_gate.shape[-2:]
    tk = d // MOE_KS
    assert (xs_rows - n_assign) % MOE_SBLK == 0 and xs.shape[1] == d // 2

    def blk(i, nu):
        return jnp.minimum(i, nu[0] - 1)

    def kk(i, k, nu):
        return jnp.where(i < nu[0], k, MOE_KS - 1)

    grid_spec = pltpu.PrefetchScalarGridSpec(
        num_scalar_prefetch=4,
        grid=(n_blocks, MOE_KS),
        in_specs=[
            pl.BlockSpec((pl.Element(MOE_SBLK), pl.Element(tk // 2)),
                         lambda i, k, be, rs, nv, nu: (pl.multiple_of(rs[blk(i, nu)], MOE_ALIGN),
                                                       kk(i, k, nu) * (tk // 2))),
            pl.BlockSpec((None, tk, f), lambda i, k, be, rs, nv, nu: (be[blk(i, nu)], kk(i, k, nu), 0)),
            pl.BlockSpec((None, tk, f), lambda i, k, be, rs, nv, nu: (be[blk(i, nu)], kk(i, k, nu), 0)),
            pl.BlockSpec((None, f, d), lambda i, k, be, rs, nv, nu: (be[blk(i, nu)], 0, 0)),
        ],
        out_specs=pl.BlockSpec(memory_space=pl.ANY),
        scratch_shapes=[pltpu.VMEM((MOE_SBLK, f), F32), pltpu.VMEM((MOE_SBLK, f), F32),
                        pltpu.VMEM((tk, f), BF16), pltpu.VMEM((tk, f), BF16),
                        pltpu.VMEM((f, d), BF16), pltpu.VMEM((MOE_SBLK, d // 2), U32),
                        pltpu.SemaphoreType.DMA(())],
    )
    return pl.pallas_call(
        functools.partial(_expert_kernel, n_assign=n_assign),
        grid_spec=grid_spec,
        out_shape=jax.ShapeDtypeStruct((xs_rows, d // 2), U32),
        compiler_params=_cparams(("arbitrary", "arbitrary")),
        name="moe_experts",
    )(step_expert, step_row, step_valid, n_used, xs, w_gate, w_up, w_down)


def _combine_kernel(dest_ref, dest_next_ref, w_ref, h_ref, g_ref, ys_ref, o_ref, ybuf, sems, *,
                    final_norm):
    tb = h_ref.shape[0]
    i = pl.program_id(0)
    slot = i % 2

    def gathers(rows_ref, s, fn):
        def body(g, _):
            r0 = pl.multiple_of(g * SUBLANES, SUBLANES)
            for q in range(SUBLANES):
                for j in range(2):
                    fn(pltpu.make_async_copy(ys_ref.at[pl.ds(rows_ref[0, 0, 2 * (r0 + q) + j], 1)],
                                             ybuf.at[s, j, pl.ds(r0 + q, 1)], sems.at[s]))
            return 0
        lax.fori_loop(0, tb // SUBLANES, body, 0)

    @pl.when(i == 0)
    def _():
        gathers(dest_ref, 0, lambda c: c.start())

    @pl.when(i + 1 < pl.num_programs(0))
    def _():
        gathers(dest_next_ref, 1 - slot, lambda c: c.start())

    gathers(dest_ref, slot, lambda c: c.wait())

    w = w_ref[...]
    lo0, hi0 = _unpack_bf16_pair(ybuf[slot, 0])
    lo1, hi1 = _unpack_bf16_pair(ybuf[slot, 1])
    w0, w1 = w[:, 0:1], w[:, 1:2]
    h = h_ref[...] + jnp.concatenate([w0 * lo0 + w1 * lo1, w0 * hi0 + w1 * hi1], axis=1)
    if final_norm:
        h = (h * lax.rsqrt(jnp.mean(h * h, axis=-1, keepdims=True) + EPS)) * g_ref[...]
    o_ref[...] = h


def _combine(dest_blocks, wts, h, g, ys, final_norm):
    t, d = h.shape
    tb = COMBINE_TB
    n_tiles = t // tb
    return pl.pallas_call(
        functools.partial(_combine_kernel, final_norm=final_norm),
        grid=(n_tiles,),
        in_specs=[pl.BlockSpec((1, 1, 2 * tb), lambda i: (i, 0, 0), memory_space=pltpu.SMEM),
                  pl.BlockSpec((1, 1, 2 * tb), lambda i: (jnp.minimum(i + 1, n_tiles - 1), 0, 0),
                               memory_space=pltpu.SMEM),
                  pl.BlockSpec((tb, LANES), lambda i: (i, 0)),
                  pl.BlockSpec((tb, d), lambda i: (i, 0)),
                  pl.BlockSpec((1, d), lambda i: (0, 0)),
                  pl.BlockSpec(memory_space=pl.ANY)],
        out_specs=pl.BlockSpec((tb, d), lambda i: (i, 0)),
        out_shape=jax.ShapeDtypeStruct((t, d), F32),
        scratch_shapes=[pltpu.VMEM((2, 2, tb, d // 2), U32), pltpu.SemaphoreType.DMA((2,))],
        compiler_params=_cparams(("arbitrary",)),
        name="moe_combine",
    )(dest_blocks, dest_blocks, wts, h, g.reshape(1, d).astype(F32), ys)


def kernel(x, mem, norm_mix, w_in, conv_qk, b_gates, g_mlstm, w_proj_a, w_proj_b, w_out,
           norm_xattn, norm_mem, w_q_mem, w_kv_mem, w_o_mem, norm_moe,
           w_router_group, b_router_group, w_router_expert, b_router_expert,
           w_gate, w_up, w_down, norm_final):
    b, s, d = x.shape
    t = b * s
    n_mem = mem.shape[1]
    depth = w_in.shape[0]
    ml_qk_w = ML_HEADS * ML_QK
    ml_v_w = ML_HEADS * ML_V
    sb_w = SB_HEADS * SB_HD
    seg_a_w = 2 * ml_qk_w + 2 * ml_v_w
    n_gate_cols = 2 * ML_HEADS
    seg_b_w = 3 * sb_w + 2 * d
    n_assign = 2 * t
    n_blocks = n_assign // MOE_SBLK + N_EXPERTS
    xs_rows = n_assign + N_EXPERTS * MOE_ALIGN + MOE_SBLK

    h = x.reshape(t, d)
    mem2 = mem.reshape(b * n_mem, d)
    for l in range(depth):
        xn = _rmsnorm(h, norm_mix[l], BF16)
        w_in_t = jnp.swapaxes(w_in[l], 0, 1)
        seg_a = _matmul(xn, w_in_t, seg_a_w, BF16, transposed=True, name="in_proj_a")
        gates = _matmul(xn, w_in_t, LANES, F32, tn=LANES, col_start=seg_a_w, transposed=True,
                        name="in_proj_gates")
        seg_b = _matmul(xn, w_in_t, seg_b_w, BF16, col_start=seg_a_w + n_gate_cols, transposed=True,
                        name="in_proj_b")
        gates_t = gates[:, :n_gate_cols].reshape(b, s, 2, ML_HEADS).transpose(0, 2, 3, 1)
        gates_t = gates_t.reshape(b, 2, ML_HEADS, s // ML_CHUNK, ML_CHUNK)
        hm = _mlstm(seg_a.reshape(b, s, seg_a_w), gates_t, conv_qk[l], b_gates[l], g_mlstm[l])
        hs = _stick_breaking(seg_b.reshape(b, s, seg_b_w))
        y = _merge(hm.reshape(t, ml_v_w), hs.reshape(t, sb_w), w_proj_a[l], w_proj_b[l],
                   seg_b, 3 * sb_w, 3 * sb_w + d, d)
        h = _matmul(y, w_out[l], d, F32, res=h, name="out_proj")
        hn = _rmsnorm(h, norm_xattn[l], BF16)
        q = _matmul(hn, w_q_mem[l], d, BF16, name="xattn_q")
        memn = _rmsnorm(mem2, norm_mem[l], BF16)
        kv = _matmul(memn, w_kv_mem[l], 2 * d, BF16, name="xattn_kv")
        o = _xattn(q.reshape(b, s, d), kv.reshape(b, n_mem, 2 * d))
        h = _matmul(o.reshape(t, d), w_o_mem[l], d, F32, res=h, name="xattn_o")
        w_r = jnp.pad(jnp.concatenate([w_router_group[l], w_router_expert[l]], axis=1),
                      ((0, 0), (0, LANES - N_GROUPS - N_EXPERTS)))
        b_r = jnp.pad(jnp.concatenate([b_router_group[l], b_router_expert[l]]),
                      (0, LANES - N_GROUPS - N_EXPERTS)).reshape(1, LANES).astype(F32)
        hn3, ids, wts = _router(h, norm_moe[l], w_r, b_r)
        dest, steps, misc = _moe_meta(ids, n_blocks)
        xs = _dispatch(dest[:, :2].reshape(t // DISPATCH_TB, 1, 2 * DISPATCH_TB), misc, hn3, xs_rows)
        ys = _experts(steps[:, 0], steps[:, 1], steps[:, 2], misc[1, :1], xs,
                      w_gate[l], w_up[l], w_down[l], n_assign)
        h = _combine(dest[:, :2].reshape(t // COMBINE_TB, 1, 2 * COMBINE_TB), wts, h,
                     norm_final, ys, final_norm=(l == depth - 1))
    return h.reshape(b, s, d)
```

```python
import functools
import math

import jax
import jax.numpy as jnp
from jax import lax
from jax.experimental import pallas as pl
from jax.experimental.pallas import tpu as pltpu

F32 = jnp.float32
BF16 = jnp.bfloat16
I32 = jnp.int32
U32 = jnp.uint32

EPS = 1e-6
ML_HEADS = 8
ML_QK = 128
ML_V = 256
ML_CHUNK = 128
CONV_W = 4
ML_HPS = 2
SB_HEADS = 16
SB_HD = 128
SB_BLOCK = 128
XA_HEADS = 4
N_GROUPS = 8
EXPERTS_PER_GROUP = 8
N_EXPERTS = N_GROUPS * EXPERTS_PER_GROUP
D_EXPERT = 512

LANES = 128
SUBLANES = 8
VMEM_LIMIT = 56 * 1024 * 1024

SB_TQ = 2048
SB_KB = 256
MM_TM = 1024
MM_TN = 512
MM_CAST_ROWS = 512
MOE_ALIGN = 8
MOE_SBLK = 512
MOE_SUB = 128
MOE_KS = 2
ROUTE_TM = 512
META_TB = 256
DISPATCH_TB = 512
COMBINE_TB = 256


def _cparams(sem, vmem=VMEM_LIMIT):
    return pltpu.CompilerParams(dimension_semantics=sem, vmem_limit_bytes=vmem)


def _sigmoid(x):
    return 1.0 / (1.0 + jnp.exp(-x))


def _pack_bf16_pair(lo, hi):
    lo_bits = lax.bitcast_convert_type(lo.astype(BF16).astype(F32), U32)
    hi_bits = lax.bitcast_convert_type(hi.astype(BF16).astype(F32), U32)
    return (lo_bits >> 16) | (hi_bits & jnp.uint32(0xFFFF0000))


def _unpack_bf16_pair(p):
    lo = lax.bitcast_convert_type(p << 16, F32)
    hi = lax.bitcast_convert_type(p & jnp.uint32(0xFFFF0000), F32)
    return lo, hi


def _neg_softplus(x):
    return -(jnp.maximum(x, 0.0) + jnp.log1p(jnp.exp(-jnp.abs(x))))


def _rmsnorm_kernel(x_ref, g_ref, o_ref):
    x = x_ref[...].astype(F32)
    ms = jnp.mean(x * x, axis=-1, keepdims=True)
    o_ref[...] = ((x * lax.rsqrt(ms + EPS)) * g_ref[...]).astype(o_ref.dtype)


def _rmsnorm(x, g, out_dtype, tm=512):
    m, d = x.shape
    return pl.pallas_call(
        _rmsnorm_kernel,
        grid=(m // tm,),
        in_specs=[pl.BlockSpec((tm, d), lambda i: (i, 0)),
                  pl.BlockSpec((1, d), lambda i: (0, 0))],
        out_specs=pl.BlockSpec((tm, d), lambda i: (i, 0)),
        out_shape=jax.ShapeDtypeStruct((m, d), out_dtype),
        compiler_params=_cparams(("arbitrary",)),
        name="rmsnorm",
    )(x, g.reshape(1, d).astype(F32))


def _mm_kernel(*refs, has_res, transposed):
    a_ref, w_ref = refs[0], refs[1]
    r_ref = refs[2] if has_res else None
    o_ref, w16_ref = refs[-2], refs[-1]
    k, tn = w16_ref.shape

    @pl.when(pl.program_id(1) == 0)
    def _():
        if transposed:
            for k0 in range(0, k, MM_CAST_ROWS):
                ks = pl.ds(k0, MM_CAST_ROWS)
                w16_ref[ks, :] = w_ref[:, ks].T.astype(BF16)
        else:
            w16_ref[...] = w_ref[...].astype(BF16)

    acc = jnp.dot(a_ref[...], w16_ref[...], preferred_element_type=F32)
    if has_res:
        acc = r_ref[...] + acc
    o_ref[...] = acc.astype(o_ref.dtype)


def _matmul(a, w, n_cols, out_dtype, res=None, tn=MM_TN, col_start=0, transposed=False,
            name="matmul"):
    m, k = a.shape
    tm = min(MM_TM, m)
    grid = (n_cols // tn, m // tm)
    if transposed:
        assert col_start % SUBLANES == 0
        w_spec = pl.BlockSpec((pl.Element(tn), pl.Element(k)),
                              lambda j, i: (pl.multiple_of(col_start + j * tn, SUBLANES), 0))
    else:
        base_blk, rem = divmod(col_start, tn)
        assert rem == 0
        w_spec = pl.BlockSpec((k, tn), lambda j, i: (0, base_blk + j))
    in_specs = [pl.BlockSpec((tm, k), lambda j, i: (i, 0)), w_spec]
    args = [a, w]
    if res is not None:
        in_specs.append(pl.BlockSpec((tm, tn), lambda j, i: (i, j)))
        args.append(res)
    return pl.pallas_call(
        functools.partial(_mm_kernel, has_res=res is not None, transposed=transposed),
        grid=grid,
        in_specs=in_specs,
        out_specs=pl.BlockSpec((tm, tn), lambda j, i: (i, j)),
        out_shape=jax.ShapeDtypeStruct((m, n_cols), out_dtype),
        scratch_shapes=[pltpu.VMEM((k, tn), BF16)],
        compiler_params=_cparams(("arbitrary", "arbitrary")),
        name=name,
    )(*args)


def _mlstm_kernel(bg_ref, q_ref, k_ref, v_ref, o_ref, gt_ref, cwq_ref, cwk_ref, gm_ref,
                  out_ref, qpad, kpad, bc_s, li_s, *, seq):
    head0 = pl.program_id(1) * ML_HPS
    n_chunks = seq // ML_CHUNK
    L = ML_CHUNK

    qpad[0:8, :] = jnp.zeros((8, ML_HPS * ML_QK), F32)
    kpad[0:8, :] = jnp.zeros((8, ML_HPS * ML_QK), F32)
    qpad[8:, :] = q_ref[...].astype(F32)
    kpad[8:, :] = k_ref[...].astype(F32)

    lane = lax.broadcasted_iota(I32, (n_chunks, L), 1)
    for hh in range(ML_HPS):
        li_s[hh] = gt_ref[0, hh] + bg_ref[head0 + hh]
        bc = _neg_softplus(-(gt_ref[1, hh] + bg_ref[ML_HEADS + head0 + hh]))
        for s in (1, 2, 4, 8, 16, 32, 64):
            bc = bc + jnp.where(lane >= s, pltpu.roll(bc, s, axis=1), 0.0)
        bc_s[hh] = bc

    rows = lax.broadcasted_iota(I32, (L, L), 0)
    cols = lax.broadcasted_iota(I32, (L, L), 1)
    eye = rows == cols
    causal = cols <= rows
    k_scale = 1.0 / math.sqrt(ML_QK)

    def to_col(row):
        return jnp.sum(jnp.where(eye, row, 0.0), axis=1, keepdims=True)

    def conv_silu(win, cw):
        y = (cw[0:1, :] * win[5:5 + L] + cw[1:2, :] * win[6:6 + L]
             + cw[2:3, :] * win[7:7 + L] + cw[3:4, :] * win[8:8 + L])
        return y * _sigmoid(y)

    def head_chunk(hh, c, r0, carry):
        c_st, n_st, m_st = carry
        qk_cols = pl.ds(hh * ML_QK, ML_QK)
        v_cols = pl.ds(hh * ML_V, ML_V)
        qb = conv_silu(qpad[pl.ds(r0, L + 8), qk_cols], cwq_ref[:, qk_cols])
        kb = conv_silu(kpad[pl.ds(r0, L + 8), qk_cols], cwk_ref[:, qk_cols]) * k_scale
        vb = v_ref[pl.ds(r0, L), v_cols]
        bc_row = bc_s[hh, pl.ds(c, 1), :]
        li_row = li_s[hh, pl.ds(c, 1), :]
        bc_col = to_col(bc_row)

        d = jnp.where(causal, bc_col - bc_row + li_row, -jnp.inf)
        inter = bc_col + m_st
        m_t = jnp.maximum(inter, jnp.max(d, axis=1, keepdims=True))
        w_intra = jnp.exp(d - m_t)
        w_inter = jnp.exp(inter - m_t)

        qb16 = qb.astype(BF16)
        kb16 = kb.astype(BF16)
        sc = lax.dot_general(qb16, kb16, (((1,), (1,)), ((), ())),
                             preferred_element_type=F32) * w_intra
        num = (w_inter * jnp.dot(qb16, c_st.astype(BF16), preferred_element_type=F32)
               + jnp.dot(sc.astype(BF16), vb, preferred_element_type=F32))
        den = (w_inter * jnp.sum(qb * n_st, axis=1, keepdims=True)
               + jnp.sum(sc, axis=1, keepdims=True))
        hval = num / jnp.maximum(jnp.abs(den), jnp.exp(-m_t))

        hn = hval * lax.rsqrt(jnp.mean(hval * hval, axis=1, keepdims=True) + EPS)
        og = _sigmoid(o_ref[pl.ds(r0, L), v_cols].astype(F32))
        out_ref[pl.ds(r0, L), v_cols] = (og * (hn * gm_ref[:, v_cols])).astype(out_ref.dtype)

        g = bc_row[:, L - 1:L]
        ds_row = g - bc_row + li_row
        m_new = jnp.maximum(g + m_st, jnp.max(ds_row, axis=1, keepdims=True))
        w_s = jnp.exp(ds_row - m_new)
        decay = jnp.exp(g + m_st - m_new)
        kw = kb * to_col(w_s)
        c_new = decay * c_st + jnp.dot(kw.T.astype(BF16), vb, preferred_element_type=F32)
        n_new = decay * n_st + jnp.sum(kw, axis=0, keepdims=True)
        return c_new, n_new, m_new

    def chunk(c, carry):
        r0 = pl.multiple_of(c * L, L)
        return tuple(head_chunk(hh, c, r0, carry[hh]) for hh in range(ML_HPS))

    init = (jnp.zeros((ML_QK, ML_V), F32), jnp.zeros((1, ML_QK), F32),
            jnp.full((1, 1), -1e30, F32))
    lax.fori_loop(0, n_chunks, chunk, (init,) * ML_HPS)


def _mlstm(seg_a, gates_t, conv_qk, b_gates, g_mlstm):
    b, s, _ = seg_a.shape
    h = ML_HEADS
    hp = ML_HPS
    n_chunks = s // ML_CHUNK
    groups = h // hp
    qk_w, v_w = hp * ML_QK, hp * ML_V
    return pl.pallas_call(
        functools.partial(_mlstm_kernel, seq=s),
        grid=(b, groups),
        in_specs=[
            pl.BlockSpec(memory_space=pltpu.SMEM),
            pl.BlockSpec((None, s, qk_w), lambda bi, gi: (bi, 0, gi)),
            pl.BlockSpec((None, s, qk_w), lambda bi, gi: (bi, 0, groups + gi)),
            pl.BlockSpec((None, s, v_w), lambda bi, gi: (bi, 0, groups + gi)),
            pl.BlockSpec((None, s, v_w), lambda bi, gi: (bi, 0, 2 * groups + gi)),
            pl.BlockSpec((None, 2, hp, n_chunks, ML_CHUNK), lambda bi, gi: (bi, 0, gi, 0, 0)),
            pl.BlockSpec((CONV_W, qk_w), lambda bi, gi: (0, gi)),
            pl.BlockSpec((CONV_W, qk_w), lambda bi, gi: (0, groups + gi)),
            pl.BlockSpec((1, v_w), lambda bi, gi: (0, gi)),
        ],
        out_specs=pl.BlockSpec((None, s, v_w), lambda bi, gi: (bi, 0, gi)),
        out_shape=jax.ShapeDtypeStruct((b, s, h * ML_V), BF16),
        scratch_shapes=[pltpu.VMEM((s + 8, qk_w), F32), pltpu.VMEM((s + 8, qk_w), F32),
                        pltpu.VMEM((hp, n_chunks, ML_CHUNK), F32),
                        pltpu.VMEM((hp, n_chunks, ML_CHUNK), F32)],
        compiler_params=_cparams(("arbitrary", "arbitrary")),
        name="mlstm",
    )(b_gates.astype(F32), seg_a, seg_a, seg_a, seg_a, gates_t, conv_qk.astype(F32),
      conv_qk.astype(F32), g_mlstm.reshape(1, -1).astype(F32))


def _sb_kernel(q_ref, k_ref, v_ref, o_ref, acc_s, r_s, *, seq):
    L = SB_BLOCK
    tq, kb = SB_TQ, SB_KB
    scale = 1.0 / math.sqrt(SB_HD)
    rows = lax.broadcasted_iota(I32, (tq, kb), 0)
    cols = lax.broadcasted_iota(I32, (tq, kb), 1)
    ur = lax.broadcasted_iota(I32, (L, 2 * L), 0)
    uc = lax.broadcasted_iota(I32, (L, 2 * L), 1)
    u = jnp.where((uc >= L) | (ur > uc), 1.0, 0.0).astype(BF16)

    def step(qi, c0, mask, row_lo=0):
        kj = k_ref[pl.ds(c0, kb), :]
        vj = v_ref[pl.ds(c0, kb), :]
        tile_rows = pl.ds(row_lo, tq - row_lo)
        z = lax.dot_general(qi[row_lo:], kj, (((1,), (1,)), ((), ())),
                            preferred_element_type=F32) * scale
        l1m = -(jnp.maximum(z, 0.0) + jnp.log(1.0 + jnp.exp(-jnp.abs(z))))
        lm = l1m if mask is None else jnp.where(mask, l1m, 0.0)
        lm16 = lm.astype(BF16)
        cs_far = jnp.dot(lm16[:, :L], u, preferred_element_type=F32)
        cs_near = jnp.dot(lm16[:, L:], u, preferred_element_type=F32)
        r0 = r_s[tile_rows, :]
        r1 = r0 + cs_near[:, L:]
        rest = jnp.concatenate([cs_far[:, :L] + r1, cs_near[:, :L] + r0], axis=1)
        a = jnp.exp((z + l1m) + rest)
        if mask is not None:
            a = jnp.where(mask, a, 0.0)
        acc_s[tile_rows, :] += jnp.dot(a.astype(BF16), vj, preferred_element_type=F32)
        r_s[tile_rows, :] = r1 + cs_far[:, L:]

    def qtile(t, _):
        q0 = pl.multiple_of(t * tq, tq)
        qi = q_ref[pl.ds(q0, tq), :]
        acc_s[...] = jnp.zeros_like(acc_s)
        r_s[...] = jnp.zeros_like(r_s)
        for off in range(tq - kb, -1, -kb):
            step(qi, pl.multiple_of(q0 + off, kb), (cols + off < rows)[off:], row_lo=off)

        def earlier(jj, _):
            c0 = q0 - (jj + 1) * (2 * kb)
            step(qi, pl.multiple_of(c0 + kb, kb), None)
            step(qi, pl.multiple_of(c0, kb), None)
            return 0

        lax.fori_loop(0, t * (tq // (2 * kb)), earlier, 0)
        o_ref[pl.ds(q0, tq), :] = acc_s[...].astype(o_ref.dtype)
        return 0

    lax.fori_loop(0, seq // tq, qtile, 0)


def _stick_breaking(seg_b):
    b, s, _ = seg_b.shape
    h = SB_HEADS
    assert SB_KB == 2 * SB_BLOCK and SB_TQ % (2 * SB_KB) == 0 and s % SB_TQ == 0
    return pl.pallas_call(
        functools.partial(_sb_kernel, seq=s),
        grid=(b, h),
        in_specs=[pl.BlockSpec((None, s, SB_HD), lambda bi, hi: (bi, 0, hi)),
                  pl.BlockSpec((None, s, SB_HD), lambda bi, hi: (bi, 0, h + hi)),
                  pl.BlockSpec((None, s, SB_HD), lambda bi, hi: (bi, 0, 2 * h + hi))],
        out_specs=pl.BlockSpec((None, s, SB_HD), lambda bi, hi: (bi, 0, hi)),
        out_shape=jax.ShapeDtypeStruct((b, s, h * SB_HD), BF16),
        scratch_shapes=[pltpu.VMEM((SB_TQ, SB_HD), F32), pltpu.VMEM((SB_TQ, SB_BLOCK), F32)],
        compiler_params=_cparams(("arbitrary", "arbitrary")),
        name="stick_breaking",
    )(seg_b, seg_b, seg_b)


def _merge_kernel(hm_ref, hs_ref, wa_ref, wb_ref, ga_ref, gb_ref, o_ref, wa16, wb16):
    @pl.when(pl.program_id(1) == 0)
    def _():
        wa16[...] = wa_ref[...].astype(BF16)
        wb16[...] = wb_ref[...].astype(BF16)

    ya = jnp.dot(hm_ref[...], wa16[...], preferred_element_type=F32)
    yb = jnp.dot(hs_ref[...], wb16[...], preferred_element_type=F32)
    y = _sigmoid(ga_ref[...].astype(F32)) * ya + _sigmoid(gb_ref[...].astype(F32)) * yb
    o_ref[...] = y.astype(o_ref.dtype)


def _merge(hm, hs, w_a, w_b, seg_b, gate_a_col, gate_b_col, d_model):
    m, ka = hm.shape
    kb = hs.shape[1]
    tm, tn = MM_TM, MM_TN
    ga_blk = gate_a_col // tn
    gb_blk = gate_b_col // tn
    return pl.pallas_call(
        _merge_kernel,
        grid=(d_model // tn, m // tm),
        in_specs=[pl.BlockSpec((tm, ka), lambda j, i: (i, 0)),
                  pl.BlockSpec((tm, kb), lambda j, i: (i, 0)),
                  pl.BlockSpec((ka, tn), lambda j, i: (0, j)),
                  pl.BlockSpec((kb, tn), lambda j, i: (0, j)),
                  pl.BlockSpec((tm, tn), lambda j, i: (i, ga_blk + j)),
                  pl.BlockSpec((tm, tn), lambda j, i: (i, gb_blk + j))],
        out_specs=pl.BlockSpec((tm, tn), lambda j, i: (i, j)),
        out_shape=jax.ShapeDtypeStruct((m, d_model), BF16),
        scratch_shapes=[pltpu.VMEM((ka, tn), BF16), pltpu.VMEM((kb, tn), BF16)],
        compiler_params=_cparams(("arbitrary", "arbitrary")),
        name="merge",
    )(hm, hs, w_a, w_b, seg_b, seg_b)


def _xattn_kernel(q_ref, k_ref, v_ref, o_ref, *, scale):
    s = lax.dot_general(q_ref[...], k_ref[...], (((1,), (1,)), ((), ())),
                        preferred_element_type=F32) * scale
    p = jnp.exp(s - jnp.max(s, axis=1, keepdims=True))
    p = p / jnp.sum(p, axis=1, keepdims=True)
    o_ref[...] = jnp.dot(p.astype(BF16), v_ref[...], preferred_element_type=F32).astype(o_ref.dtype)


def _xattn(q, kv, tq=2048):
    b, s, d = q.shape
    n_mem = kv.shape[1]
    hd = d // XA_HEADS
    return pl.pallas_call(
        functools.partial(_xattn_kernel, scale=1.0 / math.sqrt(hd)),
        grid=(b, XA_HEADS, s // tq),
        in_specs=[pl.BlockSpec((None, tq, hd), lambda bi, hi, i: (bi, i, hi)),
                  pl.BlockSpec((None, n_mem, hd), lambda bi, hi, i: (bi, 0, hi)),
                  pl.BlockSpec((None, n_mem, hd), lambda bi, hi, i: (bi, 0, XA_HEADS + hi))],
        out_specs=pl.BlockSpec((None, tq, hd), lambda bi, hi, i: (bi, i, hi)),
        out_shape=jax.ShapeDtypeStruct((b, s, d), BF16),
        compiler_params=_cparams(("arbitrary", "arbitrary", "arbitrary")),
        name="xattn",
    )(q, kv, kv)


def _router_kernel(h_ref, g_ref, wr_ref, br_ref, hn_ref, id_ref, wt_ref, wh_ref, wl_ref):
    @pl.when(pl.program_id(0) == 0)
    def _():
        w = wr_ref[...]
        wh = w.astype(BF16)
        wh_ref[...] = wh
        wl_ref[...] = (w - wh.astype(F32)).astype(BF16)

    x = h_ref[...]
    hn = (x * lax.rsqrt(jnp.mean(x * x, axis=-1, keepdims=True) + EPS)) * g_ref[...]
    tk = hn.shape[1] // MOE_KS
    for c in range(MOE_KS):
        hn_ref[:, c * (tk // 2):(c + 1) * (tk // 2)] = _pack_bf16_pair(
            hn[:, c * tk:c * tk + tk // 2], hn[:, c * tk + tk // 2:(c + 1) * tk])
    xh = hn.astype(BF16)
    xl = (hn - xh.astype(F32)).astype(BF16)
    logits = (jnp.dot(xh, wh_ref[...], preferred_element_type=F32)
              + (jnp.dot(xh, wl_ref[...], preferred_element_type=F32)
                 + jnp.dot(xl, wh_ref[...], preferred_element_type=F32))) + br_ref[...]

    tm = logits.shape[0]
    lane = lax.broadcasted_iota(I32, (tm, LANES), 1)
    lane_f = lane.astype(F32)
    ninf = -jnp.inf

    def first_lane_of(v, vmax):
        return jnp.min(jnp.where(v == vmax, lane_f, float(LANES)), axis=1, keepdims=True)

    gl = jnp.where(lane < N_GROUPS, logits, ninf)
    gmax = jnp.max(gl, axis=1, keepdims=True)
    g_w = 1.0 / jnp.sum(jnp.exp(gl - gmax), axis=1, keepdims=True)
    g_idx = first_lane_of(gl, gmax)
    lo = float(N_GROUPS) + g_idx * float(EXPERTS_PER_GROUP)
    in_group = (lane_f >= lo) & (lane_f < lo + float(EXPERTS_PER_GROUP))
    el = jnp.where(in_group, logits, ninf)
    m1 = jnp.max(el, axis=1, keepdims=True)
    i1 = first_lane_of(el, m1)
    el2 = jnp.where(lane_f == i1, ninf, el)
    m2 = jnp.max(el2, axis=1, keepdims=True)
    i2 = first_lane_of(el2, m2)
    denom = jnp.sum(jnp.exp(el - m1), axis=1, keepdims=True)
    p1 = 1.0 / denom
    p2 = jnp.exp(m2 - m1) / denom
    psum = p1 + p2
    w1 = g_w * (p1 / psum)
    w2 = g_w * (p2 / psum)
    ids = jnp.where(lane == 0, i1 - float(N_GROUPS), jnp.where(lane == 1, i2 - float(N_GROUPS), 0.0))
    id_ref[...] = ids.astype(I32)
    wt_ref[...] = jnp.where(lane == 0, w1, jnp.where(lane == 1, w2, 0.0))


def _router(h, g, w_r, b_r):
    t, d = h.shape
    tm = ROUTE_TM
    return pl.pallas_call(
        _router_kernel,
        grid=(t // tm,),
        in_specs=[pl.BlockSpec((tm, d), lambda i: (i, 0)),
                  pl.BlockSpec((1, d), lambda i: (0, 0)),
                  pl.BlockSpec((d, LANES), lambda i: (0, 0)),
                  pl.BlockSpec((1, LANES), lambda i: (0, 0))],
        out_specs=[pl.BlockSpec((tm, d // 2), lambda i: (i, 0)),
                   pl.BlockSpec((tm, LANES), lambda i: (i, 0)),
                   pl.BlockSpec((tm, LANES), lambda i: (i, 0))],
        out_shape=[jax.ShapeDtypeStruct((t, d // 2), U32),
                   jax.ShapeDtypeStruct((t, LANES), I32),
                   jax.ShapeDtypeStruct((t, LANES), F32)],
        scratch_shapes=[pltpu.VMEM((d, LANES), BF16), pltpu.VMEM((d, LANES), BF16)],
        compiler_params=_cparams(("arbitrary",)),
        name="router",
    )(h, g.reshape(1, d).astype(F32), w_r, b_r)


def _lane_cumsum(x):
    lane = lax.broadcasted_iota(I32, x.shape, 1)
    s = 1
    while s < LANES:
        x = x + jnp.where(lane >= s, pltpu.roll(x, s, axis=1), 0.0)
        s *= 2
    return x


def _meta_kernel(ids_ref, dest_ref, blk_ref, misc_ref, rank_s, *, n_tok, n_blocks):
    tb = META_TB
    lane = lax.broadcasted_iota(I32, (tb, LANES), 1)
    lower = jnp.where(lax.broadcasted_iota(I32, (tb, tb), 0) > lax.broadcasted_iota(I32, (tb, tb), 1),
                      1.0, 0.0).astype(BF16)

    def onehots(b):
        ids = ids_ref[pl.ds(pl.multiple_of(b * tb, tb), tb), :]
        return lane == ids[:, 0:1], lane == ids[:, 1:2]

    def lanes01(v0, v1):
        return jnp.where(lane == 0, v0, jnp.where(lane == 1, v1, 0.0))

    def count(b, carry):
        o1, o2 = onehots(b)
        cnt = jnp.where(o1, 1.0, 0.0) + jnp.where(o2, 1.0, 0.0)
        before = jnp.dot(lower, cnt.astype(BF16), preferred_element_type=F32) + carry
        r1 = jnp.sum(jnp.where(o1, before, 0.0), axis=1, keepdims=True)
        r2 = jnp.sum(jnp.where(o2, before, 0.0), axis=1, keepdims=True)
        rank_s[pl.ds(pl.multiple_of(b * tb, tb), tb), :] = lanes01(r1, r2)
        return carry + jnp.sum(cnt, axis=0, keepdims=True)

    counts = lax.fori_loop(0, n_tok // tb, count, jnp.zeros((1, LANES), F32))

    cnt8 = jnp.broadcast_to(counts, (8, LANES))
    whole = jnp.floor(cnt8 * (1.0 / MOE_ALIGN)) * MOE_ALIGN
    seg = jnp.where(cnt8 > whole, whole + MOE_ALIGN, whole)
    seg_start = _lane_cumsum(seg) - seg
    nblk = jnp.floor((cnt8 + (MOE_SBLK - 1)) * (1.0 / MOE_SBLK))
    blk_end = _lane_cumsum(nblk)
    blk_start = blk_end - nblk
    row_start = seg_start[0:1, :]

    def place(b, _):
        o1, o2 = onehots(b)
        s1 = jnp.sum(jnp.where(o1, row_start, 0.0), axis=1, keepdims=True)
        s2 = jnp.sum(jnp.where(o2, row_start, 0.0), axis=1, keepdims=True)
        sl = pl.ds(pl.multiple_of(b * tb, tb), tb)
        dest_ref[sl, :] = (rank_s[sl, :] + lanes01(s1, s2)).astype(I32)
        return 0

    lax.fori_loop(0, n_tok // tb, place, 0)

    step = lax.broadcasted_iota(I32, (n_blocks, LANES), 0).astype(F32)
    elane = lax.broadcasted_iota(I32, (n_blocks, LANES), 1)
    mine = (elane < N_EXPERTS) & (blk_start[0:1, :] <= step) & (step < blk_end[0:1, :])
    done_rows = (step - blk_start[0:1, :]) * float(MOE_SBLK)

    def pick(v):
        return jnp.sum(jnp.where(mine, v, 0.0), axis=1, keepdims=True)

    s_exp = pick(elane.astype(F32))
    s_row = pick(row_start + done_rows)
    s_val = pick(jnp.minimum(counts - done_rows, float(MOE_SBLK)))
    blk_ref[...] = jnp.where(elane == 0, s_exp, jnp.where(elane == 1, s_row, jnp.where(
        elane == 2, s_val, 0.0))).astype(I32)

    lane8 = lax.broadcasted_iota(I32, (8, LANES), 1)
    sub8 = lax.broadcasted_iota(I32, (8, LANES), 0)
    pad_row = jnp.where((cnt8 > whole) & (lane8 < N_EXPERTS), seg_start + whole, -1.0)
    n_used = jnp.broadcast_to(blk_end[:, N_EXPERTS - 1:N_EXPERTS], (8, LANES))
    misc_ref[...] = jnp.where(sub8 == 0, pad_row, jnp.where(sub8 == 1, n_used, 0.0)).astype(I32)


def _moe_meta(ids, n_blocks):
    t = ids.shape[0]
    assert MOE_ALIGN & (MOE_ALIGN - 1) == 0 and MOE_SBLK & (MOE_SBLK - 1) == 0
    return pl.pallas_call(
        functools.partial(_meta_kernel, n_tok=t, n_blocks=n_blocks),
        out_shape=[jax.ShapeDtypeStruct((t, LANES), I32),
                   jax.ShapeDtypeStruct((n_blocks, LANES), I32),
                   jax.ShapeDtypeStruct((8, LANES), I32)],
        scratch_shapes=[pltpu.VMEM((t, LANES), F32)],
        compiler_params=pltpu.CompilerParams(vmem_limit_bytes=VMEM_LIMIT),
        name="moe_meta",
    )(ids)


def _dispatch_kernel(dest_ref, misc_ref, x_ref, xs_ref, zbuf, sem, *, n_assign):
    tb = x_ref.shape[0]
    zrows = zbuf.shape[0]

    @pl.when(pl.program_id(0) == 0)
    def _():
        zbuf[...] = jnp.zeros_like(zbuf)

        def tail_copy(j):
            return pltpu.make_async_copy(zbuf, xs_ref.at[pl.ds(n_assign + j * zrows, zrows)], sem)

        n_tail = (xs_ref.shape[0] - n_assign) // zrows
        for j in range(n_tail):
            tail_copy(j).start()
        for j in range(n_tail):
            tail_copy(j).wait()

        def pad_copy(row):
            return pltpu.make_async_copy(zbuf.at[pl.ds(0, MOE_ALIGN)],
                                         xs_ref.at[pl.ds(row, MOE_ALIGN)], sem)

        def pads(fn):
            def body(e, _):
                row = misc_ref[0, e]

                @pl.when(row >= 0)
                def _():
                    fn(pad_copy(pl.multiple_of(row, MOE_ALIGN)))
                return 0
            lax.fori_loop(0, N_EXPERTS, body, 0)

        pads(lambda c: c.start())
        pads(lambda c: c.wait())

    def row_copy(r, d):
        return pltpu.make_async_copy(x_ref.at[pl.ds(r, 1)], xs_ref.at[pl.ds(d, 1)], sem)

    def scatters(fn):
        def body(g, _):
            r0 = pl.multiple_of(g * SUBLANES, SUBLANES)
            for q in range(SUBLANES):
                for j in range(2):
                    fn(row_copy(r0 + q, dest_ref[0, 0, 2 * (r0 + q) + j]))
            return 0
        lax.fori_loop(0, tb // SUBLANES, body, 0)

    scatters(lambda c: c.start())
    scatters(lambda c: c.wait())


def _dispatch(dest_blocks, misc, hn, xs_rows):
    t, d = hn.shape
    tb = DISPATCH_TB
    n_assign = 2 * t
    assert (xs_rows - n_assign) % MOE_SUB == 0
    return pl.pallas_call(
        functools.partial(_dispatch_kernel, n_assign=n_assign),
        grid=(t // tb,),
        in_specs=[pl.BlockSpec((1, 1, 2 * tb), lambda i: (i, 0, 0), memory_space=pltpu.SMEM),
                  pl.BlockSpec(memory_space=pltpu.SMEM),
                  pl.BlockSpec((tb, d), lambda i: (i, 0))],
        out_specs=pl.BlockSpec(memory_space=pl.ANY),
        out_shape=jax.ShapeDtypeStruct((xs_rows, d), hn.dtype),
        scratch_shapes=[pltpu.VMEM((MOE_SUB, d), hn.dtype), pltpu.SemaphoreType.DMA(())],
        compiler_params=_cparams(("arbitrary",)),
        name="moe_dispatch",
    )(dest_blocks, misc, hn)


def _expert_kernel(be_ref, rs_ref, nv_ref, nu_ref, xs_ref, wg_ref, wu_ref, wd_ref, ys_ref,
                   gacc, uacc, wg16, wu16, wd16, ybuf, sem, *, n_assign):
    del be_ref
    i = pl.program_id(0)
    k = pl.program_id(1)
    n_used = nu_ref[0]
    sub_shift = MOE_SUB.bit_length() - 1

    def out_copies(j, fn):
        row0 = rs_ref[j]
        rows = ((nv_ref[j] + (MOE_ALIGN - 1)) // MOE_ALIGN) * MOE_ALIGN
        n_full = rows >> sub_shift
        rem = rows & (MOE_SUB - 1)

        def piece_copy(off, size):
            src_off = off if isinstance(off, int) else pl.multiple_of(off, MOE_ALIGN)
            return pltpu.make_async_copy(
                ybuf.at[pl.ds(src_off, size)],
                ys_ref.at[pl.ds(pl.multiple_of(row0 + off, MOE_ALIGN), size)], sem)

        for s in range(MOE_SBLK // MOE_SUB):
            @pl.when(s < n_full)
            def _():
                fn(piece_copy(s * MOE_SUB, MOE_SUB))
        piece = MOE_SUB // 2
        while piece >= MOE_ALIGN:
            off = n_full * MOE_SUB + (rem & (MOE_SUB - 2 * piece))

            @pl.when((rem & piece) != 0)
            def _():
                fn(piece_copy(off, piece))
            piece //= 2

    @pl.when((i == 0) & (k == 0))
    def _():
        ybuf[...] = jnp.zeros_like(ybuf)

        def tail_copy(j):
            return pltpu.make_async_copy(ybuf, ys_ref.at[pl.ds(n_assign + j * MOE_SBLK, MOE_SBLK)], sem)

        n_tail = (ys_ref.shape[0] - n_assign) // MOE_SBLK
        for j in range(n_tail):
            tail_copy(j).start()
        for j in range(n_tail):
            tail_copy(j).wait()

    @pl.when(i < n_used)
    def _():
        n_sub = (nv_ref[i] + (MOE_SUB - 1)) >> sub_shift
        wg16[...] = wg_ref[...].astype(BF16)
        wu16[...] = wu_ref[...].astype(BF16)

        @pl.when(k == 0)
        def _():
            gacc[...] = jnp.zeros_like(gacc)
            uacc[...] = jnp.zeros_like(uacc)

        def up(r, _):
            rows = pl.ds(pl.multiple_of(r * MOE_SUB, MOE_SUB), MOE_SUB)
            x = jnp.concatenate(_unpack_bf16_pair(xs_ref[rows, :]), axis=1).astype(BF16)
            gacc[rows, :] += jnp.dot(x, wg16[...], preferred_element_type=F32)
            uacc[rows, :] += jnp.dot(x, wu16[...], preferred_element_type=F32)
            return 0

        lax.fori_loop(0, n_sub, up, 0)

        @pl.when(k == MOE_KS - 1)
        def _():
            wd16[...] = wd_ref[...].astype(BF16)

            @pl.when(i > 0)
            def _():
                out_copies(i - 1, lambda c: c.wait())

            def down(r, _):
                rows = pl.ds(pl.multiple_of(r * MOE_SUB, MOE_SUB), MOE_SUB)
                g = gacc[rows, :]
                hb = (g * _sigmoid(g)) * uacc[rows, :]
                y = jnp.dot(hb.astype(BF16), wd16[...], preferred_element_type=F32)
                half = y.shape[1] // 2
                ybuf[rows, :] = _pack_bf16_pair(y[:, :half], y[:, half:])
                return 0

            lax.fori_loop(0, n_sub, down, 0)
            out_copies(i, lambda c: c.start())

            @pl.when(i == n_used - 1)
            def _():
                out_copies(i, lambda c: c.wait())


def _experts(step_expert, step_row, step_valid, n_used, xs, w_gate, w_up, w_down, n_assign):
    xs_rows = xs.shape[0]
    n_blocks = step_expert.shape[0]
    d, f = w_gate.shape[-2:]
    tk = d // MOE_KS
    assert (xs_rows - n_assign) % MOE_SBLK == 0 and xs.shape[1] == d // 2

    def blk(i, nu):
        return jnp.minimum(i, nu[0] - 1)

    def kk(i, k, nu):
        return jnp.where(i < nu[0], k, MOE_KS - 1)

    grid_spec = pltpu.PrefetchScalarGridSpec(
        num_scalar_prefetch=4,
        grid=(n_blocks, MOE_KS),
        in_specs=[
            pl.BlockSpec((pl.Element(MOE_SBLK), pl.Element(tk // 2)),
                         lambda i, k, be, rs, nv, nu: (pl.multiple_of(rs[blk(i, nu)], MOE_ALIGN),
                                                       kk(i, k, nu) * (tk // 2))),
            pl.BlockSpec((None, tk, f), lambda i, k, be, rs, nv, nu: (be[blk(i, nu)], kk(i, k, nu), 0)),
            pl.BlockSpec((None, tk, f), lambda i, k, be, rs, nv, nu: (be[blk(i, nu)], kk(i, k, nu), 0)),
            pl.BlockSpec((None, f, d), lambda i, k, be, rs, nv, nu: (be[blk(i, nu)], 0, 0)),
        ],
        out_specs=pl.BlockSpec(memory_space=pl.ANY),
        scratch_shapes=[pltpu.VMEM((MOE_SBLK, f), F32), pltpu.VMEM((MOE_SBLK, f), F32),
                        pltpu.VMEM((tk, f), BF16), pltpu.VMEM((tk, f), BF16),
                        pltpu.VMEM((f, d), BF16), pltpu.VMEM((MOE_SBLK, d // 2), U32),
                        pltpu.SemaphoreType.DMA(())],
    )
    return pl.pallas_call(
        functools.partial(_expert_kernel, n_assign=n_assign),
        grid_spec=grid_spec,
        out_shape=jax.ShapeDtypeStruct((xs_rows, d // 2), U32),
        compiler_params=_cparams(("arbitrary", "arbitrary")),
        name="moe_experts",
    )(step_expert, step_row, step_valid, n_used, xs, w_gate, w_up, w_down)


def _combine_kernel(dest_ref, dest_next_ref, w_ref, h_ref, g_ref, ys_ref, o_ref, ybuf, sems, *,
                    final_norm):
    tb = h_ref.shape[0]
    i = pl.program_id(0)
    slot = i % 2

    def gathers(rows_ref, s, fn):
        def body(g, _):
            r0 = pl.multiple_of(g * SUBLANES, SUBLANES)
            for q in range(SUBLANES):
                for j in range(2):
                    fn(pltpu.make_async_copy(ys_ref.at[pl.ds(rows_ref[0, 0, 2 * (r0 + q) + j], 1)],
                                             ybuf.at[s, j, pl.ds(r0 + q, 1)], sems.at[s]), j)
            return 0
        lax.fori_loop(0, tb // SUBLANES, body, 0)

    @pl.when(i == 0)
    def _():
        gathers(dest_ref, 0, lambda c, p: c.start(priority=p))

    @pl.when(i + 1 < pl.num_programs(0))
    def _():
        gathers(dest_next_ref, 1 - slot, lambda c, p: c.start(priority=p))

    gathers(dest_ref, slot, lambda c, p: c.wait())

    w = w_ref[...]
    lo0, hi0 = _unpack_bf16_pair(ybuf[slot, 0])
    lo1, hi1 = _unpack_bf16_pair(ybuf[slot, 1])
    w0, w1 = w[:, 0:1], w[:, 1:2]
    h = h_ref[...] + jnp.concatenate([w0 * lo0 + w1 * lo1, w0 * hi0 + w1 * hi1], axis=1)
    if final_norm:
        h = (h * lax.rsqrt(jnp.mean(h * h, axis=-1, keepdims=True) + EPS)) * g_ref[...]
    o_ref[...] = h


def _combine(dest_blocks, wts, h, g, ys, final_norm):
    t, d = h.shape
    tb = COMBINE_TB
    n_tiles = t // tb
    return pl.pallas_call(
        functools.partial(_combine_kernel, final_norm=final_norm),
        grid=(n_tiles,),
        in_specs=[pl.BlockSpec((1, 1, 2 * tb), lambda i: (i, 0, 0), memory_space=pltpu.SMEM),
                  pl.BlockSpec((1, 1, 2 * tb), lambda i: (jnp.minimum(i + 1, n_tiles - 1), 0, 0),
                               memory_space=pltpu.SMEM),
                  pl.BlockSpec((tb, LANES), lambda i: (i, 0)),
                  pl.BlockSpec((tb, d), lambda i: (i, 0)),
                  pl.BlockSpec((1, d), lambda i: (0, 0)),
                  pl.BlockSpec(memory_space=pl.ANY)],
        out_specs=pl.BlockSpec((tb, d), lambda i: (i, 0)),
        out_shape=jax.ShapeDtypeStruct((t, d), F32),
        scratch_shapes=[pltpu.VMEM((2, 2, tb, d // 2), U32), pltpu.SemaphoreType.DMA((2,))],
        compiler_params=_cparams(("arbitrary",)),
        name="moe_combine",
    )(dest_blocks, dest_blocks, wts, h, g.reshape(1, d).astype(F32), ys)


def kernel(x, mem, norm_mix, w_in, conv_qk, b_gates, g_mlstm, w_proj_a, w_proj_b, w_out,
           norm_xattn, norm_mem, w_q_mem, w_kv_mem, w_o_mem, norm_moe,
           w_router_group, b_router_group, w_router_expert, b_router_expert,
           w_gate, w_up, w_down, norm_final):
    b, s, d = x.shape
    t = b * s
    n_mem = mem.shape[1]
    depth = w_in.shape[0]
    ml_qk_w = ML_HEADS * ML_QK
    ml_v_w = ML_HEADS * ML_V
    sb_w = SB_HEADS * SB_HD
    seg_a_w = 2 * ml_qk_w + 2 * ml_v_w
    n_gate_cols = 2 * ML_HEADS
    seg_b_w = 3 * sb_w + 2 * d
    n_assign = 2 * t
    n_blocks = n_assign // MOE_SBLK + N_EXPERTS
    xs_rows = n_assign + N_EXPERTS * MOE_ALIGN + MOE_SBLK

    h = x.reshape(t, d)
    mem2 = mem.reshape(b * n_mem, d)
    for l in range(depth):
        xn = _rmsnorm(h, norm_mix[l], BF16)
        w_in_t = jnp.swapaxes(w_in[l], 0, 1)
        seg_a = _matmul(xn, w_in_t, seg_a_w, BF16, transposed=True, name="in_proj_a")
        gates = _matmul(xn, w_in_t, LANES, F32, tn=LANES, col_start=seg_a_w, transposed=True,
                        name="in_proj_gates")
        seg_b = _matmul(xn, w_in_t, seg_b_w, BF16, col_start=seg_a_w + n_gate_cols, transposed=True,
                        name="in_proj_b")
        gates_t = gates[:, :n_gate_cols].reshape(b, s, 2, ML_HEADS).transpose(0, 2, 3, 1)
        gates_t = gates_t.reshape(b, 2, ML_HEADS, s // ML_CHUNK, ML_CHUNK)
        hm = _mlstm(seg_a.reshape(b, s, seg_a_w), gates_t, conv_qk[l], b_gates[l], g_mlstm[l])
        hs = _stick_breaking(seg_b.reshape(b, s, seg_b_w))
        y = _merge(hm.reshape(t, ml_v_w), hs.reshape(t, sb_w), w_proj_a[l], w_proj_b[l],
                   seg_b, 3 * sb_w, 3 * sb_w + d, d)
        h = _matmul(y, w_out[l], d, F32, res=h, name="out_proj")
        hn = _rmsnorm(h, norm_xattn[l], BF16)
        q = _matmul(hn, w_q_mem[l], d, BF16, name="xattn_q")
        memn = _rmsnorm(mem2, norm_mem[l], BF16)
        kv = _matmul(memn, w_kv_mem[l], 2 * d, BF16, name="xattn_kv")
        o = _xattn(q.reshape(b, s, d), kv.reshape(b, n_mem, 2 * d))
        h = _matmul(o.reshape(t, d), w_o_mem[l], d, F32, res=h, name="xattn_o")
        w_r = jnp.pad(jnp.concatenate([w_router_group[l], w_router_expert[l]], axis=1),
                      ((0, 0), (0, LANES - N_GROUPS - N_EXPERTS)))
        b_r = jnp.pad(jnp.concatenate([b_router_group[l], b_router_expert[l]]),
                      (0, LANES - N_GROUPS - N_EXPERTS)).reshape(1, LANES).astype(F32)
        hn3, ids, wts = _router(h, norm_moe[l], w_r, b_r)
        dest, steps, misc = _moe_meta(ids, n_blocks)
        xs = _dispatch(dest[:, :2].reshape(t // DISPATCH_TB, 1, 2 * DISPATCH_TB), misc, hn3, xs_rows)
        ys = _experts(steps[:, 0], steps[:, 1], steps[:, 2], misc[1, :1], xs,
                      w_gate[l], w_up[l], w_down[l], n_assign)
        h = _combine(dest[:, :2].reshape(t // COMBINE_TB, 1, 2 * COMBINE_TB), wts, h,
                     norm_final, ys, final_norm=(l == depth - 1))
    return h.reshape(b, s, d)
```
